```python
import math
import jax, jax.numpy as jnp
from jax import lax
import numpy as np

D_MODEL = 2048
BATCH = 4
SEQ = 8192
DEPTH = 1

HEAD_DIM = 128
HEADS_A = D_MODEL // 256
KV_HEADS_A = HEADS_A // 4
HEADS_B = D_MODEL // 256
KV_HEADS_B = HEADS_B // 4
Q_WIDTH_A = HEADS_A * HEAD_DIM
KV_WIDTH_A = KV_HEADS_A * HEAD_DIM
Q_WIDTH_B = HEADS_B * HEAD_DIM
KV_WIDTH_B = KV_HEADS_B * HEAD_DIM
IN_WIDTH = Q_WIDTH_A + 2 * KV_WIDTH_A + Q_WIDTH_B + 2 * KV_WIDTH_B + 2 * D_MODEL
BLOCK = 128
WINDOW = 128
GRID_W = 64
ROPE_THETA = 10000.0
NUM_BUCKETS = 32
MAX_DISTANCE = 128
N_EXPERTS = 16
CAPACITY_FACTOR = 2
D_FF_EXPERT = D_MODEL // 2
EPS = 1e-6
NEG = -1e30

kernel_name = "hybrid_gated_gqa_window_ecmoe"


def rmsnorm(x, g):
    xf = x.astype(jnp.float32)
    y = xf * lax.rsqrt(jnp.mean(xf * xf, axis=-1, keepdims=True) + EPS)
    return (y * g.astype(jnp.float32)).astype(x.dtype)


def rotate_half(xh, cos, sin):
    n = xh.shape[-1] // 2
    x1, x2 = xh[..., :n], xh[..., n:]
    return jnp.concatenate([x1 * cos - x2 * sin, x2 * cos + x1 * sin], axis=-1)


def axial_rope(x, seq_len):
    rows = seq_len // GRID_W
    r = jnp.repeat(jnp.arange(rows, dtype=jnp.float32), GRID_W)
    c = jnp.tile(jnp.arange(GRID_W, dtype=jnp.float32), rows)
    half = HEAD_DIM // 2
    inv = 1.0 / (ROPE_THETA ** (jnp.arange(0, half, 2, dtype=jnp.float32) / half))
    ang_r = r[:, None, None] * inv
    ang_c = c[:, None, None] * inv
    xr = rotate_half(x[..., :half], jnp.cos(ang_r), jnp.sin(ang_r))
    xc = rotate_half(x[..., half:], jnp.cos(ang_c), jnp.sin(ang_c))
    return jnp.concatenate([xr, xc], axis=-1)


def t5_bucket(rel):
    half = NUM_BUCKETS // 2
    ret = jnp.where(rel > 0, half, 0)
    n = jnp.abs(rel)
    max_exact = half // 2
    nf = jnp.maximum(n, 1).astype(jnp.float32)
    large = max_exact + (jnp.log(nf / max_exact) / math.log(MAX_DISTANCE / max_exact)
                         * (half - max_exact)).astype(jnp.int32)
    large = jnp.minimum(large, half - 1)
    return ret + jnp.where(n < max_exact, n, large)


def global_axial_gqa(q, k, v, qn, kn):
    bsz, s, _, hd = q.shape
    g = HEADS_A // KV_HEADS_A
    nb = s // BLOCK
    q = axial_rope(rmsnorm(q.astype(jnp.float32), qn), s)
    k = axial_rope(rmsnorm(k.astype(jnp.float32), kn), s)
    v = v.astype(jnp.float32)
    scale = 1.0 / math.sqrt(hd)
    qblocks = q.reshape(bsz, nb, BLOCK, KV_HEADS_A, g, hd).transpose(1, 0, 3, 4, 2, 5)
    kt = k.transpose(0, 2, 1, 3)
    vt = v.transpose(0, 2, 1, 3)

    def one_block(qb):
        sc = jnp.einsum('bkgqd,bksd->bkgqs', qb, kt) * scale
        p = jax.nn.softmax(sc, axis=-1)
        return jnp.einsum('bkgqs,bksd->bkgqd', p, vt)

    o = lax.map(one_block, qblocks)
    return o.transpose(1, 0, 4, 2, 3, 5).reshape(bsz, s, HEADS_A * hd)


def window_sink_gqa(q, k, v, sink, rel_bias):
    bsz, s, _, hd = q.shape
    g = HEADS_B // KV_HEADS_B
    nb = s // BLOCK
    scale = 1.0 / math.sqrt(hd)
    q = q.astype(jnp.float32).reshape(bsz, nb, BLOCK, KV_HEADS_B, g, hd)
    pad = ((0, 0), (WINDOW, WINDOW), (0, 0), (0, 0))
    kp = jnp.pad(k.astype(jnp.float32), pad).reshape(bsz, nb + 2, BLOCK, KV_HEADS_B, hd)
    vp = jnp.pad(v.astype(jnp.float32), pad).reshape(bsz, nb + 2, BLOCK, KV_HEADS_B, hd)
    kband = jnp.concatenate([kp[:, :-2], kp[:, 1:-1], kp[:, 2:]], axis=2)
    vband = jnp.concatenate([vp[:, :-2], vp[:, 1:-1], vp[:, 2:]], axis=2)
    sc = jnp.einsum('bnqkgd,bnjkd->bnkgqj', q, kband) * scale
    rel = (jnp.arange(3 * BLOCK) - BLOCK)[None, :] - jnp.arange(BLOCK)[:, None]
    bias = rel_bias.astype(jnp.float32)[t5_bucket(rel)]
    bias = bias.transpose(2, 0, 1).reshape(KV_HEADS_B, g, BLOCK, 3 * BLOCK)
    kpos = jnp.arange(nb)[:, None] * BLOCK - BLOCK + jnp.arange(3 * BLOCK)[None, :]
    valid = (jnp.abs(rel) <= WINDOW)[None] & ((kpos >= 0) & (kpos < s))[:, None, :]
    valid = valid[None, :, None, None]
    sc = jnp.where(valid, sc + bias, NEG)
    sk = sink.astype(jnp.float32).reshape(KV_HEADS_B, g, 1, 1)
    m = jnp.maximum(jnp.max(sc, axis=-1, keepdims=True), sk)
    e = jnp.exp(sc - m)
    p = e / (jnp.sum(e, axis=-1, keepdims=True) + jnp.exp(sk - m))
    o = jnp.einsum('bnkgqj,bnjkd->bnqkgd', p, vband)
    return o.reshape(bsz, s, HEADS_B * hd)


def expert_choice_moe(h, w_router, w_gate_e, w_up_e, w_down_e):
    bsz, s, d = h.shape
    cap = CAPACITY_FACTOR * s // N_EXPERTS
    logits = jnp.einsum('bsd,de->bse', h, w_router).astype(jnp.float32)
    aff = jax.nn.softmax(logits, axis=-1).transpose(0, 2, 1)
    gates, idx = lax.top_k(aff, cap)
    xin = jax.vmap(lambda hb, ib: hb[ib])(h, idx)
    a = jnp.einsum('becd,edf->becf', xin, w_gate_e)
    u = jnp.einsum('becd,edf->becf', xin, w_up_e)
    y = jnp.einsum('becf,efd->becd', jax.nn.silu(a) * u, w_down_e)
    y = y * gates[..., None].astype(y.dtype)

    def scatter(ib, yb):
        return jnp.zeros((s, d), yb.dtype).at[ib.reshape(-1)].add(yb.reshape(-1, d))

    return jax.vmap(scatter)(idx, y)


def setup_inputs(seed: int = 0) -> dict:
    key = jax.random.key(seed)
    ks = jax.random.split(key, 20)
    f32 = jnp.float32
    nrm = lambda k, shape, sc: jax.random.normal(k, shape, f32) * sc
    return {
        "x": nrm(ks[0], (BATCH, SEQ, D_MODEL), 1.0),
        "g_mix": 1.0 + nrm(ks[1], (DEPTH, D_MODEL), 0.05),
        "w_in": nrm(ks[2], (DEPTH, D_MODEL, IN_WIDTH), D_MODEL ** -0.5),
        "b_gate": nrm(ks[3], (DEPTH, 2 * D_MODEL), 0.1),
        "qn_a": 1.0 + nrm(ks[4], (DEPTH, HEAD_DIM), 0.05),
        "kn_a": 1.0 + nrm(ks[5], (DEPTH, HEAD_DIM), 0.05),
        "w_proj_a": nrm(ks[6], (DEPTH, Q_WIDTH_A, D_MODEL), Q_WIDTH_A ** -0.5),
        "sink_b": nrm(ks[7], (DEPTH, HEADS_B), 0.5),
        "rel_bias": nrm(ks[8], (NUM_BUCKETS, HEADS_B), 0.5),
        "w_proj_b": nrm(ks[9], (DEPTH, Q_WIDTH_B, D_MODEL), Q_WIDTH_B ** -0.5),
        "w_o": nrm(ks[10], (DEPTH, D_MODEL, D_MODEL), D_MODEL ** -0.5),
        "g_ffn": 1.0 + nrm(ks[11], (DEPTH, D_MODEL), 0.05),
        "w_router": nrm(ks[12], (DEPTH, D_MODEL, N_EXPERTS), D_MODEL ** -0.5),
        "w_gate_e": nrm(ks[13], (DEPTH, N_EXPERTS, D_MODEL, D_FF_EXPERT), D_MODEL ** -0.5),
        "w_up_e": nrm(ks[14], (DEPTH, N_EXPERTS, D_MODEL, D_FF_EXPERT), D_MODEL ** -0.5),
        "w_down_e": nrm(ks[15], (DEPTH, N_EXPERTS, D_FF_EXPERT, D_MODEL), D_FF_EXPERT ** -0.5),
        "g_final": 1.0 + nrm(ks[16], (D_MODEL,), 0.05),
    }


def reference(x, g_mix, w_in, b_gate, qn_a, kn_a, w_proj_a, sink_b, rel_bias, w_proj_b,
              w_o, g_ffn, w_router, w_gate_e, w_up_e, w_down_e, g_final):
    bsz, s, d = x.shape
    splits = np.cumsum([Q_WIDTH_A, KV_WIDTH_A, KV_WIDTH_A, Q_WIDTH_B, KV_WIDTH_B, KV_WIDTH_B, D_MODEL]).tolist()
    for l in range(DEPTH):
        h = rmsnorm(x, g_mix[l])
        proj = jnp.einsum('bsd,dn->bsn', h, w_in[l])
        qa, ka, va, qb, kb, vb, ga, gb = jnp.split(proj, splits, axis=-1)
        oa = global_axial_gqa(qa.reshape(bsz, s, HEADS_A, HEAD_DIM),
                              ka.reshape(bsz, s, KV_HEADS_A, HEAD_DIM),
                              va.reshape(bsz, s, KV_HEADS_A, HEAD_DIM), qn_a[l], kn_a[l]).astype(x.dtype)
        ob = window_sink_gqa(qb.reshape(bsz, s, HEADS_B, HEAD_DIM),
                             kb.reshape(bsz, s, KV_HEADS_B, HEAD_DIM),
                             vb.reshape(bsz, s, KV_HEADS_B, HEAD_DIM), sink_b[l], rel_bias).astype(x.dtype)
        gate_a = jax.nn.sigmoid(ga + b_gate[l, :D_MODEL])
        gate_b = jax.nn.sigmoid(gb + b_gate[l, D_MODEL:])
        merged = gate_a * jnp.einsum('bsc,cd->bsd', oa, w_proj_a[l]) \
            + gate_b * jnp.einsum('bsc,cd->bsd', ob, w_proj_b[l])
        x = x + jnp.einsum('bsd,de->bse', merged, w_o[l])
        x = x + expert_choice_moe(rmsnorm(x, g_ffn[l]), w_router[l], w_gate_e[l], w_up_e[l], w_down_e[l])
    return rmsnorm(x, g_final)
```

```python
import functools
import math

import jax
import jax.numpy as jnp
from jax import lax
from jax.experimental import pallas as pl
from jax.experimental.pallas import tpu as pltpu

F32 = jnp.float32
BF16 = jnp.bfloat16
I32 = jnp.int32
U32 = jnp.uint32

HEAD_DIM = 128
GROUP = 4
BLOCK = 128
GRID_W = 64
ROPE_THETA = 10000.0
NUM_BUCKETS = 32
MAX_DISTANCE = 128
N_EXPERTS = 16
CAPACITY_FACTOR = 2
EPS = 1e-6
NEG = -1e30
LANES = 128
SUBLANES = 8
VMEM_LIMIT = 56 * 1024 * 1024
HIGHEST = lax.Precision.HIGHEST
NT = (((1,), (1,)), ((), ()))
TN = (((0,), (0,)), ((), ()))


def _params(n_axes):
    return pltpu.CompilerParams(dimension_semantics=("arbitrary",) * n_axes,
                                vmem_limit_bytes=VMEM_LIMIT)


def _inproj_kernel(x_ref, g_ref, w_ref, o_ref, h_ref):
    @pl.when(pl.program_id(1) == 0)
    def _():
        x = x_ref[...]
        ms = jnp.mean(x * x, axis=-1, keepdims=True)
        h_ref[...] = (x * lax.rsqrt(ms + EPS) * g_ref[...]).astype(BF16)

    o_ref[...] = jnp.dot(h_ref[...], w_ref[...], preferred_element_type=F32).astype(o_ref.dtype)


def _inproj(x2d, g, w, tm, tn):
    t, d = x2d.shape
    n = w.shape[1]
    return pl.pallas_call(
        _inproj_kernel,
        grid=(t // tm, n // tn),
        in_specs=[pl.BlockSpec((tm, d), lambda i, j: (i, 0)),
                  pl.BlockSpec((1, d), lambda i, j: (0, 0)),
                  pl.BlockSpec((d, tn), lambda i, j: (0, j))],
        out_specs=pl.BlockSpec((tm, tn), lambda i, j: (i, j)),
        out_shape=jax.ShapeDtypeStruct((t, n), BF16),
        scratch_shapes=[pltpu.VMEM((tm, d), BF16)],
        compiler_params=_params(2),
        name="inproj",
    )(x2d, g, w)


def _norm_rope(x, gain, cos, sin_signed):
    ms = jnp.mean(x * x, axis=-1, keepdims=True)
    y = x * lax.rsqrt(ms + EPS) * gain
    lane = lax.broadcasted_iota(I32, y.shape, 1)
    swapped = jnp.where((lane & 32) == 0, pltpu.roll(y, 96, axis=1), pltpu.roll(y, 32, axis=1))
    return y * cos + swapped * sin_signed


def _attn_a_kernel(q_ref, k_ref, v_ref, cq_ref, sq_ref, ck_ref, sk_ref, qn_ref, kn_ref, o_ref,
                   kp_ref, qs_ref, m_ref, l_ref, acc_ref, *, tq, tk, kch):
    s_len = k_ref.shape[0]

    @pl.when(pl.program_id(2) == 0)
    def _():
        def body(c, carry):
            off = pl.multiple_of(c * kch, kch)
            kk = k_ref[pl.ds(off, kch), :].astype(F32)
            kk = _norm_rope(kk, kn_ref[...], ck_ref[pl.ds(off, kch), :], sk_ref[pl.ds(off, kch), :])
            kp_ref[pl.ds(off, kch), :] = kk.astype(BF16)
            return carry
        lax.fori_loop(0, s_len // kch, body, 0)

    scale = 1.0 / math.sqrt(HEAD_DIM)
    for g in range(GROUP):
        qg = q_ref[:, g * HEAD_DIM:(g + 1) * HEAD_DIM].astype(F32)
        qg = _norm_rope(qg, qn_ref[...], cq_ref[...], sq_ref[...]) * scale
        qs_ref[g * tq:(g + 1) * tq, :] = qg.astype(BF16)

    m_ref[...] = jnp.full(m_ref.shape, -jnp.inf, F32)
    l_ref[...] = jnp.zeros(l_ref.shape, F32)
    acc_ref[...] = jnp.zeros(acc_ref.shape, F32)

    def kv_body(c, carry):
        off = pl.multiple_of(c * tk, tk)
        s = lax.dot_general(qs_ref[...], kp_ref[pl.ds(off, tk), :], NT, preferred_element_type=F32)
        m_prev = m_ref[...]
        m_new = jnp.maximum(m_prev, jnp.max(s, axis=-1, keepdims=True))
        alpha = jnp.exp(m_prev - m_new)
        p = jnp.exp(s - m_new)
        l_ref[...] = alpha * l_ref[...] + jnp.sum(p, axis=-1, keepdims=True)
        acc_ref[...] = alpha * acc_ref[...] + jnp.dot(p.astype(BF16), v_ref[pl.ds(off, tk), :],
                                                      preferred_element_type=F32)
        m_ref[...] = m_new
        return carry
    lax.fori_loop(0, s_len // tk, kv_body, 0)

    o = acc_ref[...] / l_ref[...]
    for g in range(GROUP):
        o_ref[:, g * HEAD_DIM:(g + 1) * HEAD_DIM] = o[g * tq:(g + 1) * tq, :].astype(o_ref.dtype)


def _attn_a(proj, cos, sin, qn, kn, bsz, s_len, d, tq, tk):
    q_w = d // 2
    kv_w = d // 8
    kvh = kv_w // HEAD_DIM
    gw = GROUP * HEAD_DIM
    nq = s_len // tq
    k_blk = q_w // HEAD_DIM
    v_blk = (q_w + kv_w) // HEAD_DIM
    r = GROUP * tq
    kern = functools.partial(_attn_a_kernel, tq=tq, tk=tk, kch=min(512, s_len))
    return pl.pallas_call(
        kern,
        grid=(bsz, kvh, nq),
        in_specs=[pl.BlockSpec((tq, gw), lambda b, k, i: (b * nq + i, k)),
                  pl.BlockSpec((s_len, HEAD_DIM), lambda b, k, i: (b, k_blk + k)),
                  pl.BlockSpec((s_len, HEAD_DIM), lambda b, k, i: (b, v_blk + k)),
                  pl.BlockSpec((tq, HEAD_DIM), lambda b, k, i: (i, 0)),
                  pl.BlockSpec((tq, HEAD_DIM), lambda b, k, i: (i, 0)),
                  pl.BlockSpec((s_len, HEAD_DIM), lambda b, k, i: (0, 0)),
                  pl.BlockSpec((s_len, HEAD_DIM), lambda b, k, i: (0, 0)),
                  pl.BlockSpec((1, HEAD_DIM), lambda b, k, i: (0, 0)),
                  pl.BlockSpec((1, HEAD_DIM), lambda b, k, i: (0, 0))],
        out_specs=pl.BlockSpec((tq, gw), lambda b, k, i: (b * nq + i, k)),
        out_shape=jax.ShapeDtypeStruct((bsz * s_len, q_w), BF16),
        scratch_shapes=[pltpu.VMEM((s_len, HEAD_DIM), BF16),
                        pltpu.VMEM((r, HEAD_DIM), BF16),
                        pltpu.VMEM((r, 1), F32),
                        pltpu.VMEM((r, 1), F32),
                        pltpu.VMEM((r, HEAD_DIM), F32)],
        compiler_params=_params(3),
        name="attn_a",
    )(proj, proj, proj, cos, sin, cos, sin, qn, kn)


def _attn_b_kernel(relb_ref, sink_ref, bucket_ref, q_ref, kp_ref, kc_ref, kn_ref, vp_ref, vc_ref, vn_ref,
                   o_ref, bias_ref, *, tq, s_len, kvh):
    b = pl.program_id(0)
    k = pl.program_id(1)
    i = pl.program_id(2)
    band = 3 * BLOCK

    @pl.when((b == 0) & (k == 0) & (i == 0))
    def _():
        bucket = bucket_ref[...]
        col = lax.broadcasted_iota(I32, (BLOCK, band), 1)
        row = lax.broadcasted_iota(I32, (BLOCK, band), 0)
        in_window = jnp.abs(col - BLOCK - row) <= BLOCK
        for kk in range(kvh):
            for g in range(GROUP):
                h = kk * GROUP + g
                tab = jnp.zeros((BLOCK, band), F32)
                for bkt in range(NUM_BUCKETS):
                    tab = jnp.where(bucket == bkt, relb_ref[bkt, h], tab)
                bias_ref[kk, g * BLOCK:(g + 1) * BLOCK, :] = jnp.where(in_window, tab, NEG)

    kcat = jnp.concatenate([kp_ref[...], kc_ref[...], kn_ref[...]], axis=0)
    vcat = jnp.concatenate([vp_ref[...], vc_ref[...], vn_ref[...]], axis=0)
    bias = bias_ref[k]
    sink = jnp.concatenate([jnp.full((BLOCK, 1), sink_ref[k * GROUP + g], F32) for g in range(GROUP)], axis=0)
    scale = 1.0 / math.sqrt(HEAD_DIM)
    nsub = tq // BLOCK
    for jb in range(nsub):
        rows = slice(jb * BLOCK, (jb + 1) * BLOCK)
        q4 = jnp.concatenate([q_ref[rows, g * HEAD_DIM:(g + 1) * HEAD_DIM] for g in range(GROUP)], axis=0)
        kb = kcat[jb * BLOCK:jb * BLOCK + band]
        vb = vcat[jb * BLOCK:jb * BLOCK + band]
        s = lax.dot_general(q4, kb, NT, preferred_element_type=F32) * scale + bias
        if jb == 0 or jb == nsub - 1:
            kpos = i * tq + (jb - 1) * BLOCK + lax.broadcasted_iota(I32, (1, band), 1)
            s = jnp.where((kpos >= 0) & (kpos < s_len), s, NEG)
        m = jnp.maximum(jnp.max(s, axis=-1, keepdims=True), sink)
        e = jnp.exp(s - m)
        den = jnp.sum(e, axis=-1, keepdims=True) + jnp.exp(sink - m)
        o = jnp.dot(e.astype(BF16), vb, preferred_element_type=F32) / den
        for g in range(GROUP):
            o_ref[rows, g * HEAD_DIM:(g + 1) * HEAD_DIM] = o[g * BLOCK:(g + 1) * BLOCK, :].astype(o_ref.dtype)


def _attn_b(proj, bucket, rel_bias, sink, bsz, s_len, d, tq):
    q_w = d // 2
    kv_w = d // 8
    kvh = kv_w // HEAD_DIM
    gw = GROUP * HEAD_DIM
    nq = s_len // tq
    sub = tq // BLOCK
    nblk = s_len // BLOCK
    q_blk = (q_w + 2 * kv_w) // gw
    k_blk = (2 * q_w + 2 * kv_w) // HEAD_DIM
    v_blk = (2 * q_w + 3 * kv_w) // HEAD_DIM

    def prev_map(col):
        return lambda b, k, i: (b * nblk + jnp.maximum(i * sub - 1, 0), col + k)

    def cur_map(col):
        return lambda b, k, i: (b * nq + i, col + k)

    def next_map(col):
        return lambda b, k, i: (b * nblk + jnp.minimum((i + 1) * sub, nblk - 1), col + k)

    small = (BLOCK, HEAD_DIM)
    kern = functools.partial(_attn_b_kernel, tq=tq, s_len=s_len, kvh=kvh)
    return pl.pallas_call(
        kern,
        grid=(bsz, kvh, nq),
        in_specs=[pl.BlockSpec(memory_space=pltpu.SMEM),
                  pl.BlockSpec(memory_space=pltpu.SMEM),
                  pl.BlockSpec((BLOCK, 3 * BLOCK), lambda b, k, i: (0, 0)),
                  pl.BlockSpec((tq, gw), lambda b, k, i: (b * nq + i, q_blk + k)),
                  pl.BlockSpec(small, prev_map(k_blk)),
                  pl.BlockSpec((tq, HEAD_DIM), cur_map(k_blk)),
                  pl.BlockSpec(small, next_map(k_blk)),
                  pl.BlockSpec(small, prev_map(v_blk)),
                  pl.BlockSpec((tq, HEAD_DIM), cur_map(v_blk)),
                  pl.BlockSpec(small, next_map(v_blk))],
        out_specs=pl.BlockSpec((tq, gw), lambda b, k, i: (b * nq + i, k)),
        out_shape=jax.ShapeDtypeStruct((bsz * s_len, q_w), BF16),
        scratch_shapes=[pltpu.VMEM((kvh, GROUP * BLOCK, 3 * BLOCK), F32)],
        compiler_params=_params(3),
        name="attn_b",
    )(rel_bias, sink, bucket, proj, proj, proj, proj, proj, proj, proj)


def _outproj_kernel(oa_ref, ob_ref, ga0_ref, ga1_ref, gb0_ref, gb1_ref, x_ref, bg_ref, wpa_ref, wpb_ref,
                    wo_ref, gf_ref, wr_ref, x1_ref, h2p_ref, aff_ref):
    d = x_ref.shape[1]
    half = d // 2
    pa = jnp.dot(oa_ref[...], wpa_ref[...], preferred_element_type=F32)
    pb = jnp.dot(ob_ref[...], wpb_ref[...], preferred_element_type=F32)
    ga = jnp.concatenate([ga0_ref[...], ga1_ref[...]], axis=1).astype(F32) + bg_ref[:, :d]
    gb = jnp.concatenate([gb0_ref[...], gb1_ref[...]], axis=1).astype(F32) + bg_ref[:, d:]
    merged = jax.nn.sigmoid(ga) * pa + jax.nn.sigmoid(gb) * pb
    x1 = x_ref[...] + jnp.dot(merged.astype(BF16), wo_ref[...], preferred_element_type=F32)
    x1_ref[...] = x1
    ms = jnp.mean(x1 * x1, axis=-1, keepdims=True)
    h = x1 * lax.rsqrt(ms + EPS) * gf_ref[...]
    bits = pltpu.bitcast(h.astype(BF16).astype(F32), U32)
    h2p_ref[...] = (bits[:, :half] >> 16) | (bits[:, half:] & jnp.uint32(0xFFFF0000))
    logits = lax.dot_general(wr_ref[...], h, NT, precision=HIGHEST, preferred_element_type=F32)
    ex = jnp.exp(logits - jnp.max(logits, axis=0, keepdims=True))
    aff_ref[...] = ex / jnp.sum(ex, axis=0, keepdims=True)


def _outproj(oa, ob, proj, x2d, b_gate, wpa, wpb, wo, g_ffn, wr_t, bsz, s_len, tm):
    t, d = x2d.shape
    half = d // 2
    nt = s_len // tm
    row = lambda c: (lambda i: (i, c))
    const = lambda i: (0, 0)
    return pl.pallas_call(
        _outproj_kernel,
        grid=(t // tm,),
        in_specs=[pl.BlockSpec((tm, half), row(0)),
                  pl.BlockSpec((tm, half), row(0)),
                  pl.BlockSpec((tm, half), row(3)),
                  pl.BlockSpec((tm, half), row(4)),
                  pl.BlockSpec((tm, half), row(5)),
                  pl.BlockSpec((tm, half), row(6)),
                  pl.BlockSpec((tm, d), row(0)),
                  pl.BlockSpec((1, 2 * d), const),
                  pl.BlockSpec((half, d), const),
                  pl.BlockSpec((half, d), const),
                  pl.BlockSpec((d, d), const),
                  pl.BlockSpec((1, d), const),
                  pl.BlockSpec((N_EXPERTS, d), const)],
        out_specs=[pl.BlockSpec((tm, d), row(0)),
                   pl.BlockSpec((tm, half), row(0)),
                   pl.BlockSpec((None, N_EXPERTS, tm), lambda i: (i // nt, 0, i % nt))],
        out_shape=[jax.ShapeDtypeStruct((t, d), F32),
                   jax.ShapeDtypeStruct((t, half), U32),
                   jax.ShapeDtypeStruct((bsz, N_EXPERTS, s_len), F32)],
        compiler_params=_params(1),
        name="outproj",
    )(oa, ob, proj, proj, proj, proj, x2d, b_gate, wpa, wpb, wo, g_ffn, wr_t)


def _topk_kernel(a_ref, idx_ref, gate_ref, cumx_ref, sel_ref, *, cap):
    a = a_ref[...]
    n_e, n_r, _ = a.shape
    bits = pltpu.bitcast(a, I32)

    def total(x):
        return jnp.sum(jnp.sum(x, axis=1, keepdims=True), axis=2, keepdims=True)

    def search(_, carry):
        lo, hi = carry
        mid = lo + ((hi - lo) >> 1)
        enough = total(jnp.where(bits >= mid, 1.0, 0.0)) >= cap
        return jnp.where(enough, mid, lo), jnp.where(enough, hi, mid)

    lo0 = jnp.zeros((n_e, 1, 1), I32)
    hi0 = jnp.full((n_e, 1, 1), 0x7F800000, I32)
    thr, _ = lax.fori_loop(0, 31, search, (lo0, hi0))

    ri = lax.broadcasted_iota(I32, (LANES, LANES), 0)
    ci = lax.broadcasted_iota(I32, (LANES, LANES), 1)
    upper = jnp.where(ri <= ci, 1.0, 0.0).astype(BF16)
    ones = jnp.ones((LANES, LANES), BF16)
    rr = lax.broadcasted_iota(I32, (n_r, n_r), 0)
    rc = lax.broadcasted_iota(I32, (n_r, n_r), 1)
    strict_lower = jnp.where(rc < rr, 1.0, 0.0).astype(BF16)

    def prefix(x):
        x2 = x.reshape(n_e * n_r, LANES).astype(BF16)
        in_row = jnp.dot(x2, upper, preferred_element_type=F32).reshape(n_e, n_r, LANES)
        row_tot = jnp.dot(x2, ones, preferred_element_type=F32).reshape(n_e, n_r, LANES)
        before = jnp.stack([jnp.dot(strict_lower, row_tot[e].astype(BF16), preferred_element_type=F32)
                            for e in range(n_e)], axis=0)
        return in_row + before

    above = jnp.where(bits > thr, 1.0, 0.0)
    tied = jnp.where(bits == thr, 1.0, 0.0)
    need = cap - total(above)
    sel = above + tied * jnp.where(prefix(tied) <= need, 1.0, 0.0)
    cum = prefix(sel)
    sel_ref[...] = sel
    cumx_ref[...] = cum - sel

    slot = lax.broadcasted_iota(I32, (1, cap), 1).astype(F32)
    row_id = lax.broadcasted_iota(I32, (n_r, 1), 0).astype(F32)
    lane_id = lax.broadcasted_iota(I32, (LANES, 1), 0).astype(F32)
    for e in range(n_e):
        c = cum[e]
        row_end = c[:, LANES - 1:LANES]
        srow = jnp.sum(jnp.where(row_end <= slot, 1.0, 0.0), axis=0, keepdims=True)
        pick = jnp.where(row_id == srow, 1.0, 0.0)
        c_row = lax.dot_general(c, pick, TN, precision=HIGHEST, preferred_element_type=F32)
        slane = jnp.sum(jnp.where(c_row <= slot, 1.0, 0.0), axis=0, keepdims=True)
        a_row = lax.dot_general(a[e], pick, TN, precision=HIGHEST, preferred_element_type=F32)
        idx_ref[e:e + 1, :] = (srow * LANES + slane).astype(I32)
        gate_ref[e:e + 1, :] = jnp.sum(jnp.where(lane_id == slane, a_row, 0.0), axis=0, keepdims=True)


def _topk(aff4, cap):
    bsz, n_e, n_r, _ = aff4.shape
    blk4 = pl.BlockSpec((None, n_e, n_r, LANES), lambda b: (b, 0, 0, 0))
    blk3 = pl.BlockSpec((None, n_e, cap), lambda b: (b, 0, 0))
    return pl.pallas_call(
        functools.partial(_topk_kernel, cap=cap),
        grid=(bsz,),
        in_specs=[blk4],
        out_specs=[blk3, blk3, blk4, blk4],
        out_shape=[jax.ShapeDtypeStruct((bsz, n_e, cap), I32),
                   jax.ShapeDtypeStruct((bsz, n_e, cap), F32),
                   jax.ShapeDtypeStruct(aff4.shape, F32),
                   jax.ShapeDtypeStruct(aff4.shape, F32)],
        compiler_params=_params(1),
        name="topk",
    )(aff4)


def _ffn_kernel(idx_ref, h2_hbm, gate_ref, wg_ref, wu_ref, wd_ref, y_ref, xbuf, sem, *,
                bsz, s_len, cap, ch, unroll):
    n_c = cap // ch
    lin = (pl.program_id(0) * bsz + pl.program_id(1)) * n_c + pl.program_id(2)
    n_steps = pl.num_programs(0) * bsz * n_c
    slot = lin % 2

    def issue(step, to_slot):
        c = step % n_c
        pair = step // n_c
        bb = pair % bsz
        ee = pair // bsz
        idx_base = (bb * N_EXPERTS + ee) * cap + c * ch
        row_base = bb * s_len

        def body(j, carry):
            for u in range(unroll):
                r = j * unroll + u
                tok = idx_ref[idx_base + r]
                pltpu.make_async_copy(h2_hbm.at[pl.ds(row_base + tok, 1), :],
                                      xbuf.at[to_slot, pl.ds(r, 1), :],
                                      sem.at[to_slot]).start()
            return carry
        lax.fori_loop(0, ch // unroll, body, 0)

    @pl.when(lin == 0)
    def _():
        issue(lin, slot)

    @pl.when(lin + 1 < n_steps)
    def _():
        issue(lin + 1, 1 - slot)

    pltpu.make_async_copy(h2_hbm.at[pl.ds(0, ch), :], xbuf.at[slot], sem.at[slot]).wait()

    xp = xbuf[slot]
    lo = pltpu.bitcast(xp << 16, F32).astype(BF16)
    hi = pltpu.bitcast(xp & jnp.uint32(0xFFFF0000), F32).astype(BF16)
    x = jnp.concatenate([lo, hi], axis=1)
    a = jnp.dot(x, wg_ref[...], preferred_element_type=F32)
    u = jnp.dot(x, wu_ref[...], preferred_element_type=F32)
    act = (jax.nn.silu(a) * u).astype(BF16)
    y_ref[...] = jnp.dot(act, wd_ref[...], preferred_element_type=F32) * gate_ref[...]


def _ffn(idx_flat, h2p, gates4, wg, wu, wd, bsz, s_len, cap, ch):
    n_e, d, f = wg.shape
    half = d // 2
    n_c = cap // ch
    kern = functools.partial(_ffn_kernel, bsz=bsz, s_len=s_len, cap=cap, ch=ch, unroll=8)
    grid_spec = pltpu.PrefetchScalarGridSpec(
        num_scalar_prefetch=1,
        grid=(n_e, bsz, n_c),
        in_specs=[pl.BlockSpec(memory_space=pl.ANY),
                  pl.BlockSpec((None, None, ch, 1), lambda e, b, c, idx: (b, e, c, 0)),
                  pl.BlockSpec((None, d, f), lambda e, b, c, idx: (e, 0, 0)),
                  pl.BlockSpec((None, d, f), lambda e, b, c, idx: (e, 0, 0)),
                  pl.BlockSpec((None, f, d), lambda e, b, c, idx: (e, 0, 0))],
        out_specs=pl.BlockSpec((None, None, ch, d), lambda e, b, c, idx: (b, e, c, 0)),
        scratch_shapes=[pltpu.VMEM((2, ch, half), U32),
                        pltpu.SemaphoreType.DMA((2,))],
    )
    return pl.pallas_call(
        kern,
        grid_spec=grid_spec,
        out_shape=jax.ShapeDtypeStruct((bsz, n_e, cap, d), F32),
        compiler_params=_params(3),
        name="ffn",
    )(idx_flat, h2p, gates4, wg, wu, wd)


def _combine_kernel(off_ref, y_hbm, x1_ref, cumx_ref, sel_ref, gf_ref, o_ref, ybuf, sem, *, cap, win, n_r,
                    final):
    b = pl.program_id(0)
    r = pl.program_id(1)
    obase = (b * (n_r + 1) + r) * N_EXPERTS
    starts, counts = [], []
    rounds = jnp.int32(0)
    for e in range(N_EXPERTS):
        off = off_ref[obase + e]
        cnt = off_ref[obase + N_EXPERTS + e] - off
        off8 = (off // SUBLANES) * SUBLANES
        need = jnp.where(cnt > 0, (off - off8 + cnt + win - 1) // win, 0)
        rounds = jnp.maximum(rounds, need)
        starts.append(off8)
    o_ref[...] = x1_ref[...]
    jcol = lax.broadcasted_iota(I32, (win, 1), 0).astype(F32)

    def round_body(k, carry):
        copies = []
        firsts = []
        for e in range(N_EXPERTS):
            first = starts[e] + k * win
            begin = pl.multiple_of(jnp.minimum(first, cap - win), SUBLANES)
            src = (b * N_EXPERTS + e) * cap + begin
            cp = pltpu.make_async_copy(y_hbm.at[pl.ds(pl.multiple_of(src, SUBLANES), win), :],
                                       ybuf.at[pl.ds(e * win, win), :], sem.at[0])
            cp.start()
            copies.append(cp)
            firsts.append((first, begin))
        for cp in copies:
            cp.wait()
        blocks = []
        for e in range(N_EXPERTS):
            first, begin = firsts[e]
            rank = cumx_ref[e:e + 1, :]
            hit = jnp.where(rank == jcol + begin.astype(F32),
                            jnp.where(rank >= first.astype(F32), sel_ref[e:e + 1, :], 0.0), 0.0)
            blocks.append(hit)
        onehot = jnp.concatenate(blocks, axis=0).astype(BF16)
        o_ref[...] += lax.dot_general(onehot, ybuf[...].astype(BF16), TN, preferred_element_type=F32)
        return carry
    lax.fori_loop(0, rounds, round_body, 0)

    if final:
        x2 = o_ref[...]
        ms = jnp.mean(x2 * x2, axis=-1, keepdims=True)
        o_ref[...] = x2 * lax.rsqrt(ms + EPS) * gf_ref[...]


def _combine(off_flat, y2d, x1, cumx_t, sel_t, g_final, bsz, s_len, cap, win, final):
    t, d = x1.shape
    n_r = s_len // LANES
    kern = functools.partial(_combine_kernel, cap=cap, win=win, n_r=n_r, final=final)
    meta = pl.BlockSpec((None, None, N_EXPERTS, LANES), lambda b, r, off: (b, r, 0, 0))
    grid_spec = pltpu.PrefetchScalarGridSpec(
        num_scalar_prefetch=1,
        grid=(bsz, n_r),
        in_specs=[pl.BlockSpec(memory_space=pl.ANY),
                  pl.BlockSpec((LANES, d), lambda b, r, off: (b * n_r + r, 0)),
                  meta, meta,
                  pl.BlockSpec((1, d), lambda b, r, off: (0, 0))],
        out_specs=pl.BlockSpec((LANES, d), lambda b, r, off: (b * n_r + r, 0)),
        scratch_shapes=[pltpu.VMEM((N_EXPERTS * win, d), F32),
                        pltpu.SemaphoreType.DMA((1,))],
    )
    return pl.pallas_call(
        kern,
        grid_spec=grid_spec,
        out_shape=jax.ShapeDtypeStruct((t, d), F32),
        compiler_params=_params(2),
        name="combine",
    )(off_flat, y2d, x1, cumx_t, sel_t, g_final)


def _rope_tables(s_len):
    rows = s_len // GRID_W
    r = jnp.repeat(jnp.arange(rows, dtype=F32), GRID_W)
    c = jnp.tile(jnp.arange(GRID_W, dtype=F32), rows)
    half = HEAD_DIM // 2
    inv = 1.0 / (ROPE_THETA ** (jnp.arange(0, half, 2, dtype=F32) / half))
    ang_r = r[:, None] * inv
    ang_c = c[:, None] * inv
    cos = jnp.concatenate([jnp.cos(ang_r), jnp.cos(ang_r), jnp.cos(ang_c), jnp.cos(ang_c)], axis=-1)
    sin = jnp.concatenate([-jnp.sin(ang_r), jnp.sin(ang_r), -jnp.sin(ang_c), jnp.sin(ang_c)], axis=-1)
    return cos, sin


def _t5_bucket_table():
    rel = (jnp.arange(3 * BLOCK) - BLOCK)[None, :] - jnp.arange(BLOCK)[:, None]
    half = NUM_BUCKETS // 2
    ret = jnp.where(rel > 0, half, 0)
    n = jnp.abs(rel)
    max_exact = half // 2
    nf = jnp.maximum(n, 1).astype(F32)
    large = max_exact + (jnp.log(nf / max_exact) / math.log(MAX_DISTANCE / max_exact)
                         * (half - max_exact)).astype(I32)
    large = jnp.minimum(large, half - 1)
    return (ret + jnp.where(n < max_exact, n, large)).astype(I32)


def kernel(x, g_mix, w_in, b_gate, qn_a, kn_a, w_proj_a, sink_b, rel_bias, w_proj_b, w_o, g_ffn, w_router,
           w_gate_e, w_up_e, w_down_e, g_final):
    bsz, s_len, d = x.shape
    depth = g_mix.shape[0]
    t = bsz * s_len
    cap = CAPACITY_FACTOR * s_len // N_EXPERTS
    n_r = s_len // LANES
    cos, sin = _rope_tables(s_len)
    bucket = _t5_bucket_table()
    x2d = x.reshape(t, d)
    for l in range(depth):
        proj = _inproj(x2d, g_mix[l][None, :], w_in[l].astype(BF16), tm=min(1024, t), tn=512)
        oa = _attn_a(proj, cos, sin, qn_a[l][None, :], kn_a[l][None, :], bsz, s_len, d,
                     tq=min(256, s_len), tk=min(512, s_len))
        ob = _attn_b(proj, bucket, rel_bias, sink_b[l], bsz, s_len, d, tq=min(512, s_len))
        x1, h2p, aff = _outproj(oa, ob, proj, x2d, b_gate[l][None, :], w_proj_a[l].astype(BF16),
                                w_proj_b[l].astype(BF16), w_o[l].astype(BF16), g_ffn[l][None, :],
                                w_router[l].T, bsz, s_len, tm=256)
        idx, gates, cumx, sel = _topk(aff.reshape(bsz, N_EXPERTS, n_r, LANES), cap)
        y = _ffn(idx.reshape(-1), h2p, gates[..., None], w_gate_e[l].astype(BF16), w_up_e[l].astype(BF16),
                 w_down_e[l].astype(BF16), bsz, s_len, cap, ch=min(512, cap))
        off = jnp.concatenate([cumx[:, :, :, 0].astype(I32), jnp.full((bsz, N_EXPERTS, 1), cap, I32)], axis=2)
        off_flat = off.transpose(0, 2, 1).reshape(-1)
        x2d = _combine(off_flat, y.reshape(bsz * N_EXPERTS * cap, d), x1, cumx.transpose(0, 2, 1, 3),
                       sel.transpose(0, 2, 1, 3), g_final[None, :], bsz, s_len, cap, win=40,
                       final=(l == depth - 1))
    return x2d.reshape(bsz, s_len, d)
```

```python
import functools
import math

import jax
import jax.numpy as jnp
from jax import lax
from jax.experimental import pallas as pl
from jax.experimental.pallas import tpu as pltpu

F32 = jnp.float32
BF16 = jnp.bfloat16
I32 = jnp.int32
U32 = jnp.uint32

HEAD_DIM = 128
GROUP = 4
BLOCK = 128
GRID_W = 64
ROPE_THETA = 10000.0
NUM_BUCKETS = 32
MAX_DISTANCE = 128
N_EXPERTS = 16
CAPACITY_FACTOR = 2
EPS = 1e-6
NEG = -1e30
LANES = 128
SUBLANES = 8
ONES_ROWS = 16
VMEM_LIMIT = 56 * 1024 * 1024
HIGHEST = lax.Precision.HIGHEST
NT = (((1,), (1,)), ((), ()))
TN = (((0,), (0,)), ((), ()))


def _params(n_axes):
    return pltpu.CompilerParams(dimension_semantics=("arbitrary",) * n_axes,
                                vmem_limit_bytes=VMEM_LIMIT)


def _inproj_kernel(x_ref, g_ref, w_ref, o_ref, h_ref):
    @pl.when(pl.program_id(1) == 0)
    def _():
        x = x_ref[...]
        ms = jnp.mean(x * x, axis=-1, keepdims=True)
        h_ref[...] = (x * lax.rsqrt(ms + EPS) * g_ref[...]).astype(BF16)

    o_ref[...] = jnp.dot(h_ref[...], w_ref[...], preferred_element_type=F32).astype(o_ref.dtype)


def _inproj(x2d, g, w, tm, tn):
    t, d = x2d.shape
    n = w.shape[1]
    return pl.pallas_call(
        _inproj_kernel,
        grid=(t // tm, n // tn),
        in_specs=[pl.BlockSpec((tm, d), lambda i, j: (i, 0)),
                  pl.BlockSpec((1, d), lambda i, j: (0, 0)),
                  pl.BlockSpec((d, tn), lambda i, j: (0, j))],
        out_specs=pl.BlockSpec((tm, tn), lambda i, j: (i, j)),
        out_shape=jax.ShapeDtypeStruct((t, n), BF16),
        scratch_shapes=[pltpu.VMEM((tm, d), BF16)],
        compiler_params=_params(2),
        name="inproj",
    )(x2d, g, w)


def _norm_rope(x, gain, cos, sin_signed):
    ms = jnp.mean(x * x, axis=-1, keepdims=True)
    y = x * lax.rsqrt(ms + EPS) * gain
    lane = lax.broadcasted_iota(I32, y.shape, 1)
    swapped = jnp.where((lane & 32) == 0, pltpu.roll(y, 96, axis=1), pltpu.roll(y, 32, axis=1))
    return y * cos + swapped * sin_signed


def _attn_a_kernel(q_ref, k_ref, v_ref, cq_ref, sq_ref, ck_ref, sk_ref, qn_ref, kn_ref, o_ref,
                   kp_ref, vt_ref, qt_ref, m_ref, acc_ref, st0_ref, st1_ref, *, tq, tk):
    s_len = k_ref.shape[0]
    n_chunks = s_len // tk

    @pl.when(pl.program_id(2) == 0)
    def _():
        def body(c, carry):
            off = pl.multiple_of(c * tk, tk)
            kk = k_ref[pl.ds(off, tk), :].astype(F32)
            kk = _norm_rope(kk, kn_ref[...], ck_ref[pl.ds(off, tk), :], sk_ref[pl.ds(off, tk), :])
            kp_ref[pl.ds(off, tk), :] = kk.astype(BF16)
            vt_ref[c, :HEAD_DIM, :] = v_ref[pl.ds(off, tk), :].astype(F32).T.astype(BF16)
            vt_ref[c, HEAD_DIM:, :] = jnp.ones((ONES_ROWS, tk), BF16)
            return carry
        lax.fori_loop(0, n_chunks, body, 0)

    scale = math.log2(math.e) / math.sqrt(HEAD_DIM)
    for g in range(GROUP):
        qg = q_ref[:, g * HEAD_DIM:(g + 1) * HEAD_DIM].astype(F32)
        qg = _norm_rope(qg, qn_ref[...], cq_ref[...], sq_ref[...]) * scale
        qt_ref[g] = qg.T.astype(BF16)

    m_ref[...] = jnp.full(m_ref.shape, -jnp.inf, F32)
    acc_ref[...] = jnp.zeros(acc_ref.shape, F32)

    def scores(c, dst_ref):
        off = pl.multiple_of(c * tk, tk)
        kc = kp_ref[pl.ds(off, tk), :]
        for g in range(GROUP):
            dst_ref[g] = jnp.dot(kc, qt_ref[g], preferred_element_type=F32)

    def softmax_pv(c, src_ref):
        vt = vt_ref[c]
        for g in range(GROUP):
            st = src_ref[g]
            m_prev = m_ref[g]
            m_new = jnp.maximum(m_prev, jnp.max(st, axis=0, keepdims=True))
            pt = jnp.exp2(st - m_new).astype(BF16)
            acc_ref[g] = jnp.exp2(m_prev - m_new) * acc_ref[g] + jnp.dot(vt, pt, preferred_element_type=F32)
            m_ref[g] = m_new

    scores(0, st0_ref)

    def kv_body(j, carry):
        c = 2 * j
        scores(c + 1, st1_ref)
        softmax_pv(c, st0_ref)
        scores(jnp.minimum(c + 2, n_chunks - 1), st0_ref)
        softmax_pv(c + 1, st1_ref)
        return carry
    lax.fori_loop(0, n_chunks // 2, kv_body, 0)

    for g in range(GROUP):
        acc = acc_ref[g]
        o = (acc[:HEAD_DIM] / acc[HEAD_DIM:HEAD_DIM + 1]).T
        o_ref[:, g * HEAD_DIM:(g + 1) * HEAD_DIM] = o.astype(o_ref.dtype)


def _attn_a(proj, cos, sin, qn, kn, bsz, s_len, d, tq, tk):
    q_w = d // 2
    kv_w = d // 8
    kvh = kv_w // HEAD_DIM
    gw = GROUP * HEAD_DIM
    nq = s_len // tq
    k_blk = q_w // HEAD_DIM
    v_blk = (q_w + kv_w) // HEAD_DIM
    kern = functools.partial(_attn_a_kernel, tq=tq, tk=tk)
    return pl.pallas_call(
        kern,
        grid=(bsz, kvh, nq),
        in_specs=[pl.BlockSpec((tq, gw), lambda b, k, i: (b * nq + i, k)),
                  pl.BlockSpec((s_len, HEAD_DIM), lambda b, k, i: (b, k_blk + k)),
                  pl.BlockSpec((s_len, HEAD_DIM), lambda b, k, i: (b, v_blk + k)),
                  pl.BlockSpec((tq, HEAD_DIM), lambda b, k, i: (i, 0)),
                  pl.BlockSpec((tq, HEAD_DIM), lambda b, k, i: (i, 0)),
                  pl.BlockSpec((s_len, HEAD_DIM), lambda b, k, i: (0, 0)),
                  pl.BlockSpec((s_len, HEAD_DIM), lambda b, k, i: (0, 0)),
                  pl.BlockSpec((1, HEAD_DIM), lambda b, k, i: (0, 0)),
                  pl.BlockSpec((1, HEAD_DIM), lambda b, k, i: (0, 0))],
        out_specs=pl.BlockSpec((tq, gw), lambda b, k, i: (b * nq + i, k)),
        out_shape=jax.ShapeDtypeStruct((bsz * s_len, q_w), BF16),
        scratch_shapes=[pltpu.VMEM((s_len, HEAD_DIM), BF16),
                        pltpu.VMEM((s_len // tk, HEAD_DIM + ONES_ROWS, tk), BF16),
                        pltpu.VMEM((GROUP, HEAD_DIM, tq), BF16),
                        pltpu.VMEM((GROUP, 1, tq), F32),
                        pltpu.VMEM((GROUP, HEAD_DIM + ONES_ROWS, tq), F32),
                        pltpu.VMEM((GROUP, tk, tq), F32),
                        pltpu.VMEM((GROUP, tk, tq), F32)],
        compiler_params=_params(3),
        name="attn_a",
    )(proj, proj, proj, cos, sin, cos, sin, qn, kn)


def _attn_b_kernel(relb_ref, sink_ref, bucket_ref, q_ref, kp_ref, kc_ref, kn_ref, vp_ref, vc_ref, vn_ref,
                   o_ref, bias_ref, *, tq, s_len, kvh):
    b = pl.program_id(0)
    k = pl.program_id(1)
    i = pl.program_id(2)
    band = 3 * BLOCK

    @pl.when((b == 0) & (k == 0) & (i == 0))
    def _():
        bucket = bucket_ref[...]
        col = lax.broadcasted_iota(I32, (BLOCK, band), 1)
        row = lax.broadcasted_iota(I32, (BLOCK, band), 0)
        in_window = jnp.abs(col - BLOCK - row) <= BLOCK
        for kk in range(kvh):
            for g in range(GROUP):
                h = kk * GROUP + g
                tab = jnp.zeros((BLOCK, band), F32)
                for bkt in range(NUM_BUCKETS):
                    tab = jnp.where(bucket == bkt, relb_ref[bkt, h], tab)
                bias_ref[kk, g * BLOCK:(g + 1) * BLOCK, :] = jnp.where(in_window, tab, NEG)

    kcat = jnp.concatenate([kp_ref[...], kc_ref[...], kn_ref[...]], axis=0)
    vcat = jnp.concatenate([vp_ref[...], vc_ref[...], vn_ref[...]], axis=0)
    bias = bias_ref[k]
    sink = jnp.concatenate([jnp.full((BLOCK, 1), sink_ref[k * GROUP + g], F32) for g in range(GROUP)], axis=0)
    scale = 1.0 / math.sqrt(HEAD_DIM)
    nsub = tq // BLOCK
    for jb in range(nsub):
        rows = slice(jb * BLOCK, (jb + 1) * BLOCK)
        q4 = jnp.concatenate([q_ref[rows, g * HEAD_DIM:(g + 1) * HEAD_DIM] for g in range(GROUP)], axis=0)
        kb = kcat[jb * BLOCK:jb * BLOCK + band]
        vb = vcat[jb * BLOCK:jb * BLOCK + band]
        s = lax.dot_general(q4, kb, NT, preferred_element_type=F32) * scale + bias
        if jb == 0 or jb == nsub - 1:
            kpos = i * tq + (jb - 1) * BLOCK + lax.broadcasted_iota(I32, (1, band), 1)
            s = jnp.where((kpos >= 0) & (kpos < s_len), s, NEG)
        m = jnp.maximum(jnp.max(s, axis=-1, keepdims=True), sink)
        e = jnp.exp(s - m)
        den = jnp.sum(e, axis=-1, keepdims=True) + jnp.exp(sink - m)
        o = jnp.dot(e.astype(BF16), vb, preferred_element_type=F32) / den
        for g in range(GROUP):
            o_ref[rows, g * HEAD_DIM:(g + 1) * HEAD_DIM] = o[g * BLOCK:(g + 1) * BLOCK, :].astype(o_ref.dtype)


def _attn_b(proj, bucket, rel_bias, sink, bsz, s_len, d, tq):
    q_w = d // 2
    kv_w = d // 8
    kvh = kv_w // HEAD_DIM
    gw = GROUP * HEAD_DIM
    nq = s_len // tq
    sub = tq // BLOCK
    nblk = s_len // BLOCK
    q_blk = (q_w + 2 * kv_w) // gw
    k_blk = (2 * q_w + 2 * kv_w) // HEAD_DIM
    v_blk = (2 * q_w + 3 * kv_w) // HEAD_DIM

    def prev_map(col):
        return lambda b, k, i: (b * nblk + jnp.maximum(i * sub - 1, 0), col + k)

    def cur_map(col):
        return lambda b, k, i: (b * nq + i, col + k)

    def next_map(col):
        return lambda b, k, i: (b * nblk + jnp.minimum((i + 1) * sub, nblk - 1), col + k)

    small = (BLOCK, HEAD_DIM)
    kern = functools.partial(_attn_b_kernel, tq=tq, s_len=s_len, kvh=kvh)
    return pl.pallas_call(
        kern,
        grid=(bsz, kvh, nq),
        in_specs=[pl.BlockSpec(memory_space=pltpu.SMEM),
                  pl.BlockSpec(memory_space=pltpu.SMEM),
                  pl.BlockSpec((BLOCK, 3 * BLOCK), lambda b, k, i: (0, 0)),
                  pl.BlockSpec((tq, gw), lambda b, k, i: (b * nq + i, q_blk + k)),
                  pl.BlockSpec(small, prev_map(k_blk)),
                  pl.BlockSpec((tq, HEAD_DIM), cur_map(k_blk)),
                  pl.BlockSpec(small, next_map(k_blk)),
                  pl.BlockSpec(small, prev_map(v_blk)),
                  pl.BlockSpec((tq, HEAD_DIM), cur_map(v_blk)),
                  pl.BlockSpec(small, next_map(v_blk))],
        out_specs=pl.BlockSpec((tq, gw), lambda b, k, i: (b * nq + i, k)),
        out_shape=jax.ShapeDtypeStruct((bsz * s_len, q_w), BF16),
        scratch_shapes=[pltpu.VMEM((kvh, GROUP * BLOCK, 3 * BLOCK), F32)],
        compiler_params=_params(3),
        name="attn_b",
    )(rel_bias, sink, bucket, proj, proj, proj, proj, proj, proj, proj)


def _outproj_kernel(oa_ref, ob_ref, ga0_ref, ga1_ref, gb0_ref, gb1_ref, x_ref, bg_ref, wpa_ref, wpb_ref,
                    wo_ref, gf_ref, wr_ref, x1_ref, h2p_ref, aff_ref):
    d = x_ref.shape[1]
    half = d // 2
    pa = jnp.dot(oa_ref[...], wpa_ref[...], preferred_element_type=F32)
    pb = jnp.dot(ob_ref[...], wpb_ref[...], preferred_element_type=F32)
    ga = jnp.concatenate([ga0_ref[...], ga1_ref[...]], axis=1).astype(F32) + bg_ref[:, :d]
    gb = jnp.concatenate([gb0_ref[...], gb1_ref[...]], axis=1).astype(F32) + bg_ref[:, d:]
    merged = jax.nn.sigmoid(ga) * pa + jax.nn.sigmoid(gb) * pb
    x1 = x_ref[...] + jnp.dot(merged.astype(BF16), wo_ref[...], preferred_element_type=F32)
    x1_ref[...] = x1
    ms = jnp.mean(x1 * x1, axis=-1, keepdims=True)
    h = x1 * lax.rsqrt(ms + EPS) * gf_ref[...]
    bits = pltpu.bitcast(h.astype(BF16).astype(F32), U32)
    h2p_ref[...] = (bits[:, :half] >> 16) | (bits[:, half:] & jnp.uint32(0xFFFF0000))
    logits = lax.dot_general(wr_ref[...], h, NT, precision=HIGHEST, preferred_element_type=F32)
    ex = jnp.exp(logits - jnp.max(logits, axis=0, keepdims=True))
    aff_ref[...] = ex / jnp.sum(ex, axis=0, keepdims=True)


def _outproj(oa, ob, proj, x2d, b_gate, wpa, wpb, wo, g_ffn, wr_t, bsz, s_len, tm):
    t, d = x2d.shape
    half = d // 2
    nt = s_len // tm
    row = lambda c: (lambda i: (i, c))
    const = lambda i: (0, 0)
    return pl.pallas_call(
        _outproj_kernel,
        grid=(t // tm,),
        in_specs=[pl.BlockSpec((tm, half), row(0)),
                  pl.BlockSpec((tm, half), row(0)),
                  pl.BlockSpec((tm, half), row(3)),
                  pl.BlockSpec((tm, half), row(4)),
                  pl.BlockSpec((tm, half), row(5)),
                  pl.BlockSpec((tm, half), row(6)),
                  pl.BlockSpec((tm, d), row(0)),
                  pl.BlockSpec((1, 2 * d), const),
                  pl.BlockSpec((half, d), const),
                  pl.BlockSpec((half, d), const),
                  pl.BlockSpec((d, d), const),
                  pl.BlockSpec((1, d), const),
                  pl.BlockSpec((N_EXPERTS, d), const)],
        out_specs=[pl.BlockSpec((tm, d), row(0)),
                   pl.BlockSpec((tm, half), row(0)),
                   pl.BlockSpec((None, N_EXPERTS, tm), lambda i: (i // nt, 0, i % nt))],
        out_shape=[jax.ShapeDtypeStruct((t, d), F32),
                   jax.ShapeDtypeStruct((t, half), U32),
                   jax.ShapeDtypeStruct((bsz, N_EXPERTS, s_len), F32)],
        compiler_params=_params(1),
        name="outproj",
    )(oa, ob, proj, proj, proj, proj, x2d, b_gate, wpa, wpb, wo, g_ffn, wr_t)


def _topk_kernel(a_ref, idx_ref, gate_ref, cumx_ref, sel_ref, *, cap):
    a = a_ref[...]
    n_e, n_r, _ = a.shape
    bits = pltpu.bitcast(a, I32)

    def total(x):
        return jnp.sum(jnp.sum(x, axis=1, keepdims=True), axis=2, keepdims=True)

    def search(_, carry):
        lo, hi = carry
        mid = lo + ((hi - lo) >> 1)
        enough = total(jnp.where(bits >= mid, 1.0, 0.0)) >= cap
        return jnp.where(enough, mid, lo), jnp.where(enough, hi, mid)

    lo0 = jnp.zeros((n_e, 1, 1), I32)
    hi0 = jnp.full((n_e, 1, 1), 0x7F800000, I32)
    thr, _ = lax.fori_loop(0, 31, search, (lo0, hi0))

    ri = lax.broadcasted_iota(I32, (LANES, LANES), 0)
    ci = lax.broadcasted_iota(I32, (LANES, LANES), 1)
    upper = jnp.where(ri <= ci, 1.0, 0.0).astype(BF16)
    ones = jnp.ones((LANES, LANES), BF16)
    rr = lax.broadcasted_iota(I32, (n_r, n_r), 0)
    rc = lax.broadcasted_iota(I32, (n_r, n_r), 1)
    strict_lower = jnp.where(rc < rr, 1.0, 0.0).astype(BF16)

    def prefix(x):
        x2 = x.reshape(n_e * n_r, LANES).astype(BF16)
        in_row = jnp.dot(x2, upper, preferred_element_type=F32).reshape(n_e, n_r, LANES)
        row_tot = jnp.dot(x2, ones, preferred_element_type=F32).reshape(n_e, n_r, LANES)
        before = jnp.stack([jnp.dot(strict_lower, row_tot[e].astype(BF16), preferred_element_type=F32)
                            for e in range(n_e)], axis=0)
        return in_row + before

    above = jnp.where(bits > thr, 1.0, 0.0)
    tied = jnp.where(bits == thr, 1.0, 0.0)
    need = cap - total(above)
    sel = above + tied * jnp.where(prefix(tied) <= need, 1.0, 0.0)
    cum = prefix(sel)
    sel_ref[...] = sel
    cumx_ref[...] = cum - sel

    slot = lax.broadcasted_iota(I32, (1, cap), 1).astype(F32)
    row_id = lax.broadcasted_iota(I32, (n_r, 1), 0).astype(F32)
    lane_id = lax.broadcasted_iota(I32, (LANES, 1), 0).astype(F32)
    for e in range(n_e):
        c = cum[e]
        row_end = c[:, LANES - 1:LANES]
        srow = jnp.sum(jnp.where(row_end <= slot, 1.0, 0.0), axis=0, keepdims=True)
        pick = jnp.where(row_id == srow, 1.0, 0.0)
        c_row = lax.dot_general(c, pick, TN, precision=HIGHEST, preferred_element_type=F32)
        slane = jnp.sum(jnp.where(c_row <= slot, 1.0, 0.0), axis=0, keepdims=True)
        a_row = lax.dot_general(a[e], pick, TN, precision=HIGHEST, preferred_element_type=F32)
        idx_ref[e:e + 1, :] = (srow * LANES + slane).astype(I32)
        gate_ref[e:e + 1, :] = jnp.sum(jnp.where(lane_id == slane, a_row, 0.0), axis=0, keepdims=True)


def _topk(aff4, cap):
    bsz, n_e, n_r, _ = aff4.shape
    blk4 = pl.BlockSpec((None, n_e, n_r, LANES), lambda b: (b, 0, 0, 0))
    blk3 = pl.BlockSpec((None, n_e, cap), lambda b: (b, 0, 0))
    return pl.pallas_call(
        functools.partial(_topk_kernel, cap=cap),
        grid=(bsz,),
        in_specs=[blk4],
        out_specs=[blk3, blk3, blk4, blk4],
        out_shape=[jax.ShapeDtypeStruct((bsz, n_e, cap), I32),
                   jax.ShapeDtypeStruct((bsz, n_e, cap), F32),
                   jax.ShapeDtypeStruct(aff4.shape, F32),
                   jax.ShapeDtypeStruct(aff4.shape, F32)],
        compiler_params=_params(1),
        name="topk",
    )(aff4)


def _ffn_kernel(idx_ref, h2_hbm, gate_ref, wg_ref, wu_ref, wd_ref, y_ref, xbuf, sem, *,
                bsz, s_len, cap, ch, unroll):
    n_c = cap // ch
    lin = (pl.program_id(0) * bsz + pl.program_id(1)) * n_c + pl.program_id(2)
    n_steps = pl.num_programs(0) * bsz * n_c
    slot = lin % 2

    def issue(step, to_slot):
        c = step % n_c
        pair = step // n_c
        bb = pair % bsz
        ee = pair // bsz
        idx_base = (bb * N_EXPERTS + ee) * cap + c * ch
        row_base = bb * s_len

        def body(j, carry):
            for u in range(unroll):
                r = j * unroll + u
                tok = idx_ref[idx_base + r]
                pltpu.make_async_copy(h2_hbm.at[pl.ds(row_base + tok, 1), :],
                                      xbuf.at[to_slot, pl.ds(r, 1), :],
                                      sem.at[to_slot]).start()
            return carry
        lax.fori_loop(0, ch // unroll, body, 0)

    @pl.when(lin == 0)
    def _():
        issue(lin, slot)

    @pl.when(lin + 1 < n_steps)
    def _():
        issue(lin + 1, 1 - slot)

    pltpu.make_async_copy(h2_hbm.at[pl.ds(0, ch), :], xbuf.at[slot], sem.at[slot]).wait()

    xp = xbuf[slot]
    lo = pltpu.bitcast(xp << 16, F32).astype(BF16)
    hi = pltpu.bitcast(xp & jnp.uint32(0xFFFF0000), F32).astype(BF16)
    x = jnp.concatenate([lo, hi], axis=1)
    a = jnp.dot(x, wg_ref[...], preferred_element_type=F32)
    u = jnp.dot(x, wu_ref[...], preferred_element_type=F32)
    act = (jax.nn.silu(a) * u).astype(BF16)
    y_ref[...] = jnp.dot(act, wd_ref[...], preferred_element_type=F32) * gate_ref[...]


def _ffn(idx_flat, h2p, gates4, wg, wu, wd, bsz, s_len, cap, ch):
    n_e, d, f = wg.shape
    half = d // 2
    n_c = cap // ch
    kern = functools.partial(_ffn_kernel, bsz=bsz, s_len=s_len, cap=cap, ch=ch, unroll=8)
    grid_spec = pltpu.PrefetchScalarGridSpec(
        num_scalar_prefetch=1,
        grid=(n_e, bsz, n_c),
        in_specs=[pl.BlockSpec(memory_space=pl.ANY),
                  pl.BlockSpec((None, None, ch, 1), lambda e, b, c, idx: (b, e, c, 0)),
                  pl.BlockSpec((None, d, f), lambda e, b, c, idx: (e, 0, 0)),
                  pl.BlockSpec((None, d, f), lambda e, b, c, idx: (e, 0, 0)),
                  pl.BlockSpec((None, f, d), lambda e, b, c, idx: (e, 0, 0))],
        out_specs=pl.BlockSpec((None, None, ch, d), lambda e, b, c, idx: (b, e, c, 0)),
        scratch_shapes=[pltpu.VMEM((2, ch, half), U32),
                        pltpu.SemaphoreType.DMA((2,))],
    )
    return pl.pallas_call(
        kern,
        grid_spec=grid_spec,
        out_shape=jax.ShapeDtypeStruct((bsz, n_e, cap, d), F32),
        compiler_params=_params(3),
        name="ffn",
    )(idx_flat, h2p, gates4, wg, wu, wd)


def _combine_kernel(off_ref, y_hbm, x1_ref, cumx_ref, sel_ref, gf_ref, o_ref, ybuf, sem, *, cap, win, n_r,
                    final):
    b = pl.program_id(0)
    r = pl.program_id(1)
    obase = (b * (n_r + 1) + r) * N_EXPERTS
    starts, counts = [], []
    rounds = jnp.int32(0)
    for e in range(N_EXPERTS):
        off = off_ref[obase + e]
        cnt = off_ref[obase + N_EXPERTS + e] - off
        off8 = (off // SUBLANES) * SUBLANES
        need = jnp.where(cnt > 0, (off - off8 + cnt + win - 1) // win, 0)
        rounds = jnp.maximum(rounds, need)
        starts.append(off8)
    o_ref[...] = x1_ref[...]
    jcol = lax.broadcasted_iota(I32, (win, 1), 0).astype(F32)

    def round_body(k, carry):
        copies = []
        firsts = []
        for e in range(N_EXPERTS):
            first = starts[e] + k * win
            begin = pl.multiple_of(jnp.minimum(first, cap - win), SUBLANES)
            src = (b * N_EXPERTS + e) * cap + begin
            cp = pltpu.make_async_copy(y_hbm.at[pl.ds(pl.multiple_of(src, SUBLANES), win), :],
                                       ybuf.at[pl.ds(e * win, win), :], sem.at[0])
            cp.start()
            copies.append(cp)
            firsts.append((first, begin))
        for cp in copies:
            cp.wait()
        blocks = []
        for e in range(N_EXPERTS):
            first, begin = firsts[e]
            rank = cumx_ref[e:e + 1, :]
            hit = jnp.where(rank == jcol + begin.astype(F32),
                            jnp.where(rank >= first.astype(F32), sel_ref[e:e + 1, :], 0.0), 0.0)
            blocks.append(hit)
        onehot = jnp.concatenate(blocks, axis=0).astype(BF16)
        o_ref[...] += lax.dot_general(onehot, ybuf[...].astype(BF16), TN, preferred_element_type=F32)
        return carry
    lax.fori_loop(0, rounds, round_body, 0)

    if final:
        x2 = o_ref[...]
        ms = jnp.mean(x2 * x2, axis=-1, keepdims=True)
        o_ref[...] = x2 * lax.rsqrt(ms + EPS) * gf_ref[...]


def _combine(off_flat, y2d, x1, cumx_t, sel_t, g_final, bsz, s_len, cap, win, final):
    t, d = x1.shape
    n_r = s_len // LANES
    kern = functools.partial(_combine_kernel, cap=cap, win=win, n_r=n_r, final=final)
    meta = pl.BlockSpec((None, None, N_EXPERTS, LANES), lambda b, r, off: (b, r, 0, 0))
    grid_spec = pltpu.PrefetchScalarGridSpec(
        num_scalar_prefetch=1,
        grid=(bsz, n_r),
        in_specs=[pl.BlockSpec(memory_space=pl.ANY),
                  pl.BlockSpec((LANES, d), lambda b, r, off: (b * n_r + r, 0)),
                  meta, meta,
                  pl.BlockSpec((1, d), lambda b, r, off: (0, 0))],
        out_specs=pl.BlockSpec((LANES, d), lambda b, r, off: (b * n_r + r, 0)),
        scratch_shapes=[pltpu.VMEM((N_EXPERTS * win, d), F32),
                        pltpu.SemaphoreType.DMA((1,))],
    )
    return pl.pallas_call(
        kern,
        grid_spec=grid_spec,
        out_shape=jax.ShapeDtypeStruct((t, d), F32),
        compiler_params=_params(2),
        name="combine",
    )(off_flat, y2d, x1, cumx_t, sel_t, g_final)


def _rope_tables(s_len):
    rows = s_len // GRID_W
    r = jnp.repeat(jnp.arange(rows, dtype=F32), GRID_W)
    c = jnp.tile(jnp.arange(GRID_W, dtype=F32), rows)
    half = HEAD_DIM // 2
    inv = 1.0 / (ROPE_THETA ** (jnp.arange(0, half, 2, dtype=F32) / half))
    ang_r = r[:, None] * inv
    ang_c = c[:, None] * inv
    cos = jnp.concatenate([jnp.cos(ang_r), jnp.cos(ang_r), jnp.cos(ang_c), jnp.cos(ang_c)], axis=-1)
    sin = jnp.concatenate([-jnp.sin(ang_r), jnp.sin(ang_r), -jnp.sin(ang_c), jnp.sin(ang_c)], axis=-1)
    return cos, sin


def _t5_bucket_table():
    rel = (jnp.arange(3 * BLOCK) - BLOCK)[None, :] - jnp.arange(BLOCK)[:, None]
    half = NUM_BUCKETS // 2
    ret = jnp.where(rel > 0, half, 0)
    n = jnp.abs(rel)
    max_exact = half // 2
    nf = jnp.maximum(n, 1).astype(F32)
    large = max_exact + (jnp.log(nf / max_exact) / math.log(MAX_DISTANCE / max_exact)
                         * (half - max_exact)).astype(I32)
    large = jnp.minimum(large, half - 1)
    return (ret + jnp.where(n < max_exact, n, large)).astype(I32)


def kernel(x, g_mix, w_in, b_gate, qn_a, kn_a, w_proj_a, sink_b, rel_bias, w_proj_b, w_o, g_ffn, w_router,
           w_gate_e, w_up_e, w_down_e, g_final):
    bsz, s_len, d = x.shape
    depth = g_mix.shape[0]
    t = bsz * s_len
    cap = CAPACITY_FACTOR * s_len // N_EXPERTS
    n_r = s_len // LANES
    cos, sin = _rope_tables(s_len)
    bucket = _t5_bucket_table()
    x2d = x.reshape(t, d)
    for l in range(depth):
        proj = _inproj(x2d, g_mix[l][None, :], w_in[l].astype(BF16), tm=min(1024, t), tn=1024)
        oa = _attn_a(proj, cos, sin, qn_a[l][None, :], kn_a[l][None, :], bsz, s_len, d,
                     tq=min(256, s_len), tk=min(512, s_len))
        ob = _attn_b(proj, bucket, rel_bias, sink_b[l], bsz, s_len, d, tq=min(512, s_len))
        x1, h2p, aff = _outproj(oa, ob, proj, x2d, b_gate[l][None, :], w_proj_a[l].astype(BF16),
                                w_proj_b[l].astype(BF16), w_o[l].astype(BF16), g_ffn[l][None, :],
                                w_router[l].T, bsz, s_len, tm=256)
        idx, gates, cumx, sel = _topk(aff.reshape(bsz, N_EXPERTS, n_r, LANES), cap)
        y = _ffn(idx.reshape(-1), h2p, gates[..., None], w_gate_e[l].astype(BF16), w_up_e[l].astype(BF16),
                 w_down_e[l].astype(BF16), bsz, s_len, cap, ch=min(512, cap))
        off = jnp.concatenate([cumx[:, :, :, 0].astype(I32), jnp.full((bsz, N_EXPERTS, 1), cap, I32)], axis=2)
        off_flat = off.transpose(0, 2, 1).reshape(-1)
        x2d = _combine(off_flat, y.reshape(bsz * N_EXPERTS * cap, d), x1, cumx.transpose(0, 2, 1, 3),
                       sel.transpose(0, 2, 1, 3), g_final[None, :], bsz, s_len, cap, win=40,
                       final=(l == depth - 1))
    return x2d.reshape(bsz, s_len, d)
```

```python
import functools
import math

import jax
import jax.numpy as jnp
from jax import lax
from jax.experimental import pallas as pl
from jax.experimental.pallas import tpu as pltpu

F32 = jnp.float32
BF16 = jnp.bfloat16
I32 = jnp.int32
U32 = jnp.uint32

HEAD_DIM = 128
GROUP = 4
BLOCK = 128
GRID_W = 64
ROPE_THETA = 10000.0
NUM_BUCKETS = 32
MAX_DISTANCE = 128
N_EXPERTS = 16
CAPACITY_FACTOR = 2
EPS = 1e-6
NEG = -1e30
LANES = 128
SUBLANES = 8
BF16_ROWS = 16
ONES_ROWS = BF16_ROWS
VMEM_LIMIT = 56 * 1024 * 1024
HIGHEST = lax.Precision.HIGHEST
NT = (((1,), (1,)), ((), ()))
TN = (((0,), (0,)), ((), ()))


def _params(n_axes):
    return pltpu.CompilerParams(dimension_semantics=("arbitrary",) * n_axes,
                                vmem_limit_bytes=VMEM_LIMIT)


def _inproj_kernel(x_ref, g_ref, w_ref, o_ref, h_ref):
    @pl.when(pl.program_id(1) == 0)
    def _():
        x = x_ref[...]
        ms = jnp.mean(x * x, axis=-1, keepdims=True)
        h_ref[...] = (x * lax.rsqrt(ms + EPS) * g_ref[...]).astype(BF16)

    o_ref[...] = jnp.dot(h_ref[...], w_ref[...], preferred_element_type=F32).astype(o_ref.dtype)


def _inproj(x2d, g, w, tm, tn):
    t, d = x2d.shape
    n = w.shape[1]
    return pl.pallas_call(
        _inproj_kernel,
        grid=(t // tm, n // tn),
        in_specs=[pl.BlockSpec((tm, d), lambda i, j: (i, 0)),
                  pl.BlockSpec((1, d), lambda i, j: (0, 0)),
                  pl.BlockSpec((d, tn), lambda i, j: (0, j))],
        out_specs=pl.BlockSpec((tm, tn), lambda i, j: (i, j)),
        out_shape=jax.ShapeDtypeStruct((t, n), BF16),
        scratch_shapes=[pltpu.VMEM((tm, d), BF16)],
        compiler_params=_params(2),
        name="inproj",
    )(x2d, g, w)


def _norm_rope(x, gain, cos, sin_signed):
    ms = jnp.mean(x * x, axis=-1, keepdims=True)
    y = x * lax.rsqrt(ms + EPS) * gain
    lane = lax.broadcasted_iota(I32, y.shape, 1)
    swapped = jnp.where((lane & 32) == 0, pltpu.roll(y, 96, axis=1), pltpu.roll(y, 32, axis=1))
    return y * cos + swapped * sin_signed


def _attn_a_kernel(q_ref, k_ref, v_ref, cq_ref, sq_ref, ck_ref, sk_ref, qn_ref, kn_ref, o_ref,
                   kp_ref, vt_ref, qt_ref, m_ref, acc_ref, st0_ref, st1_ref, *, tq, tk):
    s_len = k_ref.shape[0]
    n_chunks = s_len // tk

    @pl.when(pl.program_id(2) == 0)
    def _():
        def body(c, carry):
            off = pl.multiple_of(c * tk, tk)
            kk = k_ref[pl.ds(off, tk), :].astype(F32)
            kk = _norm_rope(kk, kn_ref[...], ck_ref[pl.ds(off, tk), :], sk_ref[pl.ds(off, tk), :])
            kp_ref[pl.ds(off, tk), :] = kk.astype(BF16)
            vt_ref[c, :HEAD_DIM, :] = v_ref[pl.ds(off, tk), :].astype(F32).T.astype(BF16)
            vt_ref[c, HEAD_DIM:, :] = jnp.ones((ONES_ROWS, tk), BF16)
            return carry
        lax.fori_loop(0, n_chunks, body, 0)

    scale = math.log2(math.e) / math.sqrt(HEAD_DIM)
    for g in range(GROUP):
        qg = q_ref[:, g * HEAD_DIM:(g + 1) * HEAD_DIM].astype(F32)
        qg = _norm_rope(qg, qn_ref[...], cq_ref[...], sq_ref[...]) * scale
        qt_ref[g] = qg.T.astype(BF16)

    m_ref[...] = jnp.full(m_ref.shape, -jnp.inf, F32)
    acc_ref[...] = jnp.zeros(acc_ref.shape, F32)

    def scores(c, dst_ref):
        off = pl.multiple_of(c * tk, tk)
        kc = kp_ref[pl.ds(off, tk), :]
        for g in range(GROUP):
            dst_ref[g] = jnp.dot(kc, qt_ref[g], preferred_element_type=F32)

    def softmax_pv(c, src_ref):
        vt = vt_ref[c]
        for g in range(GROUP):
            st = src_ref[g]
            m_prev = m_ref[g]
            m_new = jnp.maximum(m_prev, jnp.max(st, axis=0, keepdims=True))
            pt = jnp.exp2(st - m_new).astype(BF16)
            acc_ref[g] = jnp.exp2(m_prev - m_new) * acc_ref[g] + jnp.dot(vt, pt, preferred_element_type=F32)
            m_ref[g] = m_new

    scores(0, st0_ref)

    def kv_body(j, carry):
        c = 2 * j
        scores(c + 1, st1_ref)
        softmax_pv(c, st0_ref)
        scores(jnp.minimum(c + 2, n_chunks - 1), st0_ref)
        softmax_pv(c + 1, st1_ref)
        return carry
    lax.fori_loop(0, n_chunks // 2, kv_body, 0)

    for g in range(GROUP):
        acc = acc_ref[g]
        o = (acc[:HEAD_DIM] / acc[HEAD_DIM:HEAD_DIM + 1]).T
        o_ref[:, g * HEAD_DIM:(g + 1) * HEAD_DIM] = o.astype(o_ref.dtype)


def _attn_a(proj, cos, sin, qn, kn, bsz, s_len, d, tq, tk):
    q_w = d // 2
    kv_w = d // 8
    kvh = kv_w // HEAD_DIM
    gw = GROUP * HEAD_DIM
    nq = s_len // tq
    k_blk = q_w // HEAD_DIM
    v_blk = (q_w + kv_w) // HEAD_DIM
    kern = functools.partial(_attn_a_kernel, tq=tq, tk=tk)
    return pl.pallas_call(
        kern,
        grid=(bsz, kvh, nq),
        in_specs=[pl.BlockSpec((tq, gw), lambda b, k, i: (b * nq + i, k)),
                  pl.BlockSpec((s_len, HEAD_DIM), lambda b, k, i: (b, k_blk + k)),
                  pl.BlockSpec((s_len, HEAD_DIM), lambda b, k, i: (b, v_blk + k)),
                  pl.BlockSpec((tq, HEAD_DIM), lambda b, k, i: (i, 0)),
                  pl.BlockSpec((tq, HEAD_DIM), lambda b, k, i: (i, 0)),
                  pl.BlockSpec((s_len, HEAD_DIM), lambda b, k, i: (0, 0)),
                  pl.BlockSpec((s_len, HEAD_DIM), lambda b, k, i: (0, 0)),
                  pl.BlockSpec((1, HEAD_DIM), lambda b, k, i: (0, 0)),
                  pl.BlockSpec((1, HEAD_DIM), lambda b, k, i: (0, 0))],
        out_specs=pl.BlockSpec((tq, gw), lambda b, k, i: (b * nq + i, k)),
        out_shape=jax.ShapeDtypeStruct((bsz * s_len, q_w), BF16),
        scratch_shapes=[pltpu.VMEM((s_len, HEAD_DIM), BF16),
                        pltpu.VMEM((s_len // tk, HEAD_DIM + ONES_ROWS, tk), BF16),
                        pltpu.VMEM((GROUP, HEAD_DIM, tq), BF16),
                        pltpu.VMEM((GROUP, 1, tq), F32),
                        pltpu.VMEM((GROUP, HEAD_DIM + ONES_ROWS, tq), F32),
                        pltpu.VMEM((GROUP, tk, tq), F32),
                        pltpu.VMEM((GROUP, tk, tq), F32)],
        compiler_params=_params(3),
        name="attn_a",
    )(proj, proj, proj, cos, sin, cos, sin, qn, kn)


def _attn_b_kernel(relb_ref, sink_ref, bucket_ref, q_ref, kp_ref, kc_ref, kn_ref, vp_ref, vc_ref, vn_ref,
                   o_ref, bias_ref, *, tq, s_len, kvh):
    b = pl.program_id(0)
    k = pl.program_id(1)
    i = pl.program_id(2)
    band = 3 * BLOCK

    @pl.when((b == 0) & (k == 0) & (i == 0))
    def _():
        bucket = bucket_ref[...]
        col = lax.broadcasted_iota(I32, (BLOCK, band), 1)
        row = lax.broadcasted_iota(I32, (BLOCK, band), 0)
        in_window = jnp.abs(col - BLOCK - row) <= BLOCK
        for kk in range(kvh):
            for g in range(GROUP):
                h = kk * GROUP + g
                tab = jnp.zeros((BLOCK, band), F32)
                for bkt in range(NUM_BUCKETS):
                    tab = jnp.where(bucket == bkt, relb_ref[bkt, h], tab)
                bias_ref[kk, g * BLOCK:(g + 1) * BLOCK, :] = jnp.where(in_window, tab, NEG)

    kcat = jnp.concatenate([kp_ref[...], kc_ref[...], kn_ref[...]], axis=0)
    vcat = jnp.concatenate([vp_ref[...], vc_ref[...], vn_ref[...]], axis=0)
    bias = bias_ref[k]
    sink = jnp.concatenate([jnp.full((BLOCK, 1), sink_ref[k * GROUP + g], F32) for g in range(GROUP)], axis=0)
    scale = 1.0 / math.sqrt(HEAD_DIM)
    nsub = tq // BLOCK
    for jb in range(nsub):
        rows = slice(jb * BLOCK, (jb + 1) * BLOCK)
        q4 = jnp.concatenate([q_ref[rows, g * HEAD_DIM:(g + 1) * HEAD_DIM] for g in range(GROUP)], axis=0)
        kb = kcat[jb * BLOCK:jb * BLOCK + band]
        vb = vcat[jb * BLOCK:jb * BLOCK + band]
        s = lax.dot_general(q4, kb, NT, preferred_element_type=F32) * scale + bias
        if jb == 0 or jb == nsub - 1:
            kpos = i * tq + (jb - 1) * BLOCK + lax.broadcasted_iota(I32, (1, band), 1)
            s = jnp.where((kpos >= 0) & (kpos < s_len), s, NEG)
        m = jnp.maximum(jnp.max(s, axis=-1, keepdims=True), sink)
        e = jnp.exp(s - m)
        den = jnp.sum(e, axis=-1, keepdims=True) + jnp.exp(sink - m)
        o = jnp.dot(e.astype(BF16), vb, preferred_element_type=F32) / den
        for g in range(GROUP):
            o_ref[rows, g * HEAD_DIM:(g + 1) * HEAD_DIM] = o[g * BLOCK:(g + 1) * BLOCK, :].astype(o_ref.dtype)


def _attn_b(proj, bucket, rel_bias, sink, bsz, s_len, d, tq):
    q_w = d // 2
    kv_w = d // 8
    kvh = kv_w // HEAD_DIM
    gw = GROUP * HEAD_DIM
    nq = s_len // tq
    sub = tq // BLOCK
    nblk = s_len // BLOCK
    q_blk = (q_w + 2 * kv_w) // gw
    k_blk = (2 * q_w + 2 * kv_w) // HEAD_DIM
    v_blk = (2 * q_w + 3 * kv_w) // HEAD_DIM

    def prev_map(col):
        return lambda b, k, i: (b * nblk + jnp.maximum(i * sub - 1, 0), col + k)

    def cur_map(col):
        return lambda b, k, i: (b * nq + i, col + k)

    def next_map(col):
        return lambda b, k, i: (b * nblk + jnp.minimum((i + 1) * sub, nblk - 1), col + k)

    small = (BLOCK, HEAD_DIM)
    kern = functools.partial(_attn_b_kernel, tq=tq, s_len=s_len, kvh=kvh)
    return pl.pallas_call(
        kern,
        grid=(bsz, kvh, nq),
        in_specs=[pl.BlockSpec(memory_space=pltpu.SMEM),
                  pl.BlockSpec(memory_space=pltpu.SMEM),
                  pl.BlockSpec((BLOCK, 3 * BLOCK), lambda b, k, i: (0, 0)),
                  pl.BlockSpec((tq, gw), lambda b, k, i: (b * nq + i, q_blk + k)),
                  pl.BlockSpec(small, prev_map(k_blk)),
                  pl.BlockSpec((tq, HEAD_DIM), cur_map(k_blk)),
                  pl.BlockSpec(small, next_map(k_blk)),
                  pl.BlockSpec(small, prev_map(v_blk)),
                  pl.BlockSpec((tq, HEAD_DIM), cur_map(v_blk)),
                  pl.BlockSpec(small, next_map(v_blk))],
        out_specs=pl.BlockSpec((tq, gw), lambda b, k, i: (b * nq + i, k)),
        out_shape=jax.ShapeDtypeStruct((bsz * s_len, q_w), BF16),
        scratch_shapes=[pltpu.VMEM((kvh, GROUP * BLOCK, 3 * BLOCK), F32)],
        compiler_params=_params(3),
        name="attn_b",
    )(rel_bias, sink, bucket, proj, proj, proj, proj, proj, proj, proj)


def _outproj_kernel(oa_ref, ob_ref, ga0_ref, ga1_ref, gb0_ref, gb1_ref, x_ref, bg_ref, wpa_ref, wpb_ref,
                    wo_ref, gf_ref, wr_ref, x1_ref, h2p_ref, aff_ref):
    tm, d = x_ref.shape
    half = d // 2
    oa = oa_ref[...]
    ob = ob_ref[...]
    parts = []
    for c, (ga_ref, gb_ref) in enumerate(((ga0_ref, gb0_ref), (ga1_ref, gb1_ref))):
        cols = slice(c * half, (c + 1) * half)
        pa = jnp.dot(oa, wpa_ref[:, cols], preferred_element_type=F32)
        pb = jnp.dot(ob, wpb_ref[:, cols], preferred_element_type=F32)
        gate_a = jax.nn.sigmoid(ga_ref[...].astype(F32) + bg_ref[:, cols])
        gate_b = jax.nn.sigmoid(gb_ref[...].astype(F32) + bg_ref[:, d + c * half:d + (c + 1) * half])
        parts.append((gate_a * pa + gate_b * pb).astype(BF16))
    x1 = x_ref[...] + (jnp.dot(parts[0], wo_ref[:half, :], preferred_element_type=F32)
                       + jnp.dot(parts[1], wo_ref[half:, :], preferred_element_type=F32))
    x1_ref[...] = x1
    ms = jnp.mean(x1 * x1, axis=-1, keepdims=True)
    h = x1 * lax.rsqrt(ms + EPS) * gf_ref[...]
    h_hi = h.astype(BF16)
    bits = pltpu.bitcast(h_hi.astype(F32), U32)
    packed = (bits[:, :half] >> 16) | (bits[:, half:] & jnp.uint32(0xFFFF0000))
    for s in range(half // LANES):
        h2p_ref[pl.ds(s, tm, stride=half // LANES), :] = packed[:, s * LANES:(s + 1) * LANES]
    h_lo = (h - h_hi.astype(F32)).astype(BF16)
    prod = (jnp.dot(h_hi, wr_ref[...], preferred_element_type=F32)
            + jnp.dot(h_lo, wr_ref[...], preferred_element_type=F32))
    logits = (prod[:, :LANES] + prod[:, LANES:]).T[:N_EXPERTS, :]
    ex = jnp.exp(logits - jnp.max(logits, axis=0, keepdims=True))
    aff_ref[...] = ex / jnp.sum(ex, axis=0, keepdims=True)


def _split_router(w):
    w_hi = w.astype(BF16)
    w_lo = (w - w_hi.astype(F32)).astype(BF16)
    pad = ((0, 0), (0, LANES - w.shape[1]))
    return jnp.concatenate([jnp.pad(w_hi, pad), jnp.pad(w_lo, pad)], axis=1)


def _outproj(oa, ob, proj, x2d, b_gate, wpa, wpb, wo, g_ffn, wr_split, bsz, s_len, tm):
    t, d = x2d.shape
    half = d // 2
    nt = s_len // tm
    row = lambda c: (lambda i: (i, c))
    const = lambda i: (0, 0)
    return pl.pallas_call(
        _outproj_kernel,
        grid=(t // tm,),
        in_specs=[pl.BlockSpec((tm, half), row(0)),
                  pl.BlockSpec((tm, half), row(0)),
                  pl.BlockSpec((tm, half), row(3)),
                  pl.BlockSpec((tm, half), row(4)),
                  pl.BlockSpec((tm, half), row(5)),
                  pl.BlockSpec((tm, half), row(6)),
                  pl.BlockSpec((tm, d), row(0)),
                  pl.BlockSpec((1, 2 * d), const),
                  pl.BlockSpec((half, d), const),
                  pl.BlockSpec((half, d), const),
                  pl.BlockSpec((d, d), const),
                  pl.BlockSpec((1, d), const),
                  pl.BlockSpec((d, 2 * LANES), const)],
        out_specs=[pl.BlockSpec((tm, d), row(0)),
                   pl.BlockSpec((tm * (half // LANES), LANES), row(0)),
                   pl.BlockSpec((None, N_EXPERTS, tm), lambda i: (i // nt, 0, i % nt))],
        out_shape=[jax.ShapeDtypeStruct((t, d), F32),
                   jax.ShapeDtypeStruct((t * (half // LANES), LANES), U32),
                   jax.ShapeDtypeStruct((bsz, N_EXPERTS, s_len), F32)],
        compiler_params=_params(1),
        name="outproj",
    )(oa, ob, proj, proj, proj, proj, x2d, b_gate, wpa, wpb, wo, g_ffn, wr_split)


def _topk_kernel(a_ref, idx_ref, gate_ref, cumx_ref, sel_ref, *, cap):
    a = a_ref[...]
    n_e, n_r, _ = a.shape
    bits = pltpu.bitcast(a, I32)

    def total(x):
        return jnp.sum(jnp.sum(x, axis=1, keepdims=True), axis=2, keepdims=True)

    def search(_, carry):
        lo, hi = carry
        mid = lo + ((hi - lo) >> 1)
        enough = total(jnp.where(bits >= mid, 1.0, 0.0)) >= cap
        return jnp.where(enough, mid, lo), jnp.where(enough, hi, mid)

    lo0 = jnp.zeros((n_e, 1, 1), I32)
    hi0 = jnp.full((n_e, 1, 1), 0x7F800000, I32)
    thr, _ = lax.fori_loop(0, 31, search, (lo0, hi0))

    ri = lax.broadcasted_iota(I32, (LANES, LANES), 0)
    ci = lax.broadcasted_iota(I32, (LANES, LANES), 1)
    upper = jnp.where(ri <= ci, 1.0, 0.0).astype(BF16)
    ones = jnp.ones((LANES, LANES), BF16)
    rr = lax.broadcasted_iota(I32, (n_r, n_r), 0)
    rc = lax.broadcasted_iota(I32, (n_r, n_r), 1)
    strict_lower = jnp.where(rc < rr, 1.0, 0.0).astype(BF16)

    def prefix(x):
        x2 = x.reshape(n_e * n_r, LANES).astype(BF16)
        in_row = jnp.dot(x2, upper, preferred_element_type=F32).reshape(n_e, n_r, LANES)
        row_tot = jnp.dot(x2, ones, preferred_element_type=F32).reshape(n_e, n_r, LANES)
        before = jnp.stack([jnp.dot(strict_lower, row_tot[e].astype(BF16), preferred_element_type=F32)
                            for e in range(n_e)], axis=0)
        return in_row + before

    above = jnp.where(bits > thr, 1.0, 0.0)
    tied = jnp.where(bits == thr, 1.0, 0.0)
    need = cap - total(above)
    sel = above + tied * jnp.where(prefix(tied) <= need, 1.0, 0.0)
    cum = prefix(sel)
    sel_ref[...] = sel
    cumx_ref[...] = cum - sel

    slot = lax.broadcasted_iota(I32, (1, cap), 1).astype(F32)
    row_id = lax.broadcasted_iota(I32, (n_r, 1), 0).astype(F32)
    lane_id = lax.broadcasted_iota(I32, (LANES, 1), 0).astype(F32)
    for e in range(n_e):
        c = cum[e]
        row_end = c[:, LANES - 1:LANES]
        srow = jnp.sum(jnp.where(row_end <= slot, 1.0, 0.0), axis=0, keepdims=True)
        pick = jnp.where(row_id == srow, 1.0, 0.0)
        c_row = lax.dot_general(c, pick, TN, precision=HIGHEST, preferred_element_type=F32)
        slane = jnp.sum(jnp.where(c_row <= slot, 1.0, 0.0), axis=0, keepdims=True)
        a_row = lax.dot_general(a[e], pick, TN, precision=HIGHEST, preferred_element_type=F32)
        idx_ref[e:e + 1, :] = (srow * LANES + slane).astype(I32)
        gate_ref[e:e + 1, :] = jnp.sum(jnp.where(lane_id == slane, a_row, 0.0), axis=0, keepdims=True)


def _topk(aff4, cap):
    bsz, n_e, n_r, _ = aff4.shape
    blk4 = pl.BlockSpec((None, n_e, n_r, LANES), lambda b: (b, 0, 0, 0))
    blk3 = pl.BlockSpec((None, n_e, cap), lambda b: (b, 0, 0))
    return pl.pallas_call(
        functools.partial(_topk_kernel, cap=cap),
        grid=(bsz,),
        in_specs=[blk4],
        out_specs=[blk3, blk3, blk4, blk4],
        out_shape=[jax.ShapeDtypeStruct((bsz, n_e, cap), I32),
                   jax.ShapeDtypeStruct((bsz, n_e, cap), F32),
                   jax.ShapeDtypeStruct(aff4.shape, F32),
                   jax.ShapeDtypeStruct(aff4.shape, F32)],
        compiler_params=_params(1),
        name="topk",
    )(aff4)


def _ffn_kernel(idx_ref, h2_hbm, gate_ref, wg_ref, wu_ref, wd_ref, y_ref, xa, xb, sem, *,
                bsz, s_len, cap, ch, rpt, unroll):
    n_c = cap // ch
    bufs = (xa, xb)
    pair = pl.program_id(0) * bsz + pl.program_id(1)
    n_pairs = pl.num_programs(0) * bsz

    def issue(pair_k, c):
        bb = pair_k % bsz
        ee = pair_k // bsz
        idx_base = (bb * N_EXPERTS + ee) * cap + c * ch
        row_base = bb * s_len
        buf = bufs[c % 2]

        def body(j, carry):
            for u in range(unroll):
                r = j * unroll + u
                tok = idx_ref[idx_base + r]
                src = pl.multiple_of((row_base + tok) * rpt, rpt)
                dst = pl.multiple_of(r * rpt, rpt)
                pltpu.make_async_copy(h2_hbm.at[pl.ds(src, rpt), :], buf.at[pl.ds(dst, rpt), :],
                                      sem.at[c % 2]).start()
            return carry
        lax.fori_loop(0, ch // unroll, body, 0)

    @pl.when(pair == 0)
    def _():
        issue(pair, 0)

    for c in range(n_c):
        if c + 1 < n_c:
            issue(pair, c + 1)
        else:
            @pl.when(pair + 1 < n_pairs)
            def _():
                issue(pair + 1, 0)
        buf = bufs[c % 2]
        pltpu.make_async_copy(h2_hbm.at[pl.ds(0, ch * rpt), :], buf, sem.at[c % 2]).wait()
        lo, hi = [], []
        for s in range(rpt):
            w = buf[pl.ds(s, ch, stride=rpt), :]
            lo.append(pltpu.bitcast(w << 16, F32).astype(BF16))
            hi.append(pltpu.bitcast(w & jnp.uint32(0xFFFF0000), F32).astype(BF16))
        x = jnp.concatenate(lo + hi, axis=1)
        a = jnp.dot(x, wg_ref[...], preferred_element_type=F32)
        u = jnp.dot(x, wu_ref[...], preferred_element_type=F32)
        act = (jax.nn.silu(a) * u).astype(BF16)
        rows = slice(c * ch, (c + 1) * ch)
        y = jnp.dot(act, wd_ref[...], preferred_element_type=F32) * gate_ref[rows, :]
        y_ref[rows, :] = y.astype(y_ref.dtype)


def _ffn(idx_flat, h2p, gates4, wg, wu, wd, bsz, s_len, cap, ch):
    n_e, d, f = wg.shape
    rpt = h2p.shape[0] // (bsz * s_len)
    assert (cap // ch) % 2 == 0
    kern = functools.partial(_ffn_kernel, bsz=bsz, s_len=s_len, cap=cap, ch=ch, rpt=rpt, unroll=8)
    grid_spec = pltpu.PrefetchScalarGridSpec(
        num_scalar_prefetch=1,
        grid=(n_e, bsz),
        in_specs=[pl.BlockSpec(memory_space=pl.ANY),
                  pl.BlockSpec((None, None, cap, 1), lambda e, b, idx: (b, e, 0, 0)),
                  pl.BlockSpec((None, d, f), lambda e, b, idx: (e, 0, 0)),
                  pl.BlockSpec((None, d, f), lambda e, b, idx: (e, 0, 0)),
                  pl.BlockSpec((None, f, d), lambda e, b, idx: (e, 0, 0))],
        out_specs=pl.BlockSpec((None, None, cap, d), lambda e, b, idx: (b, e, 0, 0)),
        scratch_shapes=[pltpu.VMEM((ch * rpt, LANES), U32),
                        pltpu.VMEM((ch * rpt, LANES), U32),
                        pltpu.SemaphoreType.DMA((2,))],
    )
    return pl.pallas_call(
        kern,
        grid_spec=grid_spec,
        out_shape=jax.ShapeDtypeStruct((bsz, n_e, cap, d), BF16),
        compiler_params=_params(2),
        name="ffn",
    )(idx_flat, h2p, gates4, wg, wu, wd)


def _combine_kernel(off_ref, y_hbm, x1_ref, cumx_ref, sel_ref, gf_ref, o_ref, ybuf, yextra, sem, *,
                    cap, win, n_r, final):
    b = pl.program_id(0)
    r = pl.program_id(1)
    n = b * n_r + r
    n_tiles = pl.num_programs(0) * n_r
    slot = n % 2
    rows_all = N_EXPERTS * win

    def windows(bb, rr, k):
        obase = (bb * (n_r + 1) + rr) * N_EXPERTS
        out = []
        for e in range(N_EXPERTS):
            first = (off_ref[obase + e] // BF16_ROWS) * BF16_ROWS + k * win
            out.append((first, jnp.minimum(first, cap - win)))
        return out

    def start_copies(bb, wins, dst_ref, sem_k):
        for e, (_, begin) in enumerate(wins):
            src = pl.multiple_of((bb * N_EXPERTS + e) * cap + begin, BF16_ROWS)
            pltpu.make_async_copy(y_hbm.at[pl.ds(src, win), :], dst_ref.at[pl.ds(e * win, win), :],
                                  sem_k).start()

    def wait_copies(dst_ref, sem_k):
        pltpu.make_async_copy(y_hbm.at[pl.ds(0, rows_all), :], dst_ref, sem_k).wait()

    jcol = lax.broadcasted_iota(I32, (win, 1), 0).astype(F32)

    def expand(wins, rows_bf16):
        blocks = []
        for e, (first, begin) in enumerate(wins):
            rank = cumx_ref[e:e + 1, :]
            blocks.append(jnp.where(rank == jcol + begin.astype(F32),
                                    jnp.where(rank >= first.astype(F32), sel_ref[e:e + 1, :], 0.0), 0.0))
        onehot = jnp.concatenate(blocks, axis=0).astype(BF16)
        return lax.dot_general(onehot, rows_bf16, TN, preferred_element_type=F32)

    @pl.when(n == 0)
    def _():
        start_copies(b, windows(b, r, 0), ybuf.at[slot], sem.at[slot])

    @pl.when(n + 1 < n_tiles)
    def _():
        nb = (n + 1) // n_r
        start_copies(nb, windows(nb, (n + 1) % n_r, 0), ybuf.at[1 - slot], sem.at[1 - slot])

    obase = (b * (n_r + 1) + r) * N_EXPERTS
    rounds = jnp.int32(0)
    for e in range(N_EXPERTS):
        off = off_ref[obase + e]
        cnt = off_ref[obase + N_EXPERTS + e] - off
        used = off - (off // BF16_ROWS) * BF16_ROWS + cnt
        rounds = jnp.maximum(rounds, jnp.where(cnt > 0, (used + win - 1) // win, 0))

    wait_copies(ybuf.at[slot], sem.at[slot])
    o_ref[...] = x1_ref[...] + expand(windows(b, r, 0), ybuf[slot])

    def extra_round(k, carry):
        wins = windows(b, r, k)
        start_copies(b, wins, yextra, sem.at[2])
        wait_copies(yextra, sem.at[2])
        o_ref[...] += expand(wins, yextra[...])
        return carry
    lax.fori_loop(1, rounds, extra_round, 0)

    if final:
        x2 = o_ref[...]
        ms = jnp.mean(x2 * x2, axis=-1, keepdims=True)
        o_ref[...] = x2 * lax.rsqrt(ms + EPS) * gf_ref[...]


def _combine(off_flat, y2d, x1, cumx_t, sel_t, g_final, bsz, s_len, cap, win, final):
    t, d = x1.shape
    n_r = s_len // LANES
    kern = functools.partial(_combine_kernel, cap=cap, win=win, n_r=n_r, final=final)
    meta = pl.BlockSpec((None, None, N_EXPERTS, LANES), lambda b, r, off: (b, r, 0, 0))
    grid_spec = pltpu.PrefetchScalarGridSpec(
        num_scalar_prefetch=1,
        grid=(bsz, n_r),
        in_specs=[pl.BlockSpec(memory_space=pl.ANY),
                  pl.BlockSpec((LANES, d), lambda b, r, off: (b * n_r + r, 0)),
                  meta, meta,
                  pl.BlockSpec((1, d), lambda b, r, off: (0, 0))],
        out_specs=pl.BlockSpec((LANES, d), lambda b, r, off: (b * n_r + r, 0)),
        scratch_shapes=[pltpu.VMEM((2, N_EXPERTS * win, d), BF16),
                        pltpu.VMEM((N_EXPERTS * win, d), BF16),
                        pltpu.SemaphoreType.DMA((3,))],
    )
    return pl.pallas_call(
        kern,
        grid_spec=grid_spec,
        out_shape=jax.ShapeDtypeStruct((t, d), F32),
        compiler_params=_params(2),
        name="combine",
    )(off_flat, y2d, x1, cumx_t, sel_t, g_final)


def _rope_tables(s_len):
    rows = s_len // GRID_W
    r = jnp.repeat(jnp.arange(rows, dtype=F32), GRID_W)
    c = jnp.tile(jnp.arange(GRID_W, dtype=F32), rows)
    half = HEAD_DIM // 2
    inv = 1.0 / (ROPE_THETA ** (jnp.arange(0, half, 2, dtype=F32) / half))
    ang_r = r[:, None] * inv
    ang_c = c[:, None] * inv
    cos = jnp.concatenate([jnp.cos(ang_r), jnp.cos(ang_r), jnp.cos(ang_c), jnp.cos(ang_c)], axis=-1)
    sin = jnp.concatenate([-jnp.sin(ang_r), jnp.sin(ang_r), -jnp.sin(ang_c), jnp.sin(ang_c)], axis=-1)
    return cos, sin


def _t5_bucket_table():
    rel = (jnp.arange(3 * BLOCK) - BLOCK)[None, :] - jnp.arange(BLOCK)[:, None]
    half = NUM_BUCKETS // 2
    ret = jnp.where(rel > 0, half, 0)
    n = jnp.abs(rel)
    max_exact = half // 2
    nf = jnp.maximum(n, 1).astype(F32)
    large = max_exact + (jnp.log(nf / max_exact) / math.log(MAX_DISTANCE / max_exact)
                         * (half - max_exact)).astype(I32)
    large = jnp.minimum(large, half - 1)
    return (ret + jnp.where(n < max_exact, n, large)).astype(I32)


def kernel(x, g_mix, w_in, b_gate, qn_a, kn_a, w_proj_a, sink_b, rel_bias, w_proj_b, w_o, g_ffn, w_router,
           w_gate_e, w_up_e, w_down_e, g_final):
    bsz, s_len, d = x.shape
    depth = g_mix.shape[0]
    t = bsz * s_len
    cap = CAPACITY_FACTOR * s_len // N_EXPERTS
    n_r = s_len // LANES
    cos, sin = _rope_tables(s_len)
    bucket = _t5_bucket_table()
    x2d = x.reshape(t, d)
    for l in range(depth):
        proj = _inproj(x2d, g_mix[l][None, :], w_in[l].astype(BF16), tm=min(1024, t), tn=1024)
        oa = _attn_a(proj, cos, sin, qn_a[l][None, :], kn_a[l][None, :], bsz, s_len, d,
                     tq=min(256, s_len), tk=min(512, s_len))
        ob = _attn_b(proj, bucket, rel_bias, sink_b[l], bsz, s_len, d, tq=min(512, s_len))
        x1, h2p, aff = _outproj(oa, ob, proj, x2d, b_gate[l][None, :], w_proj_a[l].astype(BF16),
                                w_proj_b[l].astype(BF16), w_o[l].astype(BF16), g_ffn[l][None, :],
                                _split_router(w_router[l]), bsz, s_len, tm=256)
        idx, gates, cumx, sel = _topk(aff.reshape(bsz, N_EXPERTS, n_r, LANES), cap)
        y = _ffn(idx.reshape(-1), h2p, gates[..., None], w_gate_e[l].astype(BF16), w_up_e[l].astype(BF16),
                 w_down_e[l].astype(BF16), bsz, s_len, cap, ch=min(512, cap // 2))
        off = jnp.concatenate([cumx[:, :, :, 0].astype(I32), jnp.full((bsz, N_EXPERTS, 1), cap, I32)], axis=2)
        off_flat = off.transpose(0, 2, 1).reshape(-1)
        x2d = _combine(off_flat, y.reshape(bsz * N_EXPERTS * cap, d), x1, cumx.transpose(0, 2, 1, 3),
                       sel.transpose(0, 2, 1, 3), g_final[None, :], bsz, s_len, cap, win=48,
                       final=(l == depth - 1))
    return x2d.reshape(bsz, s_len, d)
```

```python
import functools
import math

import jax
import jax.numpy as jnp
from jax import lax
from jax.experimental import pallas as pl
from jax.experimental.pallas import tpu as pltpu

F32 = jnp.float32
BF16 = jnp.bfloat16
I32 = jnp.int32
U32 = jnp.uint32

HEAD_DIM = 128
GROUP = 4
BLOCK = 128
GRID_W = 64
ROPE_THETA = 10000.0
NUM_BUCKETS = 32
MAX_DISTANCE = 128
N_EXPERTS = 16
CAPACITY_FACTOR = 2
EPS = 1e-6
NEG = -1e30
LANES = 128
SUBLANES = 8
BF16_ROWS = 16
ONES_ROWS = BF16_ROWS
KV_UNROLL = 8
VMEM_LIMIT = 56 * 1024 * 1024
HIGHEST = lax.Precision.HIGHEST
NT = (((1,), (1,)), ((), ()))
TN = (((0,), (0,)), ((), ()))


def _params(n_axes):
    return pltpu.CompilerParams(dimension_semantics=("arbitrary",) * n_axes,
                                vmem_limit_bytes=VMEM_LIMIT)


def _inproj_kernel(x_ref, g_ref, w_ref, o_ref, h_ref):
    @pl.when(pl.program_id(1) == 0)
    def _():
        x = x_ref[...]
        ms = jnp.mean(x * x, axis=-1, keepdims=True)
        h_ref[...] = (x * lax.rsqrt(ms + EPS) * g_ref[...]).astype(BF16)

    o_ref[...] = jnp.dot(h_ref[...], w_ref[...], preferred_element_type=F32).astype(o_ref.dtype)


def _inproj(x2d, g, w, tm, tn):
    t, d = x2d.shape
    n = w.shape[1]
    return pl.pallas_call(
        _inproj_kernel,
        grid=(t // tm, n // tn),
        in_specs=[pl.BlockSpec((tm, d), lambda i, j: (i, 0)),
                  pl.BlockSpec((1, d), lambda i, j: (0, 0)),
                  pl.BlockSpec((d, tn), lambda i, j: (0, j))],
        out_specs=pl.BlockSpec((tm, tn), lambda i, j: (i, j)),
        out_shape=jax.ShapeDtypeStruct((t, n), BF16),
        scratch_shapes=[pltpu.VMEM((tm, d), BF16)],
        compiler_params=_params(2),
        name="inproj",
    )(x2d, g, w)


def _norm_rope(xb, gain, gain_sw, cos, sin_signed):
    a = lax.broadcasted_iota(I32, (HEAD_DIM, HEAD_DIM), 0)
    c = lax.broadcasted_iota(I32, (HEAD_DIM, HEAD_DIM), 1)
    swap_mat = jnp.where(a == (c ^ 32), 1.0, 0.0).astype(BF16)
    ones_mat = jnp.ones((HEAD_DIM, HEAD_DIM), BF16)
    x = xb.astype(F32)
    sq = x * x
    sq_hi = sq.astype(BF16)
    sq_lo = (sq - sq_hi.astype(F32)).astype(BF16)
    ms = (jnp.dot(sq_hi, ones_mat, preferred_element_type=F32)
          + jnp.dot(sq_lo, ones_mat, preferred_element_type=F32)) * (1.0 / HEAD_DIM)
    swapped = jnp.dot(xb, swap_mat, preferred_element_type=F32)
    return lax.rsqrt(ms + EPS) * (x * (gain * cos) + swapped * (gain_sw * sin_signed))


def _attn_a_kernel(q_ref, k_ref, v_ref, cq_ref, sq_ref, ck_ref, sk_ref, qn_ref, kn_ref, o_ref,
                   kp_ref, vt_ref, qt_ref, m_ref, acc_ref, st0_ref, st1_ref, *, tq, tk):
    s_len = k_ref.shape[0]
    n_chunks = s_len // tk

    @pl.when(pl.program_id(2) == 0)
    def _():
        def body(c, carry):
            off = pl.multiple_of(c * tk, tk)
            kk = _norm_rope(k_ref[pl.ds(off, tk), :], kn_ref[0:1, :], kn_ref[1:2, :],
                            ck_ref[pl.ds(off, tk), :], sk_ref[pl.ds(off, tk), :])
            kp_ref[pl.ds(off, tk), :] = kk.astype(BF16)
            vt_ref[c, :HEAD_DIM, :] = v_ref[pl.ds(off, tk), :].astype(F32).T.astype(BF16)
            vt_ref[c, HEAD_DIM:, :] = jnp.ones((ONES_ROWS, tk), BF16)
            return carry
        lax.fori_loop(0, n_chunks, body, 0)

    scale = math.log2(math.e) / math.sqrt(HEAD_DIM)
    for g in range(GROUP):
        qg = _norm_rope(q_ref[:, g * HEAD_DIM:(g + 1) * HEAD_DIM], qn_ref[0:1, :], qn_ref[1:2, :],
                        cq_ref[...], sq_ref[...]) * scale
        qt_ref[g] = qg.T.astype(BF16)

    m_ref[...] = jnp.full(m_ref.shape, -jnp.inf, F32)
    acc_ref[...] = jnp.zeros(acc_ref.shape, F32)

    def scores(c, dst_ref):
        off = pl.multiple_of(c * tk, tk)
        kc = kp_ref[pl.ds(off, tk), :]
        for g in range(GROUP):
            dst_ref[g] = jnp.dot(kc, qt_ref[g], preferred_element_type=F32)

    def softmax_pv(c, src_ref):
        vt = vt_ref[c]
        for g in range(GROUP):
            st = src_ref[g]
            m_prev = m_ref[g]
            m_new = jnp.maximum(m_prev, jnp.max(st, axis=0, keepdims=True))
            pt = jnp.exp2(st - m_new).astype(BF16)
            acc_ref[g] = jnp.exp2(m_prev - m_new) * acc_ref[g] + jnp.dot(vt, pt, preferred_element_type=F32)
            m_ref[g] = m_new

    scores(0, st0_ref)

    unroll = math.gcd(KV_UNROLL, n_chunks)

    def kv_body(j, carry):
        c = unroll * j
        bufs = (st0_ref, st1_ref)
        for u in range(unroll):
            scores(jnp.minimum(c + u + 1, n_chunks - 1), bufs[(u + 1) % 2])
            softmax_pv(c + u, bufs[u % 2])
        return carry
    lax.fori_loop(0, n_chunks // unroll, kv_body, 0)

    for g in range(GROUP):
        acc = acc_ref[g]
        o = (acc[:HEAD_DIM] / acc[HEAD_DIM:HEAD_DIM + 1]).T
        o_ref[:, g * HEAD_DIM:(g + 1) * HEAD_DIM] = o.astype(o_ref.dtype)


def _attn_a(proj, cos, sin, qn, kn, bsz, s_len, d, tq, tk):
    q_w = d // 2
    kv_w = d // 8
    kvh = kv_w // HEAD_DIM
    gw = GROUP * HEAD_DIM
    nq = s_len // tq
    k_blk = q_w // HEAD_DIM
    v_blk = (q_w + kv_w) // HEAD_DIM
    assert (s_len // tk) % 2 == 0

    def with_swapped(gain):
        return jnp.stack([gain, gain.reshape(2, 2, 32)[:, ::-1, :].reshape(HEAD_DIM)])

    qn, kn = with_swapped(qn), with_swapped(kn)
    kern = functools.partial(_attn_a_kernel, tq=tq, tk=tk)
    return pl.pallas_call(
        kern,
        grid=(bsz, kvh, nq),
        in_specs=[pl.BlockSpec((tq, gw), lambda b, k, i: (b * nq + i, k)),
                  pl.BlockSpec((s_len, HEAD_DIM), lambda b, k, i: (b, k_blk + k)),
                  pl.BlockSpec((s_len, HEAD_DIM), lambda b, k, i: (b, v_blk + k)),
                  pl.BlockSpec((tq, HEAD_DIM), lambda b, k, i: (i, 0)),
                  pl.BlockSpec((tq, HEAD_DIM), lambda b, k, i: (i, 0)),
                  pl.BlockSpec((s_len, HEAD_DIM), lambda b, k, i: (0, 0)),
                  pl.BlockSpec((s_len, HEAD_DIM), lambda b, k, i: (0, 0)),
                  pl.BlockSpec((2, HEAD_DIM), lambda b, k, i: (0, 0)),
                  pl.BlockSpec((2, HEAD_DIM), lambda b, k, i: (0, 0))],
        out_specs=pl.BlockSpec((tq, gw), lambda b, k, i: (b * nq + i, k)),
        out_shape=jax.ShapeDtypeStruct((bsz * s_len, q_w), BF16),
        scratch_shapes=[pltpu.VMEM((s_len, HEAD_DIM), BF16),
                        pltpu.VMEM((s_len // tk, HEAD_DIM + ONES_ROWS, tk), BF16),
                        pltpu.VMEM((GROUP, HEAD_DIM, tq), BF16),
                        pltpu.VMEM((GROUP, 1, tq), F32),
                        pltpu.VMEM((GROUP, HEAD_DIM + ONES_ROWS, tq), F32),
                        pltpu.VMEM((GROUP, tk, tq), F32),
                        pltpu.VMEM((GROUP, tk, tq), F32)],
        compiler_params=_params(3),
        name="attn_a",
    )(proj, proj, proj, cos, sin, cos, sin, qn, kn)


def _attn_b_kernel(relb_ref, sink_ref, bucket_ref, q_ref, kp_ref, kc_ref, kn_ref, vp_ref, vc_ref, vn_ref,
                   o_ref, bias_ref, *, tq, s_len, kvh):
    b = pl.program_id(0)
    k = pl.program_id(1)
    i = pl.program_id(2)
    band = 3 * BLOCK
    log2e = math.log2(math.e)

    @pl.when((b == 0) & (k == 0) & (i == 0))
    def _():
        bucket_t = bucket_ref[...]
        key = lax.broadcasted_iota(I32, (band, BLOCK), 0)
        qry = lax.broadcasted_iota(I32, (band, BLOCK), 1)
        in_window = jnp.abs(key - BLOCK - qry) <= BLOCK
        for kk in range(kvh):
            for g in range(GROUP):
                h = kk * GROUP + g
                tab = jnp.zeros((band, BLOCK), F32)
                for bkt in range(NUM_BUCKETS):
                    tab = jnp.where(bucket_t == bkt, relb_ref[bkt, h], tab)
                tab = jnp.where(in_window, tab * log2e, NEG)
                cols = slice(g * BLOCK, (g + 1) * BLOCK)
                bias_ref[0, kk, :, cols] = tab
                bias_ref[1, kk, :, cols] = jnp.where(key >= BLOCK, tab, NEG)
                bias_ref[2, kk, :, cols] = jnp.where(key < 2 * BLOCK, tab, NEG)

    nsub = tq // BLOCK
    kcat = jnp.concatenate([kp_ref[...], kc_ref[...], kn_ref[...]], axis=0)
    vcat = jnp.concatenate([vp_ref[...], vc_ref[...], vn_ref[...]], axis=0)
    vt = jnp.concatenate([vcat.astype(F32).T.astype(BF16), jnp.ones((ONES_ROWS, vcat.shape[0]), BF16)], axis=0)
    sink = jnp.concatenate([jnp.full((1, BLOCK), sink_ref[k * GROUP + g], F32) for g in range(GROUP)],
                           axis=1) * log2e
    scale = log2e / math.sqrt(HEAD_DIM)
    for jb in range(nsub):
        rows = slice(jb * BLOCK, (jb + 1) * BLOCK)
        qt = jnp.concatenate([q_ref[rows, g * HEAD_DIM:(g + 1) * HEAD_DIM].astype(F32).T.astype(BF16)
                              for g in range(GROUP)], axis=1)
        if jb == 0:
            variant = jnp.where(i == 0, 1, 0)
        elif jb == nsub - 1:
            variant = jnp.where(i == pl.num_programs(2) - 1, 2, 0)
        else:
            variant = 0
        st = (jnp.dot(kcat[jb * BLOCK:jb * BLOCK + band], qt, preferred_element_type=F32) * scale
              + bias_ref[variant, k])
        m = jnp.maximum(jnp.max(st, axis=0, keepdims=True), sink)
        e = jnp.exp2(st - m).astype(BF16)
        acc = jnp.dot(vt[:, jb * BLOCK:jb * BLOCK + band], e, preferred_element_type=F32)
        o = acc[:HEAD_DIM] / (acc[HEAD_DIM:HEAD_DIM + 1] + jnp.exp2(sink - m))
        for g in range(GROUP):
            o_ref[rows, g * HEAD_DIM:(g + 1) * HEAD_DIM] = o[:, g * BLOCK:(g + 1) * BLOCK].T.astype(o_ref.dtype)


def _attn_b(proj, bucket, rel_bias, sink, bsz, s_len, d, tq):
    q_w = d // 2
    kv_w = d // 8
    kvh = kv_w // HEAD_DIM
    gw = GROUP * HEAD_DIM
    nq = s_len // tq
    sub = tq // BLOCK
    nblk = s_len // BLOCK
    q_blk = (q_w + 2 * kv_w) // gw
    k_blk = (2 * q_w + 2 * kv_w) // HEAD_DIM
    v_blk = (2 * q_w + 3 * kv_w) // HEAD_DIM

    def prev_map(col):
        return lambda b, k, i: (b * nblk + jnp.maximum(i * sub - 1, 0), col + k)

    def cur_map(col):
        return lambda b, k, i: (b * nq + i, col + k)

    def next_map(col):
        return lambda b, k, i: (b * nblk + jnp.minimum((i + 1) * sub, nblk - 1), col + k)

    small = (BLOCK, HEAD_DIM)
    assert sub >= 2
    kern = functools.partial(_attn_b_kernel, tq=tq, s_len=s_len, kvh=kvh)
    return pl.pallas_call(
        kern,
        grid=(bsz, kvh, nq),
        in_specs=[pl.BlockSpec(memory_space=pltpu.SMEM),
                  pl.BlockSpec(memory_space=pltpu.SMEM),
                  pl.BlockSpec((3 * BLOCK, BLOCK), lambda b, k, i: (0, 0)),
                  pl.BlockSpec((tq, gw), lambda b, k, i: (b * nq + i, q_blk + k)),
                  pl.BlockSpec(small, prev_map(k_blk)),
                  pl.BlockSpec((tq, HEAD_DIM), cur_map(k_blk)),
                  pl.BlockSpec(small, next_map(k_blk)),
                  pl.BlockSpec(small, prev_map(v_blk)),
                  pl.BlockSpec((tq, HEAD_DIM), cur_map(v_blk)),
                  pl.BlockSpec(small, next_map(v_blk))],
        out_specs=pl.BlockSpec((tq, gw), lambda b, k, i: (b * nq + i, k)),
        out_shape=jax.ShapeDtypeStruct((bsz * s_len, q_w), BF16),
        scratch_shapes=[pltpu.VMEM((3, kvh, 3 * BLOCK, GROUP * BLOCK), F32)],
        compiler_params=_params(3),
        name="attn_b",
    )(rel_bias, sink, bucket.T, proj, proj, proj, proj, proj, proj, proj)


def _outproj_kernel(oa_ref, ob_ref, ga0_ref, ga1_ref, gb0_ref, gb1_ref, x_ref, bg_ref, wpa_ref, wpb_ref,
                    wo_ref, gf_ref, wr_ref, x1_ref, h2p_ref, aff_ref):
    tm, d = x_ref.shape
    half = d // 2
    oa = oa_ref[...]
    ob = ob_ref[...]
    parts = []
    for c, (ga_ref, gb_ref) in enumerate(((ga0_ref, gb0_ref), (ga1_ref, gb1_ref))):
        cols = slice(c * half, (c + 1) * half)
        pa = jnp.dot(oa, wpa_ref[:, cols], preferred_element_type=F32)
        pb = jnp.dot(ob, wpb_ref[:, cols], preferred_element_type=F32)
        gate_a = jax.nn.sigmoid(ga_ref[...].astype(F32) + bg_ref[:, cols])
        gate_b = jax.nn.sigmoid(gb_ref[...].astype(F32) + bg_ref[:, d + c * half:d + (c + 1) * half])
        parts.append((gate_a * pa + gate_b * pb).astype(BF16))
    x1 = x_ref[...] + (jnp.dot(parts[0], wo_ref[:half, :], preferred_element_type=F32)
                       + jnp.dot(parts[1], wo_ref[half:, :], preferred_element_type=F32))
    x1_ref[...] = x1
    ms = jnp.mean(x1 * x1, axis=-1, keepdims=True)
    h = x1 * lax.rsqrt(ms + EPS) * gf_ref[...]
    h_hi = h.astype(BF16)
    bits = pltpu.bitcast(h_hi.astype(F32), U32)
    packed = (bits[:, :half] >> 16) | (bits[:, half:] & jnp.uint32(0xFFFF0000))
    for s in range(half // LANES):
        h2p_ref[pl.ds(s, tm, stride=half // LANES), :] = packed[:, s * LANES:(s + 1) * LANES]
    h_lo = (h - h_hi.astype(F32)).astype(BF16)
    prod = (jnp.dot(h_hi, wr_ref[...], preferred_element_type=F32)
            + jnp.dot(h_lo, wr_ref[...], preferred_element_type=F32))
    logits = (prod[:, :LANES] + prod[:, LANES:]).T[:N_EXPERTS, :]
    ex = jnp.exp(logits - jnp.max(logits, axis=0, keepdims=True))
    aff_ref[...] = ex / jnp.sum(ex, axis=0, keepdims=True)


def _split_router(w):
    w_hi = w.astype(BF16)
    w_lo = (w - w_hi.astype(F32)).astype(BF16)
    pad = ((0, 0), (0, LANES - w.shape[1]))
    return jnp.concatenate([jnp.pad(w_hi, pad), jnp.pad(w_lo, pad)], axis=1)


def _outproj(oa, ob, proj, x2d, b_gate, wpa, wpb, wo, g_ffn, wr_split, bsz, s_len, tm):
    t, d = x2d.shape
    half = d // 2
    nt = s_len // tm
    row = lambda c: (lambda i: (i, c))
    const = lambda i: (0, 0)
    return pl.pallas_call(
        _outproj_kernel,
        grid=(t // tm,),
        in_specs=[pl.BlockSpec((tm, half), row(0)),
                  pl.BlockSpec((tm, half), row(0)),
                  pl.BlockSpec((tm, half), row(3)),
                  pl.BlockSpec((tm, half), row(4)),
                  pl.BlockSpec((tm, half), row(5)),
                  pl.BlockSpec((tm, half), row(6)),
                  pl.BlockSpec((tm, d), row(0)),
                  pl.BlockSpec((1, 2 * d), const),
                  pl.BlockSpec((half, d), const),
                  pl.BlockSpec((half, d), const),
                  pl.BlockSpec((d, d), const),
                  pl.BlockSpec((1, d), const),
                  pl.BlockSpec((d, 2 * LANES), const)],
        out_specs=[pl.BlockSpec((tm, d), row(0)),
                   pl.BlockSpec((tm * (half // LANES), LANES), row(0)),
                   pl.BlockSpec((None, N_EXPERTS, tm), lambda i: (i // nt, 0, i % nt))],
        out_shape=[jax.ShapeDtypeStruct((t, d), F32),
                   jax.ShapeDtypeStruct((t * (half // LANES), LANES), U32),
                   jax.ShapeDtypeStruct((bsz, N_EXPERTS, s_len), F32)],
        compiler_params=_params(1),
        name="outproj",
    )(oa, ob, proj, proj, proj, proj, x2d, b_gate, wpa, wpb, wo, g_ffn, wr_split)


def _topk_kernel(a_ref, idx_ref, gate_ref, cumx_ref, sel_ref, *, cap):
    a = a_ref[...]
    n_e, n_r, _ = a.shape
    bits = pltpu.bitcast(a, I32)

    def total(x):
        return jnp.sum(jnp.sum(x, axis=1, keepdims=True), axis=2, keepdims=True)

    def search(_, carry):
        lo, hi = carry
        mid = lo + ((hi - lo) >> 1)
        enough = total(jnp.where(bits >= mid, 1.0, 0.0)) >= cap
        return jnp.where(enough, mid, lo), jnp.where(enough, hi, mid)

    lo0 = jnp.zeros((n_e, 1, 1), I32)
    hi0 = jnp.full((n_e, 1, 1), 0x7F800000, I32)
    thr, _ = lax.fori_loop(0, 31, search, (lo0, hi0))

    ri = lax.broadcasted_iota(I32, (LANES, LANES), 0)
    ci = lax.broadcasted_iota(I32, (LANES, LANES), 1)
    upper = jnp.where(ri <= ci, 1.0, 0.0).astype(BF16)
    ones = jnp.ones((LANES, LANES), BF16)
    rr = lax.broadcasted_iota(I32, (n_r, n_r), 0)
    rc = lax.broadcasted_iota(I32, (n_r, n_r), 1)
    strict_lower = jnp.where(rc < rr, 1.0, 0.0).astype(BF16)

    def prefix(x):
        x2 = x.reshape(n_e * n_r, LANES).astype(BF16)
        in_row = jnp.dot(x2, upper, preferred_element_type=F32).reshape(n_e, n_r, LANES)
        row_tot = jnp.dot(x2, ones, preferred_element_type=F32).reshape(n_e, n_r, LANES)
        before = jnp.stack([jnp.dot(strict_lower, row_tot[e].astype(BF16), preferred_element_type=F32)
                            for e in range(n_e)], axis=0)
        return in_row + before

    above = jnp.where(bits > thr, 1.0, 0.0)
    tied = jnp.where(bits == thr, 1.0, 0.0)
    need = cap - total(above)
    sel = above + tied * jnp.where(prefix(tied) <= need, 1.0, 0.0)
    cum = prefix(sel)
    sel_ref[...] = sel
    cumx_ref[...] = cum - sel

    slot = lax.broadcasted_iota(I32, (1, cap), 1).astype(F32)
    row_id = lax.broadcasted_iota(I32, (n_r, 1), 0).astype(F32)
    lane_id = lax.broadcasted_iota(I32, (LANES, 1), 0).astype(F32)
    for e in range(n_e):
        c = cum[e]
        row_end = c[:, LANES - 1:LANES]
        srow = jnp.sum(jnp.where(row_end <= slot, 1.0, 0.0), axis=0, keepdims=True)
        pick = jnp.where(row_id == srow, 1.0, 0.0)
        c_row = lax.dot_general(c, pick, TN, precision=HIGHEST, preferred_element_type=F32)
        slane = jnp.sum(jnp.where(c_row <= slot, 1.0, 0.0), axis=0, keepdims=True)
        a_row = lax.dot_general(a[e], pick, TN, precision=HIGHEST, preferred_element_type=F32)
        idx_ref[e:e + 1, :] = (srow * LANES + slane).astype(I32)
        gate_ref[e:e + 1, :] = jnp.sum(jnp.where(lane_id == slane, a_row, 0.0), axis=0, keepdims=True)


def _topk(aff4, cap):
    bsz, n_e, n_r, _ = aff4.shape
    blk4 = pl.BlockSpec((None, n_e, n_r, LANES), lambda b: (b, 0, 0, 0))
    blk3 = pl.BlockSpec((None, n_e, cap), lambda b: (b, 0, 0))
    return pl.pallas_call(
        functools.partial(_topk_kernel, cap=cap),
        grid=(bsz,),
        in_specs=[blk4],
        out_specs=[blk3, blk3, blk4, blk4],
        out_shape=[jax.ShapeDtypeStruct((bsz, n_e, cap), I32),
                   jax.ShapeDtypeStruct((bsz, n_e, cap), F32),
                   jax.ShapeDtypeStruct(aff4.shape, F32),
                   jax.ShapeDtypeStruct(aff4.shape, F32)],
        compiler_params=_params(1),
        name="topk",
    )(aff4)


def _ffn_kernel(idx_ref, h2_hbm, gate_ref, wg_ref, wu_ref, wd_ref, y_ref, xa, xb, sem, *,
                bsz, s_len, cap, ch, rpt, unroll):
    n_c = cap // ch
    bufs = (xa, xb)
    pair = pl.program_id(0) * bsz + pl.program_id(1)
    n_pairs = pl.num_programs(0) * bsz

    def issue(pair_k, c):
        bb = pair_k % bsz
        ee = pair_k // bsz
        idx_base = (bb * N_EXPERTS + ee) * cap + c * ch
        row_base = bb * s_len
        buf = bufs[c % 2]

        def body(j, carry):
            for u in range(unroll):
                r = j * unroll + u
                tok = idx_ref[idx_base + r]
                src = pl.multiple_of((row_base + tok) * rpt, rpt)
                dst = pl.multiple_of(r * rpt, rpt)
                pltpu.make_async_copy(h2_hbm.at[pl.ds(src, rpt), :], buf.at[pl.ds(dst, rpt), :],
                                      sem.at[c % 2]).start()
            return carry
        lax.fori_loop(0, ch // unroll, body, 0)

    @pl.when(pair == 0)
    def _():
        issue(pair, 0)

    for c in range(n_c):
        if c + 1 < n_c:
            issue(pair, c + 1)
        else:
            @pl.when(pair + 1 < n_pairs)
            def _():
                issue(pair + 1, 0)
        buf = bufs[c % 2]
        pltpu.make_async_copy(h2_hbm.at[pl.ds(0, ch * rpt), :], buf, sem.at[c % 2]).wait()
        lo, hi = [], []
        for s in range(rpt):
            w = buf[pl.ds(s, ch, stride=rpt), :]
            lo.append(pltpu.bitcast(w << 16, F32).astype(BF16))
            hi.append(pltpu.bitcast(w & jnp.uint32(0xFFFF0000), F32).astype(BF16))
        x = jnp.concatenate(lo + hi, axis=1)
        a = jnp.dot(x, wg_ref[...], preferred_element_type=F32)
        u = jnp.dot(x, wu_ref[...], preferred_element_type=F32)
        act = (jax.nn.silu(a) * u).astype(BF16)
        rows = slice(c * ch, (c + 1) * ch)
        y = jnp.dot(act, wd_ref[...], preferred_element_type=F32) * gate_ref[rows, :]
        y_ref[rows, :] = y.astype(y_ref.dtype)


def _ffn(idx_flat, h2p, gates4, wg, wu, wd, bsz, s_len, cap, ch):
    n_e, d, f = wg.shape
    rpt = h2p.shape[0] // (bsz * s_len)
    assert (cap // ch) % 2 == 0
    kern = functools.partial(_ffn_kernel, bsz=bsz, s_len=s_len, cap=cap, ch=ch, rpt=rpt, unroll=8)
    grid_spec = pltpu.PrefetchScalarGridSpec(
        num_scalar_prefetch=1,
        grid=(n_e, bsz),
        in_specs=[pl.BlockSpec(memory_space=pl.ANY),
                  pl.BlockSpec((None, None, cap, 1), lambda e, b, idx: (b, e, 0, 0)),
                  pl.BlockSpec((None, d, f), lambda e, b, idx: (e, 0, 0)),
                  pl.BlockSpec((None, d, f), lambda e, b, idx: (e, 0, 0)),
                  pl.BlockSpec((None, f, d), lambda e, b, idx: (e, 0, 0))],
        out_specs=pl.BlockSpec((None, None, cap, d), lambda e, b, idx: (b, e, 0, 0)),
        scratch_shapes=[pltpu.VMEM((ch * rpt, LANES), U32),
                        pltpu.VMEM((ch * rpt, LANES), U32),
                        pltpu.SemaphoreType.DMA((2,))],
    )
    return pl.pallas_call(
        kern,
        grid_spec=grid_spec,
        out_shape=jax.ShapeDtypeStruct((bsz, n_e, cap, d), BF16),
        compiler_params=_params(2),
        name="ffn",
    )(idx_flat, h2p, gates4, wg, wu, wd)


def _combine_kernel(off_ref, y_hbm, x1_ref, cumx_ref, sel_ref, gf_ref, o_ref, ybuf, yextra, sem, *,
                    cap, win, n_r, final):
    b = pl.program_id(0)
    r = pl.program_id(1)
    n = b * n_r + r
    n_tiles = pl.num_programs(0) * n_r
    slot = n % 2
    rows_all = N_EXPERTS * win

    def windows(bb, rr, k):
        obase = (bb * (n_r + 1) + rr) * N_EXPERTS
        out = []
        for e in range(N_EXPERTS):
            first = (off_ref[obase + e] // BF16_ROWS) * BF16_ROWS + k * win
            out.append((first, jnp.minimum(first, cap - win)))
        return out

    def start_copies(bb, wins, dst_ref, sem_k):
        for e, (_, begin) in enumerate(wins):
            src = pl.multiple_of((bb * N_EXPERTS + e) * cap + begin, BF16_ROWS)
            pltpu.make_async_copy(y_hbm.at[pl.ds(src, win), :], dst_ref.at[pl.ds(e * win, win), :],
                                  sem_k).start()

    def wait_copies(dst_ref, sem_k):
        pltpu.make_async_copy(y_hbm.at[pl.ds(0, rows_all), :], dst_ref, sem_k).wait()

    jcol = lax.broadcasted_iota(I32, (win, 1), 0).astype(F32)

    def expand(wins, rows_bf16):
        blocks = []
        for e, (first, begin) in enumerate(wins):
            rank = cumx_ref[e:e + 1, :]
            blocks.append(jnp.where(rank == jcol + begin.astype(F32),
                                    jnp.where(rank >= first.astype(F32), sel_ref[e:e + 1, :], 0.0), 0.0))
        onehot = jnp.concatenate(blocks, axis=0).astype(BF16)
        return lax.dot_general(onehot, rows_bf16, TN, preferred_element_type=F32)

    @pl.when(n == 0)
    def _():
        start_copies(b, windows(b, r, 0), ybuf.at[slot], sem.at[slot])

    @pl.when(n + 1 < n_tiles)
    def _():
        nb = (n + 1) // n_r
        start_copies(nb, windows(nb, (n + 1) % n_r, 0), ybuf.at[1 - slot], sem.at[1 - slot])

    obase = (b * (n_r + 1) + r) * N_EXPERTS
    rounds = jnp.int32(0)
    for e in range(N_EXPERTS):
        off = off_ref[obase + e]
        cnt = off_ref[obase + N_EXPERTS + e] - off
        used = off - (off // BF16_ROWS) * BF16_ROWS + cnt
        rounds = jnp.maximum(rounds, jnp.where(cnt > 0, (used + win - 1) // win, 0))

    wait_copies(ybuf.at[slot], sem.at[slot])
    o_ref[...] = x1_ref[...] + expand(windows(b, r, 0), ybuf[slot])

    def extra_round(k, carry):
        wins = windows(b, r, k)
        start_copies(b, wins, yextra, sem.at[2])
        wait_copies(yextra, sem.at[2])
        o_ref[...] += expand(wins, yextra[...])
        return carry
    lax.fori_loop(1, rounds, extra_round, 0)

    if final:
        x2 = o_ref[...]
        ms = jnp.mean(x2 * x2, axis=-1, keepdims=True)
        o_ref[...] = x2 * lax.rsqrt(ms + EPS) * gf_ref[...]


def _combine(off_flat, y2d, x1, cumx_t, sel_t, g_final, bsz, s_len, cap, win, final):
    t, d = x1.shape
    n_r = s_len // LANES
    kern = functools.partial(_combine_kernel, cap=cap, win=win, n_r=n_r, final=final)
    meta = pl.BlockSpec((None, None, N_EXPERTS, LANES), lambda b, r, off: (b, r, 0, 0))
    grid_spec = pltpu.PrefetchScalarGridSpec(
        num_scalar_prefetch=1,
        grid=(bsz, n_r),
        in_specs=[pl.BlockSpec(memory_space=pl.ANY),
                  pl.BlockSpec((LANES, d), lambda b, r, off: (b * n_r + r, 0)),
                  meta, meta,
                  pl.BlockSpec((1, d), lambda b, r, off: (0, 0))],
        out_specs=pl.BlockSpec((LANES, d), lambda b, r, off: (b * n_r + r, 0)),
        scratch_shapes=[pltpu.VMEM((2, N_EXPERTS * win, d), BF16),
                        pltpu.VMEM((N_EXPERTS * win, d), BF16),
                        pltpu.SemaphoreType.DMA((3,))],
    )
    return pl.pallas_call(
        kern,
        grid_spec=grid_spec,
        out_shape=jax.ShapeDtypeStruct((t, d), F32),
        compiler_params=_params(2),
        name="combine",
    )(off_flat, y2d, x1, cumx_t, sel_t, g_final)


def _rope_tables(s_len):
    rows = s_len // GRID_W
    r = jnp.repeat(jnp.arange(rows, dtype=F32), GRID_W)
    c = jnp.tile(jnp.arange(GRID_W, dtype=F32), rows)
    half = HEAD_DIM // 2
    inv = 1.0 / (ROPE_THETA ** (jnp.arange(0, half, 2, dtype=F32) / half))
    ang_r = r[:, None] * inv
    ang_c = c[:, None] * inv
    cos = jnp.concatenate([jnp.cos(ang_r), jnp.cos(ang_r), jnp.cos(ang_c), jnp.cos(ang_c)], axis=-1)
    sin = jnp.concatenate([-jnp.sin(ang_r), jnp.sin(ang_r), -jnp.sin(ang_c), jnp.sin(ang_c)], axis=-1)
    return cos, sin


def _t5_bucket_table():
    rel = (jnp.arange(3 * BLOCK) - BLOCK)[None, :] - jnp.arange(BLOCK)[:, None]
    half = NUM_BUCKETS // 2
    ret = jnp.where(rel > 0, half, 0)
    n = jnp.abs(rel)
    max_exact = half // 2
    nf = jnp.maximum(n, 1).astype(F32)
    large = max_exact + (jnp.log(nf / max_exact) / math.log(MAX_DISTANCE / max_exact)
                         * (half - max_exact)).astype(I32)
    large = jnp.minimum(large, half - 1)
    return (ret + jnp.where(n < max_exact, n, large)).astype(I32)


def kernel(x, g_mix, w_in, b_gate, qn_a, kn_a, w_proj_a, sink_b, rel_bias, w_proj_b, w_o, g_ffn, w_router,
           w_gate_e, w_up_e, w_down_e, g_final):
    bsz, s_len, d = x.shape
    depth = g_mix.shape[0]
    t = bsz * s_len
    cap = CAPACITY_FACTOR * s_len // N_EXPERTS
    n_r = s_len // LANES
    cos, sin = _rope_tables(s_len)
    bucket = _t5_bucket_table()
    x2d = x.reshape(t, d)
    for l in range(depth):
        proj = _inproj(x2d, g_mix[l][None, :], w_in[l].astype(BF16), tm=min(1024, t), tn=1792)
        oa = _attn_a(proj, cos, sin, qn_a[l], kn_a[l], bsz, s_len, d,
                     tq=min(256, s_len), tk=min(512, s_len))
        ob = _attn_b(proj, bucket, rel_bias, sink_b[l], bsz, s_len, d, tq=min(512, s_len))
        x1, h2p, aff = _outproj(oa, ob, proj, x2d, b_gate[l][None, :], w_proj_a[l].astype(BF16),
                                w_proj_b[l].astype(BF16), w_o[l].astype(BF16), g_ffn[l][None, :],
                                _split_router(w_router[l]), bsz, s_len, tm=256)
        idx, gates, cumx, sel = _topk(aff.reshape(bsz, N_EXPERTS, n_r, LANES), cap)
        y = _ffn(idx.reshape(-1), h2p, gates[..., None], w_gate_e[l].astype(BF16), w_up_e[l].astype(BF16),
                 w_down_e[l].astype(BF16), bsz, s_len, cap, ch=min(512, cap // 2))
        off = jnp.concatenate([cumx[:, :, :, 0].astype(I32), jnp.full((bsz, N_EXPERTS, 1), cap, I32)], axis=2)
        off_flat = off.transpose(0, 2, 1).reshape(-1)
        x2d = _combine(off_flat, y.reshape(bsz * N_EXPERTS * cap, d), x1, cumx.transpose(0, 2, 1, 3),
                       sel.transpose(0, 2, 1, 3), g_final[None, :], bsz, s_len, cap, win=48,
                       final=(l == depth - 1))
    return x2d.reshape(bsz, s_len, d)
```

```python
import functools
import math

import jax
import jax.numpy as jnp
from jax import lax
from jax.experimental import pallas as pl
from jax.experimental.pallas import tpu as pltpu

F32 = jnp.float32
BF16 = jnp.bfloat16
I32 = jnp.int32
U32 = jnp.uint32

HEAD_DIM = 128
GROUP = 4
BLOCK = 128
GRID_W = 64
ROPE_THETA = 10000.0
NUM_BUCKETS = 32
MAX_DISTANCE = 128
N_EXPERTS = 16
CAPACITY_FACTOR = 2
EPS = 1e-6
NEG = -1e30
LANES = 128
SUBLANES = 8
BF16_ROWS = 16
ONES_ROWS = BF16_ROWS
SAFE_EXP2_RANGE = 100.0
ROW_CHAINS = 2
KV_UNROLL = 8
VMEM_LIMIT = 56 * 1024 * 1024
HIGHEST = lax.Precision.HIGHEST
NT = (((1,), (1,)), ((), ()))
TN = (((0,), (0,)), ((), ()))


def _params(n_axes):
    return pltpu.CompilerParams(dimension_semantics=("arbitrary",) * n_axes,
                                vmem_limit_bytes=VMEM_LIMIT)


def _inproj_kernel(x_ref, g_ref, w_ref, o_ref, h_ref):
    @pl.when(pl.program_id(1) == 0)
    def _():
        x = x_ref[...]
        ms = jnp.mean(x * x, axis=-1, keepdims=True)
        h_ref[...] = (x * lax.rsqrt(ms + EPS) * g_ref[...]).astype(BF16)

    o_ref[...] = jnp.dot(h_ref[...], w_ref[...], preferred_element_type=F32).astype(o_ref.dtype)


def _inproj(x2d, g, w, tm, tn):
    t, d = x2d.shape
    n = w.shape[1]
    return pl.pallas_call(
        _inproj_kernel,
        grid=(t // tm, n // tn),
        in_specs=[pl.BlockSpec((tm, d), lambda i, j: (i, 0)),
                  pl.BlockSpec((1, d), lambda i, j: (0, 0)),
                  pl.BlockSpec((d, tn), lambda i, j: (0, j))],
        out_specs=pl.BlockSpec((tm, tn), lambda i, j: (i, j)),
        out_shape=jax.ShapeDtypeStruct((t, n), BF16),
        scratch_shapes=[pltpu.VMEM((tm, d), BF16)],
        compiler_params=_params(2),
        name="inproj",
    )(x2d, g, w)


def _norm_rope(xb, gain, gain_sw, cos, sin_signed):
    a = lax.broadcasted_iota(I32, (HEAD_DIM, HEAD_DIM), 0)
    c = lax.broadcasted_iota(I32, (HEAD_DIM, HEAD_DIM), 1)
    swap_mat = jnp.where(a == (c ^ 32), 1.0, 0.0).astype(BF16)
    ones_mat = jnp.ones((HEAD_DIM, HEAD_DIM), BF16)
    x = xb.astype(F32)
    sq = x * x
    sq_hi = sq.astype(BF16)
    sq_lo = (sq - sq_hi.astype(F32)).astype(BF16)
    ms = (jnp.dot(sq_hi, ones_mat, preferred_element_type=F32)
          + jnp.dot(sq_lo, ones_mat, preferred_element_type=F32)) * (1.0 / HEAD_DIM)
    swapped = jnp.dot(xb, swap_mat, preferred_element_type=F32)
    return lax.rsqrt(ms + EPS) * (x * (gain * cos) + swapped * (gain_sw * sin_signed))


def _attn_a_kernel(q_ref, k_ref, v_ref, cq_ref, sq_ref, ck_ref, sk_ref, qn_ref, kn_ref, o_ref,
                   kp_ref, vt_ref, qt_ref, ksq_ref, m_ref, acc_ref, st0_ref, st1_ref, pt0_ref, pt1_ref,
                   *, tq, tk):
    s_len = k_ref.shape[0]
    n_chunks = s_len // tk

    @pl.when(pl.program_id(2) == 0)
    def _():
        ksq_ref[...] = jnp.zeros(ksq_ref.shape, F32)

        def body(c, carry):
            off = pl.multiple_of(c * tk, tk)
            kb = _norm_rope(k_ref[pl.ds(off, tk), :], kn_ref[0:1, :], kn_ref[1:2, :],
                            ck_ref[pl.ds(off, tk), :], sk_ref[pl.ds(off, tk), :]).astype(BF16)
            kp_ref[pl.ds(off, tk), :] = kb
            kf = kb.astype(F32)
            ksq_ref[...] = jnp.maximum(ksq_ref[...], jnp.max(jnp.sum(kf * kf, axis=1, keepdims=True)))
            vt_ref[c, :HEAD_DIM, :] = v_ref[pl.ds(off, tk), :].astype(F32).T.astype(BF16)
            vt_ref[c, HEAD_DIM:, :] = jnp.ones((ONES_ROWS, tk), BF16)
            return carry
        lax.fori_loop(0, n_chunks, body, 0)

    scale = math.log2(math.e) / math.sqrt(HEAD_DIM)
    qsq = jnp.zeros((1, tq), F32)
    for g in range(GROUP):
        qg = _norm_rope(q_ref[:, g * HEAD_DIM:(g + 1) * HEAD_DIM], qn_ref[0:1, :], qn_ref[1:2, :],
                        cq_ref[...], sq_ref[...]) * scale
        qb = qg.T.astype(BF16)
        qt_ref[g] = qb
        qf = qb.astype(F32)
        qsq = jnp.maximum(qsq, jnp.sum(qf * qf, axis=0, keepdims=True))
    bound_sq = jnp.max(qsq) * jnp.max(ksq_ref[...])
    no_max_needed = bound_sq <= SAFE_EXP2_RANGE * SAFE_EXP2_RANGE

    acc_ref[...] = jnp.zeros(acc_ref.shape, F32)
    unroll = math.gcd(KV_UNROLL, n_chunks)

    @pl.when(no_max_needed)
    def _():
        def probs(c, dst_ref):
            kc = kp_ref[pl.ds(pl.multiple_of(c * tk, tk), tk), :]
            for g in range(GROUP):
                dst_ref[g] = jnp.exp2(jnp.dot(kc, qt_ref[g], preferred_element_type=F32)).astype(BF16)

        def weighted_sum(c, src_ref):
            vt = vt_ref[c]
            for g in range(GROUP):
                acc_ref[g] += jnp.dot(vt, src_ref[g], preferred_element_type=F32)

        probs(0, pt0_ref)

        def kv_body(j, carry):
            bufs = (pt0_ref, pt1_ref)
            for u in range(unroll):
                c = unroll * j + u
                probs(jnp.minimum(c + 1, n_chunks - 1), bufs[(u + 1) % 2])
                weighted_sum(c, bufs[u % 2])
            return carry
        lax.fori_loop(0, n_chunks // unroll, kv_body, 0)

    @pl.when(jnp.logical_not(no_max_needed))
    def _():
        m_ref[...] = jnp.full(m_ref.shape, -jnp.inf, F32)

        def scores(c, dst_ref):
            kc = kp_ref[pl.ds(pl.multiple_of(c * tk, tk), tk), :]
            for g in range(GROUP):
                dst_ref[g] = jnp.dot(kc, qt_ref[g], preferred_element_type=F32)

        def softmax_pv(c, src_ref):
            vt = vt_ref[c]
            for g in range(GROUP):
                st = src_ref[g]
                m_prev = m_ref[g]
                m_new = jnp.maximum(m_prev, jnp.max(st, axis=0, keepdims=True))
                pt = jnp.exp2(st - m_new).astype(BF16)
                acc_ref[g] = (jnp.exp2(m_prev - m_new) * acc_ref[g]
                              + jnp.dot(vt, pt, preferred_element_type=F32))
                m_ref[g] = m_new

        scores(0, st0_ref)

        def kv_body(j, carry):
            c = 2 * j
            scores(c + 1, st1_ref)
            softmax_pv(c, st0_ref)
            scores(jnp.minimum(c + 2, n_chunks - 1), st0_ref)
            softmax_pv(c + 1, st1_ref)
            return carry
        lax.fori_loop(0, n_chunks // 2, kv_body, 0)

    for g in range(GROUP):
        acc = acc_ref[g]
        o = (acc[:HEAD_DIM] / acc[HEAD_DIM:HEAD_DIM + 1]).T
        o_ref[:, g * HEAD_DIM:(g + 1) * HEAD_DIM] = o.astype(o_ref.dtype)


def _attn_a(proj, cos, sin, qn, kn, bsz, s_len, d, tq, tk):
    q_w = d // 2
    kv_w = d // 8
    kvh = kv_w // HEAD_DIM
    gw = GROUP * HEAD_DIM
    nq = s_len // tq
    k_blk = q_w // HEAD_DIM
    v_blk = (q_w + kv_w) // HEAD_DIM
    assert (s_len // tk) % 2 == 0

    def with_swapped(gain):
        return jnp.stack([gain, gain.reshape(2, 2, 32)[:, ::-1, :].reshape(HEAD_DIM)])

    qn, kn = with_swapped(qn), with_swapped(kn)
    kern = functools.partial(_attn_a_kernel, tq=tq, tk=tk)
    return pl.pallas_call(
        kern,
        grid=(bsz, kvh, nq),
        in_specs=[pl.BlockSpec((tq, gw), lambda b, k, i: (b * nq + i, k)),
                  pl.BlockSpec((s_len, HEAD_DIM), lambda b, k, i: (b, k_blk + k)),
                  pl.BlockSpec((s_len, HEAD_DIM), lambda b, k, i: (b, v_blk + k)),
                  pl.BlockSpec((tq, HEAD_DIM), lambda b, k, i: (i, 0)),
                  pl.BlockSpec((tq, HEAD_DIM), lambda b, k, i: (i, 0)),
                  pl.BlockSpec((s_len, HEAD_DIM), lambda b, k, i: (0, 0)),
                  pl.BlockSpec((s_len, HEAD_DIM), lambda b, k, i: (0, 0)),
                  pl.BlockSpec((2, HEAD_DIM), lambda b, k, i: (0, 0)),
                  pl.BlockSpec((2, HEAD_DIM), lambda b, k, i: (0, 0))],
        out_specs=pl.BlockSpec((tq, gw), lambda b, k, i: (b * nq + i, k)),
        out_shape=jax.ShapeDtypeStruct((bsz * s_len, q_w), BF16),
        scratch_shapes=[pltpu.VMEM((s_len, HEAD_DIM), BF16),
                        pltpu.VMEM((s_len // tk, HEAD_DIM + ONES_ROWS, tk), BF16),
                        pltpu.VMEM((GROUP, HEAD_DIM, tq), BF16),
                        pltpu.VMEM((1, LANES), F32),
                        pltpu.VMEM((GROUP, 1, tq), F32),
                        pltpu.VMEM((GROUP, HEAD_DIM + ONES_ROWS, tq), F32),
                        pltpu.VMEM((GROUP, tk, tq), F32),
                        pltpu.VMEM((GROUP, tk, tq), F32),
                        pltpu.VMEM((GROUP, tk, tq), BF16),
                        pltpu.VMEM((GROUP, tk, tq), BF16)],
        compiler_params=_params(3),
        name="attn_a",
    )(proj, proj, proj, cos, sin, cos, sin, qn, kn)


def _attn_b_kernel(relb_ref, sink_ref, bucket_ref, q_ref, kp_ref, kc_ref, kn_ref, vp_ref, vc_ref, vn_ref,
                   o_ref, bias_ref, *, tq, s_len, kvh):
    b = pl.program_id(0)
    k = pl.program_id(1)
    i = pl.program_id(2)
    band = 3 * BLOCK
    log2e = math.log2(math.e)

    @pl.when((b == 0) & (k == 0) & (i == 0))
    def _():
        bucket_t = bucket_ref[...]
        key = lax.broadcasted_iota(I32, (band, BLOCK), 0)
        qry = lax.broadcasted_iota(I32, (band, BLOCK), 1)
        in_window = jnp.abs(key - BLOCK - qry) <= BLOCK
        for kk in range(kvh):
            for g in range(GROUP):
                h = kk * GROUP + g
                tab = jnp.zeros((band, BLOCK), F32)
                for bkt in range(NUM_BUCKETS):
                    tab = jnp.where(bucket_t == bkt, relb_ref[bkt, h], tab)
                tab = jnp.where(in_window, tab * log2e, NEG)
                cols = slice(g * BLOCK, (g + 1) * BLOCK)
                bias_ref[0, kk, :, cols] = tab
                bias_ref[1, kk, :, cols] = jnp.where(key >= BLOCK, tab, NEG)
                bias_ref[2, kk, :, cols] = jnp.where(key < 2 * BLOCK, tab, NEG)

    nsub = tq // BLOCK
    kcat = jnp.concatenate([kp_ref[...], kc_ref[...], kn_ref[...]], axis=0)
    vcat = jnp.concatenate([vp_ref[...], vc_ref[...], vn_ref[...]], axis=0)
    vt = jnp.concatenate([vcat.astype(F32).T.astype(BF16), jnp.ones((ONES_ROWS, vcat.shape[0]), BF16)], axis=0)
    sink = jnp.concatenate([jnp.full((1, BLOCK), sink_ref[k * GROUP + g], F32) for g in range(GROUP)],
                           axis=1) * log2e
    scale = log2e / math.sqrt(HEAD_DIM)
    for jb in range(nsub):
        rows = slice(jb * BLOCK, (jb + 1) * BLOCK)
        qt = jnp.concatenate([q_ref[rows, g * HEAD_DIM:(g + 1) * HEAD_DIM].astype(F32).T.astype(BF16)
                              for g in range(GROUP)], axis=1)
        if jb == 0:
            variant = jnp.where(i == 0, 1, 0)
        elif jb == nsub - 1:
            variant = jnp.where(i == pl.num_programs(2) - 1, 2, 0)
        else:
            variant = 0
        st = (jnp.dot(kcat[jb * BLOCK:jb * BLOCK + band], qt, preferred_element_type=F32) * scale
              + bias_ref[variant, k])
        m = jnp.maximum(jnp.max(st, axis=0, keepdims=True), sink)
        e = jnp.exp2(st - m).astype(BF16)
        acc = jnp.dot(vt[:, jb * BLOCK:jb * BLOCK + band], e, preferred_element_type=F32)
        o = acc[:HEAD_DIM] / (acc[HEAD_DIM:HEAD_DIM + 1] + jnp.exp2(sink - m))
        for g in range(GROUP):
            o_ref[rows, g * HEAD_DIM:(g + 1) * HEAD_DIM] = o[:, g * BLOCK:(g + 1) * BLOCK].T.astype(o_ref.dtype)


def _attn_b(proj, bucket, rel_bias, sink, bsz, s_len, d, tq):
    q_w = d // 2
    kv_w = d // 8
    kvh = kv_w // HEAD_DIM
    gw = GROUP * HEAD_DIM
    nq = s_len // tq
    sub = tq // BLOCK
    nblk = s_len // BLOCK
    q_blk = (q_w + 2 * kv_w) // gw
    k_blk = (2 * q_w + 2 * kv_w) // HEAD_DIM
    v_blk = (2 * q_w + 3 * kv_w) // HEAD_DIM

    def prev_map(col):
        return lambda b, k, i: (b * nblk + jnp.maximum(i * sub - 1, 0), col + k)

    def cur_map(col):
        return lambda b, k, i: (b * nq + i, col + k)

    def next_map(col):
        return lambda b, k, i: (b * nblk + jnp.minimum((i + 1) * sub, nblk - 1), col + k)

    small = (BLOCK, HEAD_DIM)
    assert sub >= 2
    kern = functools.partial(_attn_b_kernel, tq=tq, s_len=s_len, kvh=kvh)
    return pl.pallas_call(
        kern,
        grid=(bsz, kvh, nq),
        in_specs=[pl.BlockSpec(memory_space=pltpu.SMEM),
                  pl.BlockSpec(memory_space=pltpu.SMEM),
                  pl.BlockSpec((3 * BLOCK, BLOCK), lambda b, k, i: (0, 0)),
                  pl.BlockSpec((tq, gw), lambda b, k, i: (b * nq + i, q_blk + k)),
                  pl.BlockSpec(small, prev_map(k_blk)),
                  pl.BlockSpec((tq, HEAD_DIM), cur_map(k_blk)),
                  pl.BlockSpec(small, next_map(k_blk)),
                  pl.BlockSpec(small, prev_map(v_blk)),
                  pl.BlockSpec((tq, HEAD_DIM), cur_map(v_blk)),
                  pl.BlockSpec(small, next_map(v_blk))],
        out_specs=pl.BlockSpec((tq, gw), lambda b, k, i: (b * nq + i, k)),
        out_shape=jax.ShapeDtypeStruct((bsz * s_len, q_w), BF16),
        scratch_shapes=[pltpu.VMEM((3, kvh, 3 * BLOCK, GROUP * BLOCK), F32)],
        compiler_params=_params(3),
        name="attn_b",
    )(rel_bias, sink, bucket.T, proj, proj, proj, proj, proj, proj, proj)


def _outproj_kernel(oa_ref, ob_ref, ga0_ref, ga1_ref, gb0_ref, gb1_ref, x_ref, bg_ref, wpa_ref, wpb_ref,
                    wo_ref, gf_ref, wr_ref, x1_ref, h2p_ref, aff_ref):
    tm, d = x_ref.shape
    half = d // 2
    rpt = half // LANES
    rn = tm // ROW_CHAINS
    for rc in range(ROW_CHAINS):
        rows = slice(rc * rn, (rc + 1) * rn)
        oa = oa_ref[rows, :]
        ob = ob_ref[rows, :]
        parts = []
        for c, (ga_ref, gb_ref) in enumerate(((ga0_ref, gb0_ref), (ga1_ref, gb1_ref))):
            cols = slice(c * half, (c + 1) * half)
            pa = jnp.dot(oa, wpa_ref[:, cols], preferred_element_type=F32)
            pb = jnp.dot(ob, wpb_ref[:, cols], preferred_element_type=F32)
            gate_a = jax.nn.sigmoid(ga_ref[rows, :].astype(F32) + bg_ref[:, cols])
            gate_b = jax.nn.sigmoid(gb_ref[rows, :].astype(F32) + bg_ref[:, d + c * half:d + (c + 1) * half])
            parts.append((gate_a * pa + gate_b * pb).astype(BF16))
        x1 = x_ref[rows, :] + (jnp.dot(parts[0], wo_ref[:half, :], preferred_element_type=F32)
                               + jnp.dot(parts[1], wo_ref[half:, :], preferred_element_type=F32))
        x1_ref[rows, :] = x1
        ms = jnp.mean(x1 * x1, axis=-1, keepdims=True)
        h = x1 * lax.rsqrt(ms + EPS) * gf_ref[...]
        h_hi = h.astype(BF16)
        bits = pltpu.bitcast(h_hi.astype(F32), U32)
        packed = (bits[:, :half] >> 16) | (bits[:, half:] & jnp.uint32(0xFFFF0000))
        for s in range(rpt):
            h2p_ref[pl.ds(rc * rn * rpt + s, rn, stride=rpt), :] = packed[:, s * LANES:(s + 1) * LANES]
        h_lo = (h - h_hi.astype(F32)).astype(BF16)
        prod = (jnp.dot(h_hi, wr_ref[...], preferred_element_type=F32)
                + jnp.dot(h_lo, wr_ref[...], preferred_element_type=F32))
        logits = (prod[:, :LANES] + prod[:, LANES:]).T[:N_EXPERTS, :]
        ex = jnp.exp(logits - jnp.max(logits, axis=0, keepdims=True))
        aff_ref[:, rows] = ex / jnp.sum(ex, axis=0, keepdims=True)


def _split_router(w):
    w_hi = w.astype(BF16)
    w_lo = (w - w_hi.astype(F32)).astype(BF16)
    pad = ((0, 0), (0, LANES - w.shape[1]))
    return jnp.concatenate([jnp.pad(w_hi, pad), jnp.pad(w_lo, pad)], axis=1)


def _outproj(oa, ob, proj, x2d, b_gate, wpa, wpb, wo, g_ffn, wr_split, bsz, s_len, tm):
    t, d = x2d.shape
    half = d // 2
    nt = s_len // tm
    row = lambda c: (lambda i: (i, c))
    const = lambda i: (0, 0)
    return pl.pallas_call(
        _outproj_kernel,
        grid=(t // tm,),
        in_specs=[pl.BlockSpec((tm, half), row(0)),
                  pl.BlockSpec((tm, half), row(0)),
                  pl.BlockSpec((tm, half), row(3)),
                  pl.BlockSpec((tm, half), row(4)),
                  pl.BlockSpec((tm, half), row(5)),
                  pl.BlockSpec((tm, half), row(6)),
                  pl.BlockSpec((tm, d), row(0)),
                  pl.BlockSpec((1, 2 * d), const),
                  pl.BlockSpec((half, d), const, pipeline_mode=pl.Buffered(1)),
                  pl.BlockSpec((half, d), const, pipeline_mode=pl.Buffered(1)),
                  pl.BlockSpec((d, d), const, pipeline_mode=pl.Buffered(1)),
                  pl.BlockSpec((1, d), const),
                  pl.BlockSpec((d, 2 * LANES), const, pipeline_mode=pl.Buffered(1))],
        out_specs=[pl.BlockSpec((tm, d), row(0)),
                   pl.BlockSpec((tm * (half // LANES), LANES), row(0)),
                   pl.BlockSpec((None, N_EXPERTS, tm), lambda i: (i // nt, 0, i % nt))],
        out_shape=[jax.ShapeDtypeStruct((t, d), F32),
                   jax.ShapeDtypeStruct((t * (half // LANES), LANES), U32),
                   jax.ShapeDtypeStruct((bsz, N_EXPERTS, s_len), F32)],
        compiler_params=_params(1),
        name="outproj",
    )(oa, ob, proj, proj, proj, proj, x2d, b_gate, wpa, wpb, wo, g_ffn, wr_split)


def _topk_kernel(a_ref, idx_ref, gate_ref, cumx_ref, sel_ref, *, cap):
    a = a_ref[...]
    n_e, n_r, _ = a.shape
    bits = pltpu.bitcast(a, I32)

    def total(x):
        return jnp.sum(jnp.sum(x, axis=1, keepdims=True), axis=2, keepdims=True)

    def search(_, carry):
        lo, hi = carry
        mid = lo + ((hi - lo) >> 1)
        enough = total(jnp.where(bits >= mid, 1.0, 0.0)) >= cap
        return jnp.where(enough, mid, lo), jnp.where(enough, hi, mid)

    lo0 = jnp.zeros((n_e, 1, 1), I32)
    hi0 = jnp.full((n_e, 1, 1), 0x7F800000, I32)
    thr, _ = lax.fori_loop(0, 31, search, (lo0, hi0))

    ri = lax.broadcasted_iota(I32, (LANES, LANES), 0)
    ci = lax.broadcasted_iota(I32, (LANES, LANES), 1)
    upper = jnp.where(ri <= ci, 1.0, 0.0).astype(BF16)
    ones = jnp.ones((LANES, LANES), BF16)
    rr = lax.broadcasted_iota(I32, (n_r, n_r), 0)
    rc = lax.broadcasted_iota(I32, (n_r, n_r), 1)
    strict_lower = jnp.where(rc < rr, 1.0, 0.0).astype(BF16)

    def prefix(x):
        x2 = x.reshape(n_e * n_r, LANES).astype(BF16)
        in_row = jnp.dot(x2, upper, preferred_element_type=F32).reshape(n_e, n_r, LANES)
        row_tot = jnp.dot(x2, ones, preferred_element_type=F32).reshape(n_e, n_r, LANES)
        before = jnp.stack([jnp.dot(strict_lower, row_tot[e].astype(BF16), preferred_element_type=F32)
                            for e in range(n_e)], axis=0)
        return in_row + before

    above = jnp.where(bits > thr, 1.0, 0.0)
    tied = jnp.where(bits == thr, 1.0, 0.0)
    need = cap - total(above)
    sel = above + tied * jnp.where(prefix(tied) <= need, 1.0, 0.0)
    cum = prefix(sel)
    sel_ref[...] = sel
    cumx_ref[...] = cum - sel

    slot = lax.broadcasted_iota(I32, (1, cap), 1).astype(F32)
    row_id = lax.broadcasted_iota(I32, (n_r, 1), 0).astype(F32)
    lane_id = lax.broadcasted_iota(I32, (LANES, 1), 0).astype(F32)
    for e in range(n_e):
        c = cum[e]
        row_end = c[:, LANES - 1:LANES]
        srow = jnp.sum(jnp.where(row_end <= slot, 1.0, 0.0), axis=0, keepdims=True)
        pick = jnp.where(row_id == srow, 1.0, 0.0)
        c_row = lax.dot_general(c, pick, TN, precision=HIGHEST, preferred_element_type=F32)
        slane = jnp.sum(jnp.where(c_row <= slot, 1.0, 0.0), axis=0, keepdims=True)
        a_row = lax.dot_general(a[e], pick, TN, precision=HIGHEST, preferred_element_type=F32)
        idx_ref[e:e + 1, :] = (srow * LANES + slane).astype(I32)
        gate_ref[e:e + 1, :] = jnp.sum(jnp.where(lane_id == slane, a_row, 0.0), axis=0, keepdims=True)


def _topk(aff4, cap):
    bsz, n_e, n_r, _ = aff4.shape
    blk4 = pl.BlockSpec((None, n_e, n_r, LANES), lambda b: (b, 0, 0, 0))
    blk3 = pl.BlockSpec((None, n_e, cap), lambda b: (b, 0, 0))
    return pl.pallas_call(
        functools.partial(_topk_kernel, cap=cap),
        grid=(bsz,),
        in_specs=[blk4],
        out_specs=[blk3, blk3, blk4, blk4],
        out_shape=[jax.ShapeDtypeStruct((bsz, n_e, cap), I32),
                   jax.ShapeDtypeStruct((bsz, n_e, cap), F32),
                   jax.ShapeDtypeStruct(aff4.shape, F32),
                   jax.ShapeDtypeStruct(aff4.shape, F32)],
        compiler_params=_params(1),
        name="topk",
    )(aff4)


def _ffn_kernel(idx_ref, h2_hbm, gate_ref, wg_ref, wu_ref, wd_ref, y_ref, xa, xb, sem, *,
                bsz, s_len, cap, ch, rpt, unroll):
    n_c = cap // ch
    bufs = (xa, xb)
    pair = pl.program_id(0) * bsz + pl.program_id(1)
    n_pairs = pl.num_programs(0) * bsz

    def issue(pair_k, c):
        bb = pair_k % bsz
        ee = pair_k // bsz
        idx_base = (bb * N_EXPERTS + ee) * cap + c * ch
        row_base = bb * s_len
        buf = bufs[c % 2]

        def body(j, carry):
            for u in range(unroll):
                r = j * unroll + u
                tok = idx_ref[idx_base + r]
                src = pl.multiple_of((row_base + tok) * rpt, rpt)
                dst = pl.multiple_of(r * rpt, rpt)
                pltpu.make_async_copy(h2_hbm.at[pl.ds(src, rpt), :], buf.at[pl.ds(dst, rpt), :],
                                      sem.at[c % 2]).start()
            return carry
        lax.fori_loop(0, ch // unroll, body, 0)

    @pl.when(pair == 0)
    def _():
        issue(pair, 0)

    for c in range(n_c):
        if c + 1 < n_c:
            issue(pair, c + 1)
        else:
            @pl.when(pair + 1 < n_pairs)
            def _():
                issue(pair + 1, 0)
        buf = bufs[c % 2]
        pltpu.make_async_copy(h2_hbm.at[pl.ds(0, ch * rpt), :], buf, sem.at[c % 2]).wait()
        lo, hi = [], []
        for s in range(rpt):
            w = buf[pl.ds(s, ch, stride=rpt), :]
            lo.append(pltpu.bitcast(w << 16, F32).astype(BF16))
            hi.append(pltpu.bitcast(w & jnp.uint32(0xFFFF0000), F32).astype(BF16))
        x = jnp.concatenate(lo + hi, axis=1)
        a = jnp.dot(x, wg_ref[...], preferred_element_type=F32)
        u = jnp.dot(x, wu_ref[...], preferred_element_type=F32)
        act = (jax.nn.silu(a) * u).astype(BF16)
        rows = slice(c * ch, (c + 1) * ch)
        y = jnp.dot(act, wd_ref[...], preferred_element_type=F32) * gate_ref[rows, :]
        y_ref[rows, :] = y.astype(y_ref.dtype)


def _ffn(idx_flat, h2p, gates4, wg, wu, wd, bsz, s_len, cap, ch):
    n_e, d, f = wg.shape
    rpt = h2p.shape[0] // (bsz * s_len)
    assert (cap // ch) % 2 == 0
    kern = functools.partial(_ffn_kernel, bsz=bsz, s_len=s_len, cap=cap, ch=ch, rpt=rpt, unroll=8)
    grid_spec = pltpu.PrefetchScalarGridSpec(
        num_scalar_prefetch=1,
        grid=(n_e, bsz),
        in_specs=[pl.BlockSpec(memory_space=pl.ANY),
                  pl.BlockSpec((None, None, cap, 1), lambda e, b, idx: (b, e, 0, 0)),
                  pl.BlockSpec((None, d, f), lambda e, b, idx: (e, 0, 0)),
                  pl.BlockSpec((None, d, f), lambda e, b, idx: (e, 0, 0)),
                  pl.BlockSpec((None, f, d), lambda e, b, idx: (e, 0, 0))],
        out_specs=pl.BlockSpec((None, None, cap, d), lambda e, b, idx: (b, e, 0, 0)),
        scratch_shapes=[pltpu.VMEM((ch * rpt, LANES), U32),
                        pltpu.VMEM((ch * rpt, LANES), U32),
                        pltpu.SemaphoreType.DMA((2,))],
    )
    return pl.pallas_call(
        kern,
        grid_spec=grid_spec,
        out_shape=jax.ShapeDtypeStruct((bsz, n_e, cap, d), BF16),
        compiler_params=_params(2),
        name="ffn",
    )(idx_flat, h2p, gates4, wg, wu, wd)


def _combine_kernel(off_ref, y_hbm, x1_ref, cumx_ref, sel_ref, gf_ref, o_ref, ybuf, yextra, sem, *,
                    cap, win, n_r, final):
    b = pl.program_id(0)
    r = pl.program_id(1)
    n = b * n_r + r
    n_tiles = pl.num_programs(0) * n_r
    slot = n % 2
    rows_all = N_EXPERTS * win

    def windows(bb, rr, k):
        obase = (bb * (n_r + 1) + rr) * N_EXPERTS
        out = []
        for e in range(N_EXPERTS):
            first = (off_ref[obase + e] // BF16_ROWS) * BF16_ROWS + k * win
            out.append((first, jnp.minimum(first, cap - win)))
        return out

    def start_copies(bb, wins, dst_ref, sem_k):
        for e, (_, begin) in enumerate(wins):
            src = pl.multiple_of((bb * N_EXPERTS + e) * cap + begin, BF16_ROWS)
            pltpu.make_async_copy(y_hbm.at[pl.ds(src, win), :], dst_ref.at[pl.ds(e * win, win), :],
                                  sem_k).start()

    def wait_copies(dst_ref, sem_k):
        pltpu.make_async_copy(y_hbm.at[pl.ds(0, rows_all), :], dst_ref, sem_k).wait()

    jcol = lax.broadcasted_iota(I32, (win, 1), 0).astype(F32)

    def expand(wins, rows_bf16):
        blocks = []
        for e, (first, begin) in enumerate(wins):
            rank = cumx_ref[e:e + 1, :]
            blocks.append(jnp.where(rank == jcol + begin.astype(F32),
                                    jnp.where(rank >= first.astype(F32), sel_ref[e:e + 1, :], 0.0), 0.0))
        onehot = jnp.concatenate(blocks, axis=0).astype(BF16)
        return lax.dot_general(onehot, rows_bf16, TN, preferred_element_type=F32)

    @pl.when(n == 0)
    def _():
        start_copies(b, windows(b, r, 0), ybuf.at[slot], sem.at[slot])

    @pl.when(n + 1 < n_tiles)
    def _():
        nb = (n + 1) // n_r
        start_copies(nb, windows(nb, (n + 1) % n_r, 0), ybuf.at[1 - slot], sem.at[1 - slot])

    obase = (b * (n_r + 1) + r) * N_EXPERTS
    rounds = jnp.int32(0)
    for e in range(N_EXPERTS):
        off = off_ref[obase + e]
        cnt = off_ref[obase + N_EXPERTS + e] - off
        used = off - (off // BF16_ROWS) * BF16_ROWS + cnt
        rounds = jnp.maximum(rounds, jnp.where(cnt > 0, (used + win - 1) // win, 0))

    wait_copies(ybuf.at[slot], sem.at[slot])
    o_ref[...] = x1_ref[...] + expand(windows(b, r, 0), ybuf[slot])

    def extra_round(k, carry):
        wins = windows(b, r, k)
        start_copies(b, wins, yextra, sem.at[2])
        wait_copies(yextra, sem.at[2])
        o_ref[...] += expand(wins, yextra[...])
        return carry
    lax.fori_loop(1, rounds, extra_round, 0)

    if final:
        x2 = o_ref[...]
        ms = jnp.mean(x2 * x2, axis=-1, keepdims=True)
        o_ref[...] = x2 * lax.rsqrt(ms + EPS) * gf_ref[...]


def _combine(off_flat, y2d, x1, cumx_t, sel_t, g_final, bsz, s_len, cap, win, final):
    t, d = x1.shape
    n_r = s_len // LANES
    kern = functools.partial(_combine_kernel, cap=cap, win=win, n_r=n_r, final=final)
    meta = pl.BlockSpec((None, None, N_EXPERTS, LANES), lambda b, r, off: (b, r, 0, 0))
    grid_spec = pltpu.PrefetchScalarGridSpec(
        num_scalar_prefetch=1,
        grid=(bsz, n_r),
        in_specs=[pl.BlockSpec(memory_space=pl.ANY),
                  pl.BlockSpec((LANES, d), lambda b, r, off: (b * n_r + r, 0)),
                  meta, meta,
                  pl.BlockSpec((1, d), lambda b, r, off: (0, 0))],
        out_specs=pl.BlockSpec((LANES, d), lambda b, r, off: (b * n_r + r, 0)),
        scratch_shapes=[pltpu.VMEM((2, N_EXPERTS * win, d), BF16),
                        pltpu.VMEM((N_EXPERTS * win, d), BF16),
                        pltpu.SemaphoreType.DMA((3,))],
    )
    return pl.pallas_call(
        kern,
        grid_spec=grid_spec,
        out_shape=jax.ShapeDtypeStruct((t, d), F32),
        compiler_params=_params(2),
        name="combine",
    )(off_flat, y2d, x1, cumx_t, sel_t, g_final)


def _rope_tables(s_len):
    rows = s_len // GRID_W
    r = jnp.repeat(jnp.arange(rows, dtype=F32), GRID_W)
    c = jnp.tile(jnp.arange(GRID_W, dtype=F32), rows)
    half = HEAD_DIM // 2
    inv = 1.0 / (ROPE_THETA ** (jnp.arange(0, half, 2, dtype=F32) / half))
    ang_r = r[:, None] * inv
    ang_c = c[:, None] * inv
    cos = jnp.concatenate([jnp.cos(ang_r), jnp.cos(ang_r), jnp.cos(ang_c), jnp.cos(ang_c)], axis=-1)
    sin = jnp.concatenate([-jnp.sin(ang_r), jnp.sin(ang_r), -jnp.sin(ang_c), jnp.sin(ang_c)], axis=-1)
    return cos, sin


def _t5_bucket_table():
    rel = (jnp.arange(3 * BLOCK) - BLOCK)[None, :] - jnp.arange(BLOCK)[:, None]
    half = NUM_BUCKETS // 2
    ret = jnp.where(rel > 0, half, 0)
    n = jnp.abs(rel)
    max_exact = half // 2
    nf = jnp.maximum(n, 1).astype(F32)
    large = max_exact + (jnp.log(nf / max_exact) / math.log(MAX_DISTANCE / max_exact)
                         * (half - max_exact)).astype(I32)
    large = jnp.minimum(large, half - 1)
    return (ret + jnp.where(n < max_exact, n, large)).astype(I32)


def kernel(x, g_mix, w_in, b_gate, qn_a, kn_a, w_proj_a, sink_b, rel_bias, w_proj_b, w_o, g_ffn, w_router,
           w_gate_e, w_up_e, w_down_e, g_final):
    bsz, s_len, d = x.shape
    depth = g_mix.shape[0]
    t = bsz * s_len
    cap = CAPACITY_FACTOR * s_len // N_EXPERTS
    n_r = s_len // LANES
    cos, sin = _rope_tables(s_len)
    bucket = _t5_bucket_table()
    x2d = x.reshape(t, d)
    for l in range(depth):
        proj = _inproj(x2d, g_mix[l][None, :], w_in[l].astype(BF16), tm=min(1024, t), tn=1792)
        oa = _attn_a(proj, cos, sin, qn_a[l], kn_a[l], bsz, s_len, d,
                     tq=min(256, s_len), tk=min(512, s_len))
        ob = _attn_b(proj, bucket, rel_bias, sink_b[l], bsz, s_len, d, tq=min(512, s_len))
        x1, h2p, aff = _outproj(oa, ob, proj, x2d, b_gate[l][None, :], w_proj_a[l].astype(BF16),
                                w_proj_b[l].astype(BF16), w_o[l].astype(BF16), g_ffn[l][None, :],
                                _split_router(w_router[l]), bsz, s_len, tm=512)
        idx, gates, cumx, sel = _topk(aff.reshape(bsz, N_EXPERTS, n_r, LANES), cap)
        y = _ffn(idx.reshape(-1), h2p, gates[..., None], w_gate_e[l].astype(BF16), w_up_e[l].astype(BF16),
                 w_down_e[l].astype(BF16), bsz, s_len, cap, ch=min(512, cap // 2))
        off = jnp.concatenate([cumx[:, :, :, 0].astype(I32), jnp.full((bsz, N_EXPERTS, 1), cap, I32)], axis=2)
        off_flat = off.transpose(0, 2, 1).reshape(-1)
        x2d = _combine(off_flat, y.reshape(bsz * N_EXPERTS * cap, d), x1, cumx.transpose(0, 2, 1, 3),
                       sel.transpose(0, 2, 1, 3), g_final[None, :], bsz, s_len, cap, win=48,
                       final=(l == depth - 1))
    return x2d.reshape(bsz, s_len, d)
```

```python
import functools
import math

import jax
import jax.numpy as jnp
from jax import lax
from jax.experimental import pallas as pl
from jax.experimental.pallas import tpu as pltpu

F32 = jnp.float32
BF16 = jnp.bfloat16
I32 = jnp.int32
U32 = jnp.uint32

HEAD_DIM = 128
GROUP = 4
BLOCK = 128
GRID_W = 64
ROPE_THETA = 10000.0
NUM_BUCKETS = 32
MAX_DISTANCE = 128
N_EXPERTS = 16
CAPACITY_FACTOR = 2
EPS = 1e-6
NEG = -1e30
LANES = 128
SUBLANES = 8
BF16_ROWS = 16
ONES_ROWS = BF16_ROWS
SAFE_EXP2_RANGE = 100.0
ROW_CHAINS = 2
KV_UNROLL = 8
VMEM_LIMIT = 56 * 1024 * 1024
HIGHEST = lax.Precision.HIGHEST
NT = (((1,), (1,)), ((), ()))
TN = (((0,), (0,)), ((), ()))


def _params(n_axes):
    return pltpu.CompilerParams(dimension_semantics=("arbitrary",) * n_axes,
                                vmem_limit_bytes=VMEM_LIMIT)


def _inproj_kernel(x_ref, g_ref, w_ref, o_ref, h_ref):
    @pl.when(pl.program_id(1) == 0)
    def _():
        x = x_ref[...]
        ms = jnp.mean(x * x, axis=-1, keepdims=True)
        h_ref[...] = (x * lax.rsqrt(ms + EPS) * g_ref[...]).astype(BF16)

    o_ref[...] = jnp.dot(h_ref[...], w_ref[...], preferred_element_type=F32).astype(o_ref.dtype)


def _inproj(x2d, g, w, tm, tn):
    t, d = x2d.shape
    n = w.shape[1]
    return pl.pallas_call(
        _inproj_kernel,
        grid=(t // tm, n // tn),
        in_specs=[pl.BlockSpec((tm, d), lambda i, j: (i, 0)),
                  pl.BlockSpec((1, d), lambda i, j: (0, 0)),
                  pl.BlockSpec((d, tn), lambda i, j: (0, j))],
        out_specs=pl.BlockSpec((tm, tn), lambda i, j: (i, j)),
        out_shape=jax.ShapeDtypeStruct((t, n), BF16),
        scratch_shapes=[pltpu.VMEM((tm, d), BF16)],
        compiler_params=_params(2),
        name="inproj",
    )(x2d, g, w)


def _norm_rope(xb, gain, gain_sw, cos, sin_signed):
    a = lax.broadcasted_iota(I32, (HEAD_DIM, HEAD_DIM), 0)
    c = lax.broadcasted_iota(I32, (HEAD_DIM, HEAD_DIM), 1)
    swap_mat = jnp.where(a == (c ^ 32), 1.0, 0.0).astype(BF16)
    ones_mat = jnp.ones((HEAD_DIM, HEAD_DIM), BF16)
    x = xb.astype(F32)
    sq = x * x
    sq_hi = sq.astype(BF16)
    sq_lo = (sq - sq_hi.astype(F32)).astype(BF16)
    ms = (jnp.dot(sq_hi, ones_mat, preferred_element_type=F32)
          + jnp.dot(sq_lo, ones_mat, preferred_element_type=F32)) * (1.0 / HEAD_DIM)
    swapped = jnp.dot(xb, swap_mat, preferred_element_type=F32)
    return lax.rsqrt(ms + EPS) * (x * (gain * cos) + swapped * (gain_sw * sin_signed))


def _attn_a_kernel(q_ref, k_ref, v_ref, cq_ref, sq_ref, ck_ref, sk_ref, qn_ref, kn_ref, o_ref,
                   kp_ref, vt_ref, qt_ref, ksq_ref, m_ref, acc_ref, st0_ref, st1_ref, pt0_ref, pt1_ref,
                   *, tq, tk):
    s_len = k_ref.shape[0]
    n_chunks = s_len // tk

    @pl.when(pl.program_id(2) == 0)
    def _():
        ksq_ref[...] = jnp.zeros(ksq_ref.shape, F32)

        def body(c, carry):
            off = pl.multiple_of(c * tk, tk)
            kb = _norm_rope(k_ref[pl.ds(off, tk), :], kn_ref[0:1, :], kn_ref[1:2, :],
                            ck_ref[pl.ds(off, tk), :], sk_ref[pl.ds(off, tk), :]).astype(BF16)
            kp_ref[pl.ds(off, tk), :] = kb
            kf = kb.astype(F32)
            ksq_ref[...] = jnp.maximum(ksq_ref[...], jnp.max(jnp.sum(kf * kf, axis=1, keepdims=True)))
            vt_ref[c, :HEAD_DIM, :] = v_ref[pl.ds(off, tk), :].astype(F32).T.astype(BF16)
            vt_ref[c, HEAD_DIM:, :] = jnp.ones((ONES_ROWS, tk), BF16)
            return carry
        lax.fori_loop(0, n_chunks, body, 0)

    scale = math.log2(math.e) / math.sqrt(HEAD_DIM)
    qsq = jnp.zeros((1, tq), F32)
    for g in range(GROUP):
        qg = _norm_rope(q_ref[:, g * HEAD_DIM:(g + 1) * HEAD_DIM], qn_ref[0:1, :], qn_ref[1:2, :],
                        cq_ref[...], sq_ref[...]) * scale
        qb = qg.T.astype(BF16)
        qt_ref[g] = qb
        qf = qb.astype(F32)
        qsq = jnp.maximum(qsq, jnp.sum(qf * qf, axis=0, keepdims=True))
    bound_sq = jnp.max(qsq) * jnp.max(ksq_ref[...])
    no_max_needed = bound_sq <= SAFE_EXP2_RANGE * SAFE_EXP2_RANGE

    acc_ref[...] = jnp.zeros(acc_ref.shape, F32)
    unroll = math.gcd(KV_UNROLL, n_chunks)

    @pl.when(no_max_needed)
    def _():
        def probs(c, dst_ref):
            kc = kp_ref[pl.ds(pl.multiple_of(c * tk, tk), tk), :]
            for g in range(GROUP):
                dst_ref[g] = jnp.exp2(jnp.dot(kc, qt_ref[g], preferred_element_type=F32)).astype(BF16)

        def weighted_sum(c, src_ref):
            vt = vt_ref[c]
            for g in range(GROUP):
                acc_ref[g] += jnp.dot(vt, src_ref[g], preferred_element_type=F32)

        probs(0, pt0_ref)

        def kv_body(j, carry):
            bufs = (pt0_ref, pt1_ref)
            for u in range(unroll):
                c = unroll * j + u
                probs(jnp.minimum(c + 1, n_chunks - 1), bufs[(u + 1) % 2])
                weighted_sum(c, bufs[u % 2])
            return carry
        lax.fori_loop(0, n_chunks // unroll, kv_body, 0)

    @pl.when(jnp.logical_not(no_max_needed))
    def _():
        m_ref[...] = jnp.full(m_ref.shape, -jnp.inf, F32)

        def scores(c, dst_ref):
            kc = kp_ref[pl.ds(pl.multiple_of(c * tk, tk), tk), :]
            for g in range(GROUP):
                dst_ref[g] = jnp.dot(kc, qt_ref[g], preferred_element_type=F32)

        def softmax_pv(c, src_ref):
            vt = vt_ref[c]
            for g in range(GROUP):
                st = src_ref[g]
                m_prev = m_ref[g]
                m_new = jnp.maximum(m_prev, jnp.max(st, axis=0, keepdims=True))
                pt = jnp.exp2(st - m_new).astype(BF16)
                acc_ref[g] = (jnp.exp2(m_prev - m_new) * acc_ref[g]
                              + jnp.dot(vt, pt, preferred_element_type=F32))
                m_ref[g] = m_new

        scores(0, st0_ref)

        def kv_body(j, carry):
            c = 2 * j
            scores(c + 1, st1_ref)
            softmax_pv(c, st0_ref)
            scores(jnp.minimum(c + 2, n_chunks - 1), st0_ref)
            softmax_pv(c + 1, st1_ref)
            return carry
        lax.fori_loop(0, n_chunks // 2, kv_body, 0)

    for g in range(GROUP):
        acc = acc_ref[g]
        o = (acc[:HEAD_DIM] / acc[HEAD_DIM:HEAD_DIM + 1]).T
        o_ref[:, g * HEAD_DIM:(g + 1) * HEAD_DIM] = o.astype(o_ref.dtype)


def _attn_a(proj, cos, sin, qn, kn, bsz, s_len, d, tq, tk):
    q_w = d // 2
    kv_w = d // 8
    kvh = kv_w // HEAD_DIM
    gw = GROUP * HEAD_DIM
    nq = s_len // tq
    k_blk = q_w // HEAD_DIM
    v_blk = (q_w + kv_w) // HEAD_DIM
    assert (s_len // tk) % 2 == 0

    def with_swapped(gain):
        return jnp.stack([gain, gain.reshape(2, 2, 32)[:, ::-1, :].reshape(HEAD_DIM)])

    qn, kn = with_swapped(qn), with_swapped(kn)
    kern = functools.partial(_attn_a_kernel, tq=tq, tk=tk)
    return pl.pallas_call(
        kern,
        grid=(bsz, kvh, nq),
        in_specs=[pl.BlockSpec((tq, gw), lambda b, k, i: (b * nq + i, k)),
                  pl.BlockSpec((s_len, HEAD_DIM), lambda b, k, i: (b, k_blk + k)),
                  pl.BlockSpec((s_len, HEAD_DIM), lambda b, k, i: (b, v_blk + k)),
                  pl.BlockSpec((tq, HEAD_DIM), lambda b, k, i: (i, 0)),
                  pl.BlockSpec((tq, HEAD_DIM), lambda b, k, i: (i, 0)),
                  pl.BlockSpec((s_len, HEAD_DIM), lambda b, k, i: (0, 0)),
                  pl.BlockSpec((s_len, HEAD_DIM), lambda b, k, i: (0, 0)),
                  pl.BlockSpec((2, HEAD_DIM), lambda b, k, i: (0, 0)),
                  pl.BlockSpec((2, HEAD_DIM), lambda b, k, i: (0, 0))],
        out_specs=pl.BlockSpec((tq, gw), lambda b, k, i: (b * nq + i, k)),
        out_shape=jax.ShapeDtypeStruct((bsz * s_len, q_w), BF16),
        scratch_shapes=[pltpu.VMEM((s_len, HEAD_DIM), BF16),
                        pltpu.VMEM((s_len // tk, HEAD_DIM + ONES_ROWS, tk), BF16),
                        pltpu.VMEM((GROUP, HEAD_DIM, tq), BF16),
                        pltpu.VMEM((1, LANES), F32),
                        pltpu.VMEM((GROUP, 1, tq), F32),
                        pltpu.VMEM((GROUP, HEAD_DIM + ONES_ROWS, tq), F32),
                        pltpu.VMEM((GROUP, tk, tq), F32),
                        pltpu.VMEM((GROUP, tk, tq), F32),
                        pltpu.VMEM((GROUP, tk, tq), BF16),
                        pltpu.VMEM((GROUP, tk, tq), BF16)],
        compiler_params=_params(3),
        name="attn_a",
    )(proj, proj, proj, cos, sin, cos, sin, qn, kn)


def _attn_b_kernel(relb_ref, sink_ref, bucket_ref, q_ref, kp_ref, kc_ref, kn_ref, vp_ref, vc_ref, vn_ref,
                   o_ref, bias_ref, st_ref, *, tq, s_len, kvh):
    b = pl.program_id(0)
    k = pl.program_id(1)
    i = pl.program_id(2)
    band = 3 * BLOCK
    log2e = math.log2(math.e)

    @pl.when((b == 0) & (k == 0) & (i == 0))
    def _():
        bucket_t = bucket_ref[...]
        key = lax.broadcasted_iota(I32, (band, BLOCK), 0)
        qry = lax.broadcasted_iota(I32, (band, BLOCK), 1)
        in_window = jnp.abs(key - BLOCK - qry) <= BLOCK
        for kk in range(kvh):
            for g in range(GROUP):
                h = kk * GROUP + g
                tab = jnp.zeros((band, BLOCK), F32)
                for bkt in range(NUM_BUCKETS):
                    tab = jnp.where(bucket_t == bkt, relb_ref[bkt, h], tab)
                tab = jnp.where(in_window, tab * log2e, NEG)
                cols = slice(g * BLOCK, (g + 1) * BLOCK)
                bias_ref[0, kk, :, cols] = tab
                bias_ref[1, kk, :, cols] = jnp.where(key >= BLOCK, tab, NEG)
                bias_ref[2, kk, :, cols] = jnp.where(key < 2 * BLOCK, tab, NEG)

    nsub = tq // BLOCK
    kcat = jnp.concatenate([kp_ref[...], kc_ref[...], kn_ref[...]], axis=0)
    vcat = jnp.concatenate([vp_ref[...], vc_ref[...], vn_ref[...]], axis=0)
    vt = jnp.concatenate([vcat.astype(F32).T.astype(BF16), jnp.ones((ONES_ROWS, vcat.shape[0]), BF16)], axis=0)
    sink = jnp.concatenate([jnp.full((1, BLOCK), sink_ref[k * GROUP + g], F32) for g in range(GROUP)],
                           axis=1) * log2e
    scale = log2e / math.sqrt(HEAD_DIM)
    for jb in range(nsub):
        rows = slice(jb * BLOCK, (jb + 1) * BLOCK)
        qt = jnp.concatenate([q_ref[rows, g * HEAD_DIM:(g + 1) * HEAD_DIM].astype(F32).T.astype(BF16)
                              for g in range(GROUP)], axis=1)
        if jb == 0:
            variant = jnp.where(i == 0, 1, 0)
        elif jb == nsub - 1:
            variant = jnp.where(i == pl.num_programs(2) - 1, 2, 0)
        else:
            variant = 0
        st_ref[jb] = (jnp.dot(kcat[jb * BLOCK:jb * BLOCK + band], qt, preferred_element_type=F32) * scale
                      + bias_ref[variant, k])
    for jb in range(nsub):
        rows = slice(jb * BLOCK, (jb + 1) * BLOCK)
        st = st_ref[jb]
        m = jnp.maximum(jnp.max(st, axis=0, keepdims=True), sink)
        e = jnp.exp2(st - m).astype(BF16)
        acc = jnp.dot(vt[:, jb * BLOCK:jb * BLOCK + band], e, preferred_element_type=F32)
        o = acc[:HEAD_DIM] / (acc[HEAD_DIM:HEAD_DIM + 1] + jnp.exp2(sink - m))
        for g in range(GROUP):
            o_ref[rows, g * HEAD_DIM:(g + 1) * HEAD_DIM] = o[:, g * BLOCK:(g + 1) * BLOCK].T.astype(o_ref.dtype)


def _attn_b(proj, bucket, rel_bias, sink, bsz, s_len, d, tq):
    q_w = d // 2
    kv_w = d // 8
    kvh = kv_w // HEAD_DIM
    gw = GROUP * HEAD_DIM
    nq = s_len // tq
    sub = tq // BLOCK
    nblk = s_len // BLOCK
    q_blk = (q_w + 2 * kv_w) // gw
    k_blk = (2 * q_w + 2 * kv_w) // HEAD_DIM
    v_blk = (2 * q_w + 3 * kv_w) // HEAD_DIM

    def prev_map(col):
        return lambda b, k, i: (b * nblk + jnp.maximum(i * sub - 1, 0), col + k)

    def cur_map(col):
        return lambda b, k, i: (b * nq + i, col + k)

    def next_map(col):
        return lambda b, k, i: (b * nblk + jnp.minimum((i + 1) * sub, nblk - 1), col + k)

    small = (BLOCK, HEAD_DIM)
    assert sub >= 2
    kern = functools.partial(_attn_b_kernel, tq=tq, s_len=s_len, kvh=kvh)
    return pl.pallas_call(
        kern,
        grid=(bsz, kvh, nq),
        in_specs=[pl.BlockSpec(memory_space=pltpu.SMEM),
                  pl.BlockSpec(memory_space=pltpu.SMEM),
                  pl.BlockSpec((3 * BLOCK, BLOCK), lambda b, k, i: (0, 0)),
                  pl.BlockSpec((tq, gw), lambda b, k, i: (b * nq + i, q_blk + k)),
                  pl.BlockSpec(small, prev_map(k_blk)),
                  pl.BlockSpec((tq, HEAD_DIM), cur_map(k_blk)),
                  pl.BlockSpec(small, next_map(k_blk)),
                  pl.BlockSpec(small, prev_map(v_blk)),
                  pl.BlockSpec((tq, HEAD_DIM), cur_map(v_blk)),
                  pl.BlockSpec(small, next_map(v_blk))],
        out_specs=pl.BlockSpec((tq, gw), lambda b, k, i: (b * nq + i, k)),
        out_shape=jax.ShapeDtypeStruct((bsz * s_len, q_w), BF16),
        scratch_shapes=[pltpu.VMEM((3, kvh, 3 * BLOCK, GROUP * BLOCK), F32),
                        pltpu.VMEM((sub, 3 * BLOCK, GROUP * BLOCK), F32)],
        compiler_params=_params(3),
        name="attn_b",
    )(rel_bias, sink, bucket.T, proj, proj, proj, proj, proj, proj, proj)


def _outproj_kernel(oa_ref, ob_ref, ga0_ref, ga1_ref, gb0_ref, gb1_ref, x_ref, bg_ref, wpa_ref, wpb_ref,
                    wo_ref, gf_ref, wr_ref, x1_ref, h2p_ref, aff_ref):
    tm, d = x_ref.shape
    half = d // 2
    rpt = half // LANES
    rn = tm // ROW_CHAINS
    for rc in range(ROW_CHAINS):
        rows = slice(rc * rn, (rc + 1) * rn)
        oa = oa_ref[rows, :]
        ob = ob_ref[rows, :]
        parts = []
        for c, (ga_ref, gb_ref) in enumerate(((ga0_ref, gb0_ref), (ga1_ref, gb1_ref))):
            cols = slice(c * half, (c + 1) * half)
            pa = jnp.dot(oa, wpa_ref[:, cols], preferred_element_type=F32)
            pb = jnp.dot(ob, wpb_ref[:, cols], preferred_element_type=F32)
            gate_a = jax.nn.sigmoid(ga_ref[rows, :].astype(F32) + bg_ref[:, cols])
            gate_b = jax.nn.sigmoid(gb_ref[rows, :].astype(F32) + bg_ref[:, d + c * half:d + (c + 1) * half])
            parts.append((gate_a * pa + gate_b * pb).astype(BF16))
        x1 = x_ref[rows, :] + (jnp.dot(parts[0], wo_ref[:half, :], preferred_element_type=F32)
                               + jnp.dot(parts[1], wo_ref[half:, :], preferred_element_type=F32))
        x1_ref[rows, :] = x1
        ms = jnp.mean(x1 * x1, axis=-1, keepdims=True)
        h = x1 * lax.rsqrt(ms + EPS) * gf_ref[...]
        h_hi = h.astype(BF16)
        bits = pltpu.bitcast(h_hi.astype(F32), U32)
        packed = (bits[:, :half] >> 16) | (bits[:, half:] & jnp.uint32(0xFFFF0000))
        for s in range(rpt):
            h2p_ref[pl.ds(rc * rn * rpt + s, rn, stride=rpt), :] = packed[:, s * LANES:(s + 1) * LANES]
        h_lo = (h - h_hi.astype(F32)).astype(BF16)
        prod = (jnp.dot(h_hi, wr_ref[...], preferred_element_type=F32)
                + jnp.dot(h_lo, wr_ref[...], preferred_element_type=F32))
        logits = (prod[:, :LANES] + prod[:, LANES:]).T[:N_EXPERTS, :]
        ex = jnp.exp(logits - jnp.max(logits, axis=0, keepdims=True))
        aff_ref[:, rows] = ex / jnp.sum(ex, axis=0, keepdims=True)


def _split_router(w):
    w_hi = w.astype(BF16)
    w_lo = (w - w_hi.astype(F32)).astype(BF16)
    pad = ((0, 0), (0, LANES - w.shape[1]))
    return jnp.concatenate([jnp.pad(w_hi, pad), jnp.pad(w_lo, pad)], axis=1)


def _outproj(oa, ob, proj, x2d, b_gate, wpa, wpb, wo, g_ffn, wr_split, bsz, s_len, tm):
    t, d = x2d.shape
    half = d // 2
    nt = s_len // tm
    row = lambda c: (lambda i: (i, c))
    const = lambda i: (0, 0)
    return pl.pallas_call(
        _outproj_kernel,
        grid=(t // tm,),
        in_specs=[pl.BlockSpec((tm, half), row(0)),
                  pl.BlockSpec((tm, half), row(0)),
                  pl.BlockSpec((tm, half), row(3)),
                  pl.BlockSpec((tm, half), row(4)),
                  pl.BlockSpec((tm, half), row(5)),
                  pl.BlockSpec((tm, half), row(6)),
                  pl.BlockSpec((tm, d), row(0)),
                  pl.BlockSpec((1, 2 * d), const),
                  pl.BlockSpec((half, d), const, pipeline_mode=pl.Buffered(1)),
                  pl.BlockSpec((half, d), const, pipeline_mode=pl.Buffered(1)),
                  pl.BlockSpec((d, d), const, pipeline_mode=pl.Buffered(1)),
                  pl.BlockSpec((1, d), const),
                  pl.BlockSpec((d, 2 * LANES), const, pipeline_mode=pl.Buffered(1))],
        out_specs=[pl.BlockSpec((tm, d), row(0)),
                   pl.BlockSpec((tm * (half // LANES), LANES), row(0)),
                   pl.BlockSpec((None, N_EXPERTS, tm), lambda i: (i // nt, 0, i % nt))],
        out_shape=[jax.ShapeDtypeStruct((t, d), F32),
                   jax.ShapeDtypeStruct((t * (half // LANES), LANES), U32),
                   jax.ShapeDtypeStruct((bsz, N_EXPERTS, s_len), F32)],
        compiler_params=_params(1),
        name="outproj",
    )(oa, ob, proj, proj, proj, proj, x2d, b_gate, wpa, wpb, wo, g_ffn, wr_split)


def _topk_kernel(a_ref, idx_ref, gate_ref, cumx_ref, sel_ref, *, cap, rpt):
    a = a_ref[...]
    n_e, n_r, _ = a.shape
    bits = pltpu.bitcast(a, I32)

    def total(x):
        return jnp.sum(jnp.sum(x, axis=1, keepdims=True), axis=2, keepdims=True)

    def search(_, carry):
        lo, hi = carry
        mid = lo + ((hi - lo) >> 1)
        enough = total(jnp.where(bits >= mid, 1.0, 0.0)) >= cap
        return jnp.where(enough, mid, lo), jnp.where(enough, hi, mid)

    lo0 = jnp.zeros((n_e, 1, 1), I32)
    hi0 = jnp.full((n_e, 1, 1), 0x7F800000, I32)
    thr, _ = lax.fori_loop(0, 31, search, (lo0, hi0))

    ri = lax.broadcasted_iota(I32, (LANES, LANES), 0)
    ci = lax.broadcasted_iota(I32, (LANES, LANES), 1)
    upper = jnp.where(ri <= ci, 1.0, 0.0).astype(BF16)
    ones = jnp.ones((LANES, LANES), BF16)
    rr = lax.broadcasted_iota(I32, (n_r, n_r), 0)
    rc = lax.broadcasted_iota(I32, (n_r, n_r), 1)
    strict_lower = jnp.where(rc < rr, 1.0, 0.0).astype(BF16)

    def prefix(x):
        x2 = x.reshape(n_e * n_r, LANES).astype(BF16)
        in_row = jnp.dot(x2, upper, preferred_element_type=F32).reshape(n_e, n_r, LANES)
        row_tot = jnp.dot(x2, ones, preferred_element_type=F32).reshape(n_e, n_r, LANES)
        before = jnp.stack([jnp.dot(strict_lower, row_tot[e].astype(BF16), preferred_element_type=F32)
                            for e in range(n_e)], axis=0)
        return in_row + before

    above = jnp.where(bits > thr, 1.0, 0.0)
    tied = jnp.where(bits == thr, 1.0, 0.0)
    need = cap - total(above)
    sel = above + tied * jnp.where(prefix(tied) <= need, 1.0, 0.0)
    cum = prefix(sel)
    sel_ref[...] = sel
    cumx_ref[...] = cum - sel

    slot = lax.broadcasted_iota(I32, (1, cap), 1).astype(F32)
    row_id = lax.broadcasted_iota(I32, (n_r, 1), 0).astype(F32)
    lane_id = lax.broadcasted_iota(I32, (LANES, 1), 0).astype(F32)
    for e in range(n_e):
        c = cum[e]
        row_end = c[:, LANES - 1:LANES]
        srow = jnp.sum(jnp.where(row_end <= slot, 1.0, 0.0), axis=0, keepdims=True)
        pick = jnp.where(row_id == srow, 1.0, 0.0)
        c_row = lax.dot_general(c, pick, TN, precision=HIGHEST, preferred_element_type=F32)
        slane = jnp.sum(jnp.where(c_row <= slot, 1.0, 0.0), axis=0, keepdims=True)
        a_row = lax.dot_general(a[e], pick, TN, precision=HIGHEST, preferred_element_type=F32)
        token = (srow * LANES + slane).astype(I32) + pl.program_id(0) * (n_r * LANES)
        idx_ref[e:e + 1, :] = token * rpt
        gate_ref[e:e + 1, :] = jnp.sum(jnp.where(lane_id == slane, a_row, 0.0), axis=0, keepdims=True)


def _topk(aff4, cap, rpt):
    bsz, n_e, n_r, _ = aff4.shape
    blk4 = pl.BlockSpec((None, n_e, n_r, LANES), lambda b: (b, 0, 0, 0))
    blk3 = pl.BlockSpec((None, n_e, cap), lambda b: (b, 0, 0))
    return pl.pallas_call(
        functools.partial(_topk_kernel, cap=cap, rpt=rpt),
        grid=(bsz,),
        in_specs=[blk4],
        out_specs=[blk3, blk3, blk4, blk4],
        out_shape=[jax.ShapeDtypeStruct((bsz, n_e, cap), I32),
                   jax.ShapeDtypeStruct((bsz, n_e, cap), F32),
                   jax.ShapeDtypeStruct(aff4.shape, F32),
                   jax.ShapeDtypeStruct(aff4.shape, F32)],
        compiler_params=_params(1),
        name="topk",
    )(aff4)


def _ffn_kernel(idx_ref, h2_hbm, wg_hbm, wu_hbm, wd_hbm, gate_ref, y_ref, xa, xb, wg_ref, wu_ref, wd_ref,
                stg_g, stg_u, stg_d, sem, wsem, *, bsz, s_len, cap, ch, rpt, unroll):
    n_c = cap // ch
    bufs = (xa, xb)
    e = pl.program_id(0)
    b = pl.program_id(1)
    pair = e * bsz + b
    n_pairs = pl.num_programs(0) * bsz
    slot = e % 2
    rows_gu = stg_g.shape[0]
    rows_d = stg_d.shape[0]

    def weight_copies(ee, part):
        r_gu = pl.ds(pl.multiple_of(part * rows_gu, rows_gu), rows_gu)
        r_d = pl.ds(pl.multiple_of(part * rows_d, rows_d), rows_d)
        return (pltpu.make_async_copy(wg_hbm.at[ee, r_gu, :], stg_g, wsem.at[0]),
                pltpu.make_async_copy(wu_hbm.at[ee, r_gu, :], stg_u, wsem.at[1]),
                pltpu.make_async_copy(wd_hbm.at[ee, r_d, :], stg_d, wsem.at[2]))

    def convert(to_slot, part):
        r_gu = pl.ds(pl.multiple_of(part * rows_gu, rows_gu), rows_gu)
        r_d = pl.ds(pl.multiple_of(part * rows_d, rows_d), rows_d)
        wg_ref[to_slot, r_gu, :] = stg_g[...].astype(BF16)
        wu_ref[to_slot, r_gu, :] = stg_u[...].astype(BF16)
        wd_ref[to_slot, r_d, :] = stg_d[...].astype(BF16)

    @pl.when(pair == 0)
    def _():
        for part in range(bsz):
            copies = weight_copies(0, part)
            for cp in copies:
                cp.start()
            for cp in copies:
                cp.wait()
            convert(0, part)

    @pl.when(e + 1 < pl.num_programs(0))
    def _():
        for cp in weight_copies(e + 1, b):
            cp.start()

    def issue(pair_k, c):
        bb = pair_k % bsz
        ee = pair_k // bsz
        idx_base = (bb * N_EXPERTS + ee) * cap + c * ch
        buf = bufs[c % 2]

        def body(j, carry):
            for u in range(unroll):
                r = j * unroll + u
                src = pl.multiple_of(idx_ref[idx_base + r], rpt)
                dst = pl.multiple_of(r * rpt, rpt)
                pltpu.make_async_copy(h2_hbm.at[pl.ds(src, rpt), :], buf.at[pl.ds(dst, rpt), :],
                                      sem.at[c % 2]).start()
            return carry
        lax.fori_loop(0, ch // unroll, body, 0)

    @pl.when(pair == 0)
    def _():
        issue(pair, 0)

    for c in range(n_c):
        if c + 1 < n_c:
            issue(pair, c + 1)
        else:
            @pl.when(pair + 1 < n_pairs)
            def _():
                issue(pair + 1, 0)
        buf = bufs[c % 2]
        pltpu.make_async_copy(h2_hbm.at[pl.ds(0, ch * rpt), :], buf, sem.at[c % 2]).wait()
        lo, hi = [], []
        for s in range(rpt):
            w = buf[pl.ds(s, ch, stride=rpt), :]
            lo.append(pltpu.bitcast(w << 16, F32).astype(BF16))
            hi.append(pltpu.bitcast(w & jnp.uint32(0xFFFF0000), F32).astype(BF16))
        x = jnp.concatenate(lo + hi, axis=1)
        a = jnp.dot(x, wg_ref[slot], preferred_element_type=F32)
        u = jnp.dot(x, wu_ref[slot], preferred_element_type=F32)
        act = (jax.nn.silu(a) * u).astype(BF16)
        rows = slice(c * ch, (c + 1) * ch)
        y = jnp.dot(act, wd_ref[slot], preferred_element_type=F32) * gate_ref[rows, :]
        y_ref[rows, :] = y.astype(y_ref.dtype)

    @pl.when(e + 1 < pl.num_programs(0))
    def _():
        for cp in weight_copies(e + 1, b):
            cp.wait()
        convert(1 - slot, b)


def _ffn(idx_flat, h2p, gates4, wg, wu, wd, bsz, s_len, cap, ch):
    n_e, d, f = wg.shape
    rpt = h2p.shape[0] // (bsz * s_len)
    assert (cap // ch) % 2 == 0 and d % bsz == 0 and f % bsz == 0
    kern = functools.partial(_ffn_kernel, bsz=bsz, s_len=s_len, cap=cap, ch=ch, rpt=rpt, unroll=8)
    grid_spec = pltpu.PrefetchScalarGridSpec(
        num_scalar_prefetch=1,
        grid=(n_e, bsz),
        in_specs=[pl.BlockSpec(memory_space=pl.ANY),
                  pl.BlockSpec(memory_space=pl.ANY),
                  pl.BlockSpec(memory_space=pl.ANY),
                  pl.BlockSpec(memory_space=pl.ANY),
                  pl.BlockSpec((None, None, cap, 1), lambda e, b, idx: (b, e, 0, 0))],
        out_specs=pl.BlockSpec((None, None, cap, d), lambda e, b, idx: (b, e, 0, 0)),
        scratch_shapes=[pltpu.VMEM((ch * rpt, LANES), U32),
                        pltpu.VMEM((ch * rpt, LANES), U32),
                        pltpu.VMEM((2, d, f), BF16),
                        pltpu.VMEM((2, d, f), BF16),
                        pltpu.VMEM((2, f, d), BF16),
                        pltpu.VMEM((d // bsz, f), F32),
                        pltpu.VMEM((d // bsz, f), F32),
                        pltpu.VMEM((f // bsz, d), F32),
                        pltpu.SemaphoreType.DMA((2,)),
                        pltpu.SemaphoreType.DMA((3,))],
    )
    return pl.pallas_call(
        kern,
        grid_spec=grid_spec,
        out_shape=jax.ShapeDtypeStruct((bsz, n_e, cap, d), BF16),
        compiler_params=_params(2),
        name="ffn",
    )(idx_flat, h2p, wg, wu, wd, gates4)


def _combine_kernel(off_ref, y_hbm, x1_ref, cumx_ref, sel_ref, gf_ref, o_ref, ybuf, yextra, sem, *,
                    cap, win, n_r, final):
    b = pl.program_id(0)
    r = pl.program_id(1)
    n = b * n_r + r
    n_tiles = pl.num_programs(0) * n_r
    slot = n % 2
    rows_all = N_EXPERTS * win

    def windows(bb, rr, k):
        obase = (bb * (n_r + 1) + rr) * N_EXPERTS
        out = []
        for e in range(N_EXPERTS):
            first = (off_ref[obase + e] // BF16_ROWS) * BF16_ROWS + k * win
            out.append((first, jnp.minimum(first, cap - win)))
        return out

    def start_copies(bb, wins, dst_ref, sem_k):
        for e, (_, begin) in enumerate(wins):
            src = pl.multiple_of((bb * N_EXPERTS + e) * cap + begin, BF16_ROWS)
            pltpu.make_async_copy(y_hbm.at[pl.ds(src, win), :], dst_ref.at[pl.ds(e * win, win), :],
                                  sem_k).start()

    def wait_copies(dst_ref, sem_k):
        pltpu.make_async_copy(y_hbm.at[pl.ds(0, rows_all), :], dst_ref, sem_k).wait()

    jcol = lax.broadcasted_iota(I32, (win, 1), 0).astype(F32)

    def expand(wins, rows_bf16):
        blocks = []
        for e, (first, begin) in enumerate(wins):
            rank = cumx_ref[e:e + 1, :]
            blocks.append(jnp.where(rank == jcol + begin.astype(F32),
                                    jnp.where(rank >= first.astype(F32), sel_ref[e:e + 1, :], 0.0), 0.0))
        onehot = jnp.concatenate(blocks, axis=0).astype(BF16)
        return lax.dot_general(onehot, rows_bf16, TN, preferred_element_type=F32)

    @pl.when(n == 0)
    def _():
        start_copies(b, windows(b, r, 0), ybuf.at[slot], sem.at[slot])

    @pl.when(n + 1 < n_tiles)
    def _():
        nb = (n + 1) // n_r
        start_copies(nb, windows(nb, (n + 1) % n_r, 0), ybuf.at[1 - slot], sem.at[1 - slot])

    obase = (b * (n_r + 1) + r) * N_EXPERTS
    rounds = jnp.int32(0)
    for e in range(N_EXPERTS):
        off = off_ref[obase + e]
        cnt = off_ref[obase + N_EXPERTS + e] - off
        used = off - (off // BF16_ROWS) * BF16_ROWS + cnt
        rounds = jnp.maximum(rounds, jnp.where(cnt > 0, (used + win - 1) // win, 0))

    wait_copies(ybuf.at[slot], sem.at[slot])
    o_ref[...] = x1_ref[...] + expand(windows(b, r, 0), ybuf[slot])

    def extra_round(k, carry):
        wins = windows(b, r, k)
        start_copies(b, wins, yextra, sem.at[2])
        wait_copies(yextra, sem.at[2])
        o_ref[...] += expand(wins, yextra[...])
        return carry
    lax.fori_loop(1, rounds, extra_round, 0)

    if final:
        x2 = o_ref[...]
        ms = jnp.mean(x2 * x2, axis=-1, keepdims=True)
        o_ref[...] = x2 * lax.rsqrt(ms + EPS) * gf_ref[...]


def _combine(off_flat, y2d, x1, cumx_t, sel_t, g_final, bsz, s_len, cap, win, final):
    t, d = x1.shape
    n_r = s_len // LANES
    kern = functools.partial(_combine_kernel, cap=cap, win=win, n_r=n_r, final=final)
    meta = pl.BlockSpec((None, None, N_EXPERTS, LANES), lambda b, r, off: (b, r, 0, 0))
    grid_spec = pltpu.PrefetchScalarGridSpec(
        num_scalar_prefetch=1,
        grid=(bsz, n_r),
        in_specs=[pl.BlockSpec(memory_space=pl.ANY),
                  pl.BlockSpec((LANES, d), lambda b, r, off: (b * n_r + r, 0)),
                  meta, meta,
                  pl.BlockSpec((1, d), lambda b, r, off: (0, 0))],
        out_specs=pl.BlockSpec((LANES, d), lambda b, r, off: (b * n_r + r, 0)),
        scratch_shapes=[pltpu.VMEM((2, N_EXPERTS * win, d), BF16),
                        pltpu.VMEM((N_EXPERTS * win, d), BF16),
                        pltpu.SemaphoreType.DMA((3,))],
    )
    return pl.pallas_call(
        kern,
        grid_spec=grid_spec,
        out_shape=jax.ShapeDtypeStruct((t, d), F32),
        compiler_params=_params(2),
        name="combine",
    )(off_flat, y2d, x1, cumx_t, sel_t, g_final)


def _rope_tables(s_len):
    rows = s_len // GRID_W
    r = jnp.repeat(jnp.arange(rows, dtype=F32), GRID_W)
    c = jnp.tile(jnp.arange(GRID_W, dtype=F32), rows)
    half = HEAD_DIM // 2
    inv = 1.0 / (ROPE_THETA ** (jnp.arange(0, half, 2, dtype=F32) / half))
    ang_r = r[:, None] * inv
    ang_c = c[:, None] * inv
    cos = jnp.concatenate([jnp.cos(ang_r), jnp.cos(ang_r), jnp.cos(ang_c), jnp.cos(ang_c)], axis=-1)
    sin = jnp.concatenate([-jnp.sin(ang_r), jnp.sin(ang_r), -jnp.sin(ang_c), jnp.sin(ang_c)], axis=-1)
    return cos, sin


def _t5_bucket_table():
    rel = (jnp.arange(3 * BLOCK) - BLOCK)[None, :] - jnp.arange(BLOCK)[:, None]
    half = NUM_BUCKETS // 2
    ret = jnp.where(rel > 0, half, 0)
    n = jnp.abs(rel)
    max_exact = half // 2
    nf = jnp.maximum(n, 1).astype(F32)
    large = max_exact + (jnp.log(nf / max_exact) / math.log(MAX_DISTANCE / max_exact)
                         * (half - max_exact)).astype(I32)
    large = jnp.minimum(large, half - 1)
    return (ret + jnp.where(n < max_exact, n, large)).astype(I32)


def kernel(x, g_mix, w_in, b_gate, qn_a, kn_a, w_proj_a, sink_b, rel_bias, w_proj_b, w_o, g_ffn, w_router,
           w_gate_e, w_up_e, w_down_e, g_final):
    bsz, s_len, d = x.shape
    depth = g_mix.shape[0]
    t = bsz * s_len
    cap = CAPACITY_FACTOR * s_len // N_EXPERTS
    n_r = s_len // LANES
    cos, sin = _rope_tables(s_len)
    bucket = _t5_bucket_table()
    x2d = x.reshape(t, d)
    for l in range(depth):
        proj = _inproj(x2d, g_mix[l][None, :], w_in[l].astype(BF16), tm=min(1024, t), tn=1792)
        oa = _attn_a(proj, cos, sin, qn_a[l], kn_a[l], bsz, s_len, d,
                     tq=min(256, s_len), tk=min(512, s_len))
        ob = _attn_b(proj, bucket, rel_bias, sink_b[l], bsz, s_len, d, tq=min(512, s_len))
        x1, h2p, aff = _outproj(oa, ob, proj, x2d, b_gate[l][None, :], w_proj_a[l].astype(BF16),
                                w_proj_b[l].astype(BF16), w_o[l].astype(BF16), g_ffn[l][None, :],
                                _split_router(w_router[l]), bsz, s_len, tm=512)
        idx, gates, cumx, sel = _topk(aff.reshape(bsz, N_EXPERTS, n_r, LANES), cap,
                                      rpt=h2p.shape[0] // t)
        y = _ffn(idx.reshape(-1), h2p, gates[..., None], w_gate_e[l], w_up_e[l], w_down_e[l],
                 bsz, s_len, cap, ch=min(256, cap // 2))
        off = jnp.concatenate([cumx[:, :, :, 0].astype(I32), jnp.full((bsz, N_EXPERTS, 1), cap, I32)], axis=2)
        off_flat = off.transpose(0, 2, 1).reshape(-1)
        x2d = _combine(off_flat, y.reshape(bsz * N_EXPERTS * cap, d), x1, cumx.transpose(0, 2, 1, 3),
                       sel.transpose(0, 2, 1, 3), g_final[None, :], bsz, s_len, cap, win=48,
                       final=(l == depth - 1))
    return x2d.reshape(bsz, s_len, d)
```

```python
import functools
import math

import jax
import jax.numpy as jnp
from jax import lax
from jax.experimental import pallas as pl
from jax.experimental.pallas import tpu as pltpu

F32 = jnp.float32
BF16 = jnp.bfloat16
I32 = jnp.int32
U32 = jnp.uint32

HEAD_DIM = 128
GROUP = 4
BLOCK = 128
GRID_W = 64
ROPE_THETA = 10000.0
NUM_BUCKETS = 32
MAX_DISTANCE = 128
N_EXPERTS = 16
CAPACITY_FACTOR = 2
EPS = 1e-6
NEG = -1e30
LANES = 128
SUBLANES = 8
BF16_ROWS = 16
ONES_ROWS = BF16_ROWS
SAFE_EXP2_RANGE = 100.0
ROW_CHAINS = 2
KV_UNROLL = 8
VMEM_LIMIT = 56 * 1024 * 1024
HIGHEST = lax.Precision.HIGHEST
NT = (((1,), (1,)), ((), ()))
TN = (((0,), (0,)), ((), ()))


def _params(n_axes):
    return pltpu.CompilerParams(dimension_semantics=("arbitrary",) * n_axes,
                                vmem_limit_bytes=VMEM_LIMIT)


def _inproj_kernel(x_ref, g_ref, w_ref, o_ref, h_ref):
    @pl.when(pl.program_id(1) == 0)
    def _():
        x = x_ref[...]
        ms = jnp.mean(x * x, axis=-1, keepdims=True)
        h_ref[...] = (x * lax.rsqrt(ms + EPS) * g_ref[...]).astype(BF16)

    o_ref[...] = jnp.dot(h_ref[...], w_ref[...], preferred_element_type=F32).astype(o_ref.dtype)


def _inproj(x2d, g, w, tm, tn):
    t, d = x2d.shape
    n = w.shape[1]
    return pl.pallas_call(
        _inproj_kernel,
        grid=(t // tm, n // tn),
        in_specs=[pl.BlockSpec((tm, d), lambda i, j: (i, 0)),
                  pl.BlockSpec((1, d), lambda i, j: (0, 0)),
                  pl.BlockSpec((d, tn), lambda i, j: (0, j))],
        out_specs=pl.BlockSpec((tm, tn), lambda i, j: (i, j)),
        out_shape=jax.ShapeDtypeStruct((t, n), BF16),
        scratch_shapes=[pltpu.VMEM((tm, d), BF16)],
        compiler_params=_params(2),
        name="inproj",
    )(x2d, g, w)


def _norm_rope(xb, gain, gain_sw, cos, sin_signed):
    a = lax.broadcasted_iota(I32, (HEAD_DIM, HEAD_DIM), 0)
    c = lax.broadcasted_iota(I32, (HEAD_DIM, HEAD_DIM), 1)
    swap_mat = jnp.where(a == (c ^ 32), 1.0, 0.0).astype(BF16)
    ones_mat = jnp.ones((HEAD_DIM, HEAD_DIM), BF16)
    x = xb.astype(F32)
    sq = x * x
    sq_hi = sq.astype(BF16)
    sq_lo = (sq - sq_hi.astype(F32)).astype(BF16)
    ms = (jnp.dot(sq_hi, ones_mat, preferred_element_type=F32)
          + jnp.dot(sq_lo, ones_mat, preferred_element_type=F32)) * (1.0 / HEAD_DIM)
    swapped = jnp.dot(xb, swap_mat, preferred_element_type=F32)
    return lax.rsqrt(ms + EPS) * (x * (gain * cos) + swapped * (gain_sw * sin_signed))


def _attn_a_kernel(q_ref, k_ref, v_ref, cq_ref, sq_ref, ck_ref, sk_ref, qn_ref, kn_ref, o_ref,
                   kp_ref, vt_ref, qt_ref, ksq_ref, m_ref, acc_ref, st0_ref, st1_ref, pt0_ref, pt1_ref,
                   *, tq, tk):
    s_len = k_ref.shape[0]
    n_chunks = s_len // tk

    @pl.when(pl.program_id(2) == 0)
    def _():
        ksq_ref[...] = jnp.zeros(ksq_ref.shape, F32)

        def body(c, carry):
            off = pl.multiple_of(c * tk, tk)
            kb = _norm_rope(k_ref[pl.ds(off, tk), :], kn_ref[0:1, :], kn_ref[1:2, :],
                            ck_ref[pl.ds(off, tk), :], sk_ref[pl.ds(off, tk), :]).astype(BF16)
            kp_ref[pl.ds(off, tk), :] = kb
            kf = kb.astype(F32)
            ksq_ref[...] = jnp.maximum(ksq_ref[...], jnp.max(jnp.sum(kf * kf, axis=1, keepdims=True)))
            vt_ref[c, :HEAD_DIM, :] = v_ref[pl.ds(off, tk), :].astype(F32).T.astype(BF16)
            vt_ref[c, HEAD_DIM:, :] = jnp.ones((ONES_ROWS, tk), BF16)
            return carry
        lax.fori_loop(0, n_chunks, body, 0)

    scale = math.log2(math.e) / math.sqrt(HEAD_DIM)
    qsq = jnp.zeros((1, tq), F32)
    for g in range(GROUP):
        qg = _norm_rope(q_ref[:, g * HEAD_DIM:(g + 1) * HEAD_DIM], qn_ref[0:1, :], qn_ref[1:2, :],
                        cq_ref[...], sq_ref[...]) * scale
        qb = qg.T.astype(BF16)
        qt_ref[g] = qb
        qf = qb.astype(F32)
        qsq = jnp.maximum(qsq, jnp.sum(qf * qf, axis=0, keepdims=True))
    bound_sq = jnp.max(qsq) * jnp.max(ksq_ref[...])
    no_max_needed = bound_sq <= SAFE_EXP2_RANGE * SAFE_EXP2_RANGE

    acc_ref[...] = jnp.zeros(acc_ref.shape, F32)
    unroll = math.gcd(KV_UNROLL, n_chunks)

    @pl.when(no_max_needed)
    def _():
        m_ref[...] = jnp.zeros(m_ref.shape, F32)

        def probs(c, dst_ref, counts):
            kc = kp_ref[pl.ds(pl.multiple_of(c * tk, tk), tk), :]
            for g in range(GROUP):
                p = jnp.exp2(jnp.dot(kc, qt_ref[g], preferred_element_type=F32))
                m_ref[g] += jnp.sum(p, axis=0, keepdims=True) * counts
                dst_ref[g] = p.astype(BF16)

        def weighted_sum(c, src_ref):
            vt = vt_ref[c, :HEAD_DIM, :]
            for g in range(GROUP):
                acc_ref[g, :HEAD_DIM, :] += jnp.dot(vt, src_ref[g], preferred_element_type=F32)

        probs(0, pt0_ref, 1.0)

        def kv_body(j, carry):
            bufs = (pt0_ref, pt1_ref)
            for u in range(unroll):
                c = unroll * j + u
                probs(jnp.minimum(c + 1, n_chunks - 1), bufs[(u + 1) % 2],
                      jnp.where(c + 1 < n_chunks, 1.0, 0.0))
                weighted_sum(c, bufs[u % 2])
            return carry
        lax.fori_loop(0, n_chunks // unroll, kv_body, 0)
        for g in range(GROUP):
            acc_ref[g, HEAD_DIM:HEAD_DIM + 1, :] = m_ref[g]

    @pl.when(jnp.logical_not(no_max_needed))
    def _():
        m_ref[...] = jnp.full(m_ref.shape, -jnp.inf, F32)

        def scores(c, dst_ref):
            kc = kp_ref[pl.ds(pl.multiple_of(c * tk, tk), tk), :]
            for g in range(GROUP):
                dst_ref[g] = jnp.dot(kc, qt_ref[g], preferred_element_type=F32)

        def softmax_pv(c, src_ref):
            vt = vt_ref[c]
            for g in range(GROUP):
                st = src_ref[g]
                m_prev = m_ref[g]
                m_new = jnp.maximum(m_prev, jnp.max(st, axis=0, keepdims=True))
                pt = jnp.exp2(st - m_new).astype(BF16)
                acc_ref[g] = (jnp.exp2(m_prev - m_new) * acc_ref[g]
                              + jnp.dot(vt, pt, preferred_element_type=F32))
                m_ref[g] = m_new

        scores(0, st0_ref)

        def kv_body(j, carry):
            c = 2 * j
            scores(c + 1, st1_ref)
            softmax_pv(c, st0_ref)
            scores(jnp.minimum(c + 2, n_chunks - 1), st0_ref)
            softmax_pv(c + 1, st1_ref)
            return carry
        lax.fori_loop(0, n_chunks // 2, kv_body, 0)

    for g in range(GROUP):
        acc = acc_ref[g]
        o = (acc[:HEAD_DIM] / acc[HEAD_DIM:HEAD_DIM + 1]).T
        o_ref[:, g * HEAD_DIM:(g + 1) * HEAD_DIM] = o.astype(o_ref.dtype)


def _attn_a(proj, cos, sin, qn, kn, bsz, s_len, d, tq, tk):
    q_w = d // 2
    kv_w = d // 8
    kvh = kv_w // HEAD_DIM
    gw = GROUP * HEAD_DIM
    nq = s_len // tq
    k_blk = q_w // HEAD_DIM
    v_blk = (q_w + kv_w) // HEAD_DIM
    assert (s_len // tk) % 2 == 0

    def with_swapped(gain):
        return jnp.stack([gain, gain.reshape(2, 2, 32)[:, ::-1, :].reshape(HEAD_DIM)])

    qn, kn = with_swapped(qn), with_swapped(kn)
    kern = functools.partial(_attn_a_kernel, tq=tq, tk=tk)
    return pl.pallas_call(
        kern,
        grid=(bsz, kvh, nq),
        in_specs=[pl.BlockSpec((tq, gw), lambda b, k, i: (b * nq + i, k)),
                  pl.BlockSpec((s_len, HEAD_DIM), lambda b, k, i: (b, k_blk + k)),
                  pl.BlockSpec((s_len, HEAD_DIM), lambda b, k, i: (b, v_blk + k)),
                  pl.BlockSpec((tq, HEAD_DIM), lambda b, k, i: (i, 0)),
                  pl.BlockSpec((tq, HEAD_DIM), lambda b, k, i: (i, 0)),
                  pl.BlockSpec((s_len, HEAD_DIM), lambda b, k, i: (0, 0)),
                  pl.BlockSpec((s_len, HEAD_DIM), lambda b, k, i: (0, 0)),
                  pl.BlockSpec((2, HEAD_DIM), lambda b, k, i: (0, 0)),
                  pl.BlockSpec((2, HEAD_DIM), lambda b, k, i: (0, 0))],
        out_specs=pl.BlockSpec((tq, gw), lambda b, k, i: (b * nq + i, k)),
        out_shape=jax.ShapeDtypeStruct((bsz * s_len, q_w), BF16),
        scratch_shapes=[pltpu.VMEM((s_len, HEAD_DIM), BF16),
                        pltpu.VMEM((s_len // tk, HEAD_DIM + ONES_ROWS, tk), BF16),
                        pltpu.VMEM((GROUP, HEAD_DIM, tq), BF16),
                        pltpu.VMEM((1, LANES), F32),
                        pltpu.VMEM((GROUP, 1, tq), F32),
                        pltpu.VMEM((GROUP, HEAD_DIM + ONES_ROWS, tq), F32),
                        pltpu.VMEM((GROUP, tk, tq), F32),
                        pltpu.VMEM((GROUP, tk, tq), F32),
                        pltpu.VMEM((GROUP, tk, tq), BF16),
                        pltpu.VMEM((GROUP, tk, tq), BF16)],
        compiler_params=_params(3),
        name="attn_a",
    )(proj, proj, proj, cos, sin, cos, sin, qn, kn)


def _attn_b_kernel(relb_ref, sink_ref, bucket_ref, q_ref, kp_ref, kc_ref, kn_ref, vp_ref, vc_ref, vn_ref,
                   o_ref, bias_ref, st_ref, *, tq, s_len, kvh):
    b = pl.program_id(0)
    k = pl.program_id(1)
    i = pl.program_id(2)
    band = 3 * BLOCK
    log2e = math.log2(math.e)

    @pl.when((b == 0) & (k == 0) & (i == 0))
    def _():
        bucket_t = bucket_ref[...]
        key = lax.broadcasted_iota(I32, (band, BLOCK), 0)
        qry = lax.broadcasted_iota(I32, (band, BLOCK), 1)
        in_window = jnp.abs(key - BLOCK - qry) <= BLOCK
        for kk in range(kvh):
            for g in range(GROUP):
                h = kk * GROUP + g
                tab = jnp.zeros((band, BLOCK), F32)
                for bkt in range(NUM_BUCKETS):
                    tab = jnp.where(bucket_t == bkt, relb_ref[bkt, h], tab)
                tab = jnp.where(in_window, tab * log2e, NEG)
                cols = slice(g * BLOCK, (g + 1) * BLOCK)
                bias_ref[0, kk, :, cols] = tab
                bias_ref[1, kk, :, cols] = jnp.where(key >= BLOCK, tab, NEG)
                bias_ref[2, kk, :, cols] = jnp.where(key < 2 * BLOCK, tab, NEG)

    nsub = tq // BLOCK
    kcat = jnp.concatenate([kp_ref[...], kc_ref[...], kn_ref[...]], axis=0)
    vcat = jnp.concatenate([vp_ref[...], vc_ref[...], vn_ref[...]], axis=0)
    vt = jnp.concatenate([vcat.astype(F32).T.astype(BF16), jnp.ones((ONES_ROWS, vcat.shape[0]), BF16)], axis=0)
    sink = jnp.concatenate([jnp.full((1, BLOCK), sink_ref[k * GROUP + g], F32) for g in range(GROUP)],
                           axis=1) * log2e
    scale = log2e / math.sqrt(HEAD_DIM)
    for jb in range(nsub):
        rows = slice(jb * BLOCK, (jb + 1) * BLOCK)
        qt = jnp.concatenate([q_ref[rows, g * HEAD_DIM:(g + 1) * HEAD_DIM].astype(F32).T.astype(BF16)
                              for g in range(GROUP)], axis=1)
        if jb == 0:
            variant = jnp.where(i == 0, 1, 0)
        elif jb == nsub - 1:
            variant = jnp.where(i == pl.num_programs(2) - 1, 2, 0)
        else:
            variant = 0
        st_ref[jb] = (jnp.dot(kcat[jb * BLOCK:jb * BLOCK + band], qt, preferred_element_type=F32) * scale
                      + bias_ref[variant, k])
    for jb in range(nsub):
        rows = slice(jb * BLOCK, (jb + 1) * BLOCK)
        st = st_ref[jb]
        m = jnp.maximum(jnp.max(st, axis=0, keepdims=True), sink)
        e = jnp.exp2(st - m).astype(BF16)
        acc = jnp.dot(vt[:, jb * BLOCK:jb * BLOCK + band], e, preferred_element_type=F32)
        o = acc[:HEAD_DIM] / (acc[HEAD_DIM:HEAD_DIM + 1] + jnp.exp2(sink - m))
        for g in range(GROUP):
            o_ref[rows, g * HEAD_DIM:(g + 1) * HEAD_DIM] = o[:, g * BLOCK:(g + 1) * BLOCK].T.astype(o_ref.dtype)


def _attn_b(proj, bucket, rel_bias, sink, bsz, s_len, d, tq):
    q_w = d // 2
    kv_w = d // 8
    kvh = kv_w // HEAD_DIM
    gw = GROUP * HEAD_DIM
    nq = s_len // tq
    sub = tq // BLOCK
    nblk = s_len // BLOCK
    q_blk = (q_w + 2 * kv_w) // gw
    k_blk = (2 * q_w + 2 * kv_w) // HEAD_DIM
    v_blk = (2 * q_w + 3 * kv_w) // HEAD_DIM

    def prev_map(col):
        return lambda b, k, i: (b * nblk + jnp.maximum(i * sub - 1, 0), col + k)

    def cur_map(col):
        return lambda b, k, i: (b * nq + i, col + k)

    def next_map(col):
        return lambda b, k, i: (b * nblk + jnp.minimum((i + 1) * sub, nblk - 1), col + k)

    small = (BLOCK, HEAD_DIM)
    assert sub >= 2
    kern = functools.partial(_attn_b_kernel, tq=tq, s_len=s_len, kvh=kvh)
    return pl.pallas_call(
        kern,
        grid=(bsz, kvh, nq),
        in_specs=[pl.BlockSpec(memory_space=pltpu.SMEM),
                  pl.BlockSpec(memory_space=pltpu.SMEM),
                  pl.BlockSpec((3 * BLOCK, BLOCK), lambda b, k, i: (0, 0)),
                  pl.BlockSpec((tq, gw), lambda b, k, i: (b * nq + i, q_blk + k)),
                  pl.BlockSpec(small, prev_map(k_blk)),
                  pl.BlockSpec((tq, HEAD_DIM), cur_map(k_blk)),
                  pl.BlockSpec(small, next_map(k_blk)),
                  pl.BlockSpec(small, prev_map(v_blk)),
                  pl.BlockSpec((tq, HEAD_DIM), cur_map(v_blk)),
                  pl.BlockSpec(small, next_map(v_blk))],
        out_specs=pl.BlockSpec((tq, gw), lambda b, k, i: (b * nq + i, k)),
        out_shape=jax.ShapeDtypeStruct((bsz * s_len, q_w), BF16),
        scratch_shapes=[pltpu.VMEM((3, kvh, 3 * BLOCK, GROUP * BLOCK), F32),
                        pltpu.VMEM((sub, 3 * BLOCK, GROUP * BLOCK), F32)],
        compiler_params=_params(3),
        name="attn_b",
    )(rel_bias, sink, bucket.T, proj, proj, proj, proj, proj, proj, proj)


def _outproj_kernel(oa_ref, ob_ref, ga0_ref, ga1_ref, gb0_ref, gb1_ref, x_ref, bg_ref, wpa_ref, wpb_ref,
                    wo_ref, gf_ref, wr_ref, x1_ref, h2p_ref, aff_ref):
    tm, d = x_ref.shape
    half = d // 2
    rpt = half // LANES
    rn = tm // ROW_CHAINS
    for rc in range(ROW_CHAINS):
        rows = slice(rc * rn, (rc + 1) * rn)
        oa = oa_ref[rows, :]
        ob = ob_ref[rows, :]
        parts = []
        for c, (ga_ref, gb_ref) in enumerate(((ga0_ref, gb0_ref), (ga1_ref, gb1_ref))):
            cols = slice(c * half, (c + 1) * half)
            pa = jnp.dot(oa, wpa_ref[:, cols], preferred_element_type=F32)
            pb = jnp.dot(ob, wpb_ref[:, cols], preferred_element_type=F32)
            gate_a = jax.nn.sigmoid(ga_ref[rows, :].astype(F32) + bg_ref[:, cols])
            gate_b = jax.nn.sigmoid(gb_ref[rows, :].astype(F32) + bg_ref[:, d + c * half:d + (c + 1) * half])
            parts.append((gate_a * pa + gate_b * pb).astype(BF16))
        x1 = x_ref[rows, :] + (jnp.dot(parts[0], wo_ref[:half, :], preferred_element_type=F32)
                               + jnp.dot(parts[1], wo_ref[half:, :], preferred_element_type=F32))
        x1_ref[rows, :] = x1
        ms = jnp.mean(x1 * x1, axis=-1, keepdims=True)
        h = x1 * lax.rsqrt(ms + EPS) * gf_ref[...]
        h_hi = h.astype(BF16)
        bits = pltpu.bitcast(h_hi.astype(F32), U32)
        packed = (bits[:, :half] >> 16) | (bits[:, half:] & jnp.uint32(0xFFFF0000))
        for s in range(rpt):
            h2p_ref[pl.ds(rc * rn * rpt + s, rn, stride=rpt), :] = packed[:, s * LANES:(s + 1) * LANES]
        h_lo = (h - h_hi.astype(F32)).astype(BF16)
        prod = (jnp.dot(h_hi, wr_ref[...], preferred_element_type=F32)
                + jnp.dot(h_lo, wr_ref[...], preferred_element_type=F32))
        logits = (prod[:, :LANES] + prod[:, LANES:]).T[:N_EXPERTS, :]
        ex = jnp.exp(logits - jnp.max(logits, axis=0, keepdims=True))
        aff_ref[:, rows] = ex / jnp.sum(ex, axis=0, keepdims=True)


def _split_router(w):
    w_hi = w.astype(BF16)
    w_lo = (w - w_hi.astype(F32)).astype(BF16)
    pad = ((0, 0), (0, LANES - w.shape[1]))
    return jnp.concatenate([jnp.pad(w_hi, pad), jnp.pad(w_lo, pad)], axis=1)


def _outproj(oa, ob, proj, x2d, b_gate, wpa, wpb, wo, g_ffn, wr_split, bsz, s_len, tm):
    t, d = x2d.shape
    half = d // 2
    nt = s_len // tm
    row = lambda c: (lambda i: (i, c))
    const = lambda i: (0, 0)
    return pl.pallas_call(
        _outproj_kernel,
        grid=(t // tm,),
        in_specs=[pl.BlockSpec((tm, half), row(0)),
                  pl.BlockSpec((tm, half), row(0)),
                  pl.BlockSpec((tm, half), row(3)),
                  pl.BlockSpec((tm, half), row(4)),
                  pl.BlockSpec((tm, half), row(5)),
                  pl.BlockSpec((tm, half), row(6)),
                  pl.BlockSpec((tm, d), row(0)),
                  pl.BlockSpec((1, 2 * d), const),
                  pl.BlockSpec((half, d), const, pipeline_mode=pl.Buffered(1)),
                  pl.BlockSpec((half, d), const, pipeline_mode=pl.Buffered(1)),
                  pl.BlockSpec((d, d), const, pipeline_mode=pl.Buffered(1)),
                  pl.BlockSpec((1, d), const),
                  pl.BlockSpec((d, 2 * LANES), const, pipeline_mode=pl.Buffered(1))],
        out_specs=[pl.BlockSpec((tm, d), row(0)),
                   pl.BlockSpec((tm * (half // LANES), LANES), row(0)),
                   pl.BlockSpec((None, N_EXPERTS, tm), lambda i: (i // nt, 0, i % nt))],
        out_shape=[jax.ShapeDtypeStruct((t, d), F32),
                   jax.ShapeDtypeStruct((t * (half // LANES), LANES), U32),
                   jax.ShapeDtypeStruct((bsz, N_EXPERTS, s_len), F32)],
        compiler_params=_params(1),
        name="outproj",
    )(oa, ob, proj, proj, proj, proj, x2d, b_gate, wpa, wpb, wo, g_ffn, wr_split)


def _topk_kernel(a_ref, idx_ref, gate_ref, rank_ref, sel_ref, *, cap, rpt):
    a = a_ref[...]
    n_e, n_r, _ = a.shape
    bits = pltpu.bitcast(a, I32)

    def total(x):
        return jnp.sum(jnp.sum(x, axis=1, keepdims=True), axis=2, keepdims=True)

    def search(_, carry):
        lo, hi = carry
        mid = lo + ((hi - lo) >> 1)
        enough = total(jnp.where(bits >= mid, 1.0, 0.0)) >= cap
        return jnp.where(enough, mid, lo), jnp.where(enough, hi, mid)

    lo0 = jnp.zeros((n_e, 1, 1), I32)
    hi0 = jnp.full((n_e, 1, 1), 0x7F800000, I32)
    thr, _ = lax.fori_loop(0, 31, search, (lo0, hi0))

    ri = lax.broadcasted_iota(I32, (LANES, LANES), 0)
    ci = lax.broadcasted_iota(I32, (LANES, LANES), 1)
    upper = jnp.where(ri <= ci, 1.0, 0.0).astype(BF16)
    ones = jnp.ones((LANES, LANES), BF16)
    rr = lax.broadcasted_iota(I32, (n_r, n_r), 0)
    rc = lax.broadcasted_iota(I32, (n_r, n_r), 1)
    strict_lower = jnp.where(rc < rr, 1.0, 0.0).astype(BF16)

    def prefix(x):
        x2 = x.reshape(n_e * n_r, LANES).astype(BF16)
        in_row = jnp.dot(x2, upper, preferred_element_type=F32).reshape(n_e, n_r, LANES)
        row_tot = jnp.dot(x2, ones, preferred_element_type=F32).reshape(n_e, n_r, LANES)
        before = jnp.stack([jnp.dot(strict_lower, row_tot[e].astype(BF16), preferred_element_type=F32)
                            for e in range(n_e)], axis=0)
        return in_row + before

    above = jnp.where(bits > thr, 1.0, 0.0)
    tied = jnp.where(bits == thr, 1.0, 0.0)
    need = cap - total(above)
    sel = above + tied * jnp.where(prefix(tied) <= need, 1.0, 0.0)
    cum = prefix(sel)
    rank = (cum - sel).astype(I32)
    for e in range(n_e):
        sel_ref[pl.ds(e, n_r, stride=n_e), :] = sel[e]
        rank_ref[pl.ds(e, n_r, stride=n_e), :] = rank[e]

    slot = lax.broadcasted_iota(I32, (1, cap), 1).astype(F32)
    row_id = lax.broadcasted_iota(I32, (n_r, 1), 0).astype(F32)
    lane_id = lax.broadcasted_iota(I32, (LANES, 1), 0).astype(F32)
    for e in range(n_e):
        c = cum[e]
        row_end = c[:, LANES - 1:LANES]
        srow = jnp.sum(jnp.where(row_end <= slot, 1.0, 0.0), axis=0, keepdims=True)
        pick = jnp.where(row_id == srow, 1.0, 0.0)
        c_row = lax.dot_general(c, pick, TN, precision=HIGHEST, preferred_element_type=F32)
        slane = jnp.sum(jnp.where(c_row <= slot, 1.0, 0.0), axis=0, keepdims=True)
        a_row = lax.dot_general(a[e], pick, TN, precision=HIGHEST, preferred_element_type=F32)
        token = (srow * LANES + slane).astype(I32) + pl.program_id(0) * (n_r * LANES)
        idx_ref[e:e + 1, :] = token * rpt
        gate_ref[e:e + 1, :] = jnp.sum(jnp.where(lane_id == slane, a_row, 0.0), axis=0, keepdims=True)


def _topk(aff4, cap, rpt):
    bsz, n_e, n_r, _ = aff4.shape
    blk4 = pl.BlockSpec((None, n_e, n_r, LANES), lambda b: (b, 0, 0, 0))
    blk3 = pl.BlockSpec((None, n_e, cap), lambda b: (b, 0, 0))
    tiles = pl.BlockSpec((None, n_r * n_e, LANES), lambda b: (b, 0, 0))
    return pl.pallas_call(
        functools.partial(_topk_kernel, cap=cap, rpt=rpt),
        grid=(bsz,),
        in_specs=[blk4],
        out_specs=[blk3, blk3, tiles, tiles],
        out_shape=[jax.ShapeDtypeStruct((bsz, n_e, cap), I32),
                   jax.ShapeDtypeStruct((bsz, n_e, cap), F32),
                   jax.ShapeDtypeStruct((bsz, n_r * n_e, LANES), I32),
                   jax.ShapeDtypeStruct((bsz, n_r * n_e, LANES), F32)],
        compiler_params=_params(1),
        name="topk",
    )(aff4)


def _ffn_kernel(idx_ref, h2_hbm, wg_hbm, wu_hbm, wd_hbm, gate_ref, y_ref, xa, xb, wg_ref, wu_ref, wd_ref,
                stg_g, stg_u, stg_d, sem, wsem, *, bsz, s_len, cap, ch, rpt, unroll):
    n_c = cap // ch
    bufs = (xa, xb)
    e = pl.program_id(0)
    b = pl.program_id(1)
    pair = e * bsz + b
    n_pairs = pl.num_programs(0) * bsz
    slot = e % 2
    rows_gu = stg_g.shape[0]
    rows_d = stg_d.shape[0]

    def weight_copies(ee, part):
        r_gu = pl.ds(pl.multiple_of(part * rows_gu, rows_gu), rows_gu)
        r_d = pl.ds(pl.multiple_of(part * rows_d, rows_d), rows_d)
        return (pltpu.make_async_copy(wg_hbm.at[ee, r_gu, :], stg_g, wsem.at[0]),
                pltpu.make_async_copy(wu_hbm.at[ee, r_gu, :], stg_u, wsem.at[1]),
                pltpu.make_async_copy(wd_hbm.at[ee, r_d, :], stg_d, wsem.at[2]))

    def convert(to_slot, part):
        r_gu = pl.ds(pl.multiple_of(part * rows_gu, rows_gu), rows_gu)
        r_d = pl.ds(pl.multiple_of(part * rows_d, rows_d), rows_d)
        wg_ref[to_slot, r_gu, :] = stg_g[...].astype(BF16)
        wu_ref[to_slot, r_gu, :] = stg_u[...].astype(BF16)
        wd_ref[to_slot, r_d, :] = stg_d[...].astype(BF16)

    @pl.when(pair == 0)
    def _():
        for part in range(bsz):
            copies = weight_copies(0, part)
            for cp in copies:
                cp.start()
            for cp in copies:
                cp.wait()
            convert(0, part)

    @pl.when(e + 1 < pl.num_programs(0))
    def _():
        for cp in weight_copies(e + 1, b):
            cp.start()

    def issue(pair_k, c):
        bb = pair_k % bsz
        ee = pair_k // bsz
        idx_base = (bb * N_EXPERTS + ee) * cap + c * ch
        buf = bufs[c % 2]

        def body(j, carry):
            for u in range(unroll):
                r = j * unroll + u
                src = pl.multiple_of(idx_ref[idx_base + r], rpt)
                dst = pl.multiple_of(r * rpt, rpt)
                pltpu.make_async_copy(h2_hbm.at[pl.ds(src, rpt), :], buf.at[pl.ds(dst, rpt), :],
                                      sem.at[c % 2]).start()
            return carry
        lax.fori_loop(0, ch // unroll, body, 0)

    @pl.when(pair == 0)
    def _():
        issue(pair, 0)

    for c in range(n_c):
        if c + 1 < n_c:
            issue(pair, c + 1)
        else:
            @pl.when(pair + 1 < n_pairs)
            def _():
                issue(pair + 1, 0)
        buf = bufs[c % 2]
        pltpu.make_async_copy(h2_hbm.at[pl.ds(0, ch * rpt), :], buf, sem.at[c % 2]).wait()
        lo, hi = [], []
        for s in range(rpt):
            w = buf[pl.ds(s, ch, stride=rpt), :]
            lo.append(pltpu.bitcast(w << 16, F32).astype(BF16))
            hi.append(pltpu.bitcast(w & jnp.uint32(0xFFFF0000), F32).astype(BF16))
        x = jnp.concatenate(lo + hi, axis=1)
        a = jnp.dot(x, wg_ref[slot], preferred_element_type=F32)
        u = jnp.dot(x, wu_ref[slot], preferred_element_type=F32)
        act = (jax.nn.silu(a) * u).astype(BF16)
        rows = slice(c * ch, (c + 1) * ch)
        y = jnp.dot(act, wd_ref[slot], preferred_element_type=F32) * gate_ref[rows, :]
        y_ref[rows, :] = y.astype(y_ref.dtype)

    @pl.when(e + 1 < pl.num_programs(0))
    def _():
        for cp in weight_copies(e + 1, b):
            cp.wait()
        convert(1 - slot, b)


def _ffn(idx_flat, h2p, gates4, wg, wu, wd, bsz, s_len, cap, ch):
    n_e, d, f = wg.shape
    rpt = h2p.shape[0] // (bsz * s_len)
    assert (cap // ch) % 2 == 0 and d % bsz == 0 and f % bsz == 0
    kern = functools.partial(_ffn_kernel, bsz=bsz, s_len=s_len, cap=cap, ch=ch, rpt=rpt, unroll=8)
    grid_spec = pltpu.PrefetchScalarGridSpec(
        num_scalar_prefetch=1,
        grid=(n_e, bsz),
        in_specs=[pl.BlockSpec(memory_space=pl.ANY),
                  pl.BlockSpec(memory_space=pl.ANY),
                  pl.BlockSpec(memory_space=pl.ANY),
                  pl.BlockSpec(memory_space=pl.ANY),
                  pl.BlockSpec((None, None, cap, 1), lambda e, b, idx: (b, e, 0, 0))],
        out_specs=pl.BlockSpec((None, None, cap, d), lambda e, b, idx: (b, e, 0, 0)),
        scratch_shapes=[pltpu.VMEM((ch * rpt, LANES), U32),
                        pltpu.VMEM((ch * rpt, LANES), U32),
                        pltpu.VMEM((2, d, f), BF16),
                        pltpu.VMEM((2, d, f), BF16),
                        pltpu.VMEM((2, f, d), BF16),
                        pltpu.VMEM((d // bsz, f), F32),
                        pltpu.VMEM((d // bsz, f), F32),
                        pltpu.VMEM((f // bsz, d), F32),
                        pltpu.SemaphoreType.DMA((2,)),
                        pltpu.SemaphoreType.DMA((3,))],
    )
    return pl.pallas_call(
        kern,
        grid_spec=grid_spec,
        out_shape=jax.ShapeDtypeStruct((bsz, n_e, cap, d), BF16),
        compiler_params=_params(2),
        name="ffn",
    )(idx_flat, h2p, wg, wu, wd, gates4)


def _combine_kernel(src0_ref, first0_ref, rounds_ref, y_hbm, x1_ref, rank_ref, sel_ref, gf_ref, o_ref,
                    ybuf, yextra, sem, *, cap, win, n_r, final):
    b = pl.program_id(0)
    n = b * n_r + pl.program_id(1)
    n_tiles = pl.num_programs(0) * n_r
    slot = n % 2
    rows_all = N_EXPERTS * win

    def start_copies(srcs, dst_ref, sem_k):
        for e, src in enumerate(srcs):
            pltpu.make_async_copy(y_hbm.at[pl.ds(pl.multiple_of(src, BF16_ROWS), win), :],
                                  dst_ref.at[pl.ds(e * win, win), :], sem_k).start()

    def wait_copies(dst_ref, sem_k):
        pltpu.make_async_copy(y_hbm.at[pl.ds(0, rows_all), :], dst_ref, sem_k).wait()

    jcol = lax.broadcasted_iota(I32, (win, 1), 0)

    def expand(firsts, begins, rows_bf16):
        blocks = []
        for e in range(N_EXPERTS):
            rank = rank_ref[e:e + 1, :]
            blocks.append(jnp.where(rank == jcol + begins[e],
                                    jnp.where(rank >= firsts[e], sel_ref[e:e + 1, :], 0.0), 0.0))
        onehot = jnp.concatenate(blocks, axis=0).astype(BF16)
        return lax.dot_general(onehot, rows_bf16, TN, preferred_element_type=F32)

    @pl.when(n == 0)
    def _():
        start_copies([src0_ref[e] for e in range(N_EXPERTS)], ybuf.at[slot], sem.at[slot])

    @pl.when(n + 1 < n_tiles)
    def _():
        start_copies([src0_ref[(n + 1) * N_EXPERTS + e] for e in range(N_EXPERTS)],
                     ybuf.at[1 - slot], sem.at[1 - slot])

    bases = [(b * N_EXPERTS + e) * cap for e in range(N_EXPERTS)]
    firsts = [first0_ref[n * N_EXPERTS + e] for e in range(N_EXPERTS)]
    begins = [src0_ref[n * N_EXPERTS + e] - bases[e] for e in range(N_EXPERTS)]

    wait_copies(ybuf.at[slot], sem.at[slot])
    o_ref[...] = x1_ref[...] + expand(firsts, begins, ybuf[slot])

    def extra_round(k, carry):
        firsts_k = [f + k * win for f in firsts]
        begins_k = [jnp.minimum(f, cap - win) for f in firsts_k]
        start_copies([bases[e] + begins_k[e] for e in range(N_EXPERTS)], yextra, sem.at[2])
        wait_copies(yextra, sem.at[2])
        o_ref[...] += expand(firsts_k, begins_k, yextra[...])
        return carry
    lax.fori_loop(1, rounds_ref[n], extra_round, 0)

    if final:
        x2 = o_ref[...]
        ms = jnp.mean(x2 * x2, axis=-1, keepdims=True)
        o_ref[...] = x2 * lax.rsqrt(ms + EPS) * gf_ref[...]


def _combine(y2d, x1, rank_t, sel_t, g_final, bsz, s_len, cap, win, final):
    t, d = x1.shape
    n_r = s_len // LANES
    off = rank_t[:, :, :, 0]
    cnt = jnp.concatenate([off[:, 1:], jnp.full((bsz, 1, N_EXPERTS), cap, I32)], axis=1) - off
    first0 = (off // BF16_ROWS) * BF16_ROWS
    base = (jnp.arange(bsz, dtype=I32)[:, None, None] * N_EXPERTS
            + jnp.arange(N_EXPERTS, dtype=I32)[None, None, :]) * cap
    src0 = base + jnp.minimum(first0, cap - win)
    rounds = jnp.max(jnp.where(cnt > 0, (off - first0 + cnt + win - 1) // win, 0), axis=2)

    kern = functools.partial(_combine_kernel, cap=cap, win=win, n_r=n_r, final=final)
    meta = pl.BlockSpec((None, None, N_EXPERTS, LANES), lambda b, r, *_: (b, r, 0, 0))
    grid_spec = pltpu.PrefetchScalarGridSpec(
        num_scalar_prefetch=3,
        grid=(bsz, n_r),
        in_specs=[pl.BlockSpec(memory_space=pl.ANY),
                  pl.BlockSpec((LANES, d), lambda b, r, *_: (b * n_r + r, 0)),
                  meta, meta,
                  pl.BlockSpec((1, d), lambda b, r, *_: (0, 0))],
        out_specs=pl.BlockSpec((LANES, d), lambda b, r, *_: (b * n_r + r, 0)),
        scratch_shapes=[pltpu.VMEM((2, N_EXPERTS * win, d), BF16),
                        pltpu.VMEM((N_EXPERTS * win, d), BF16),
                        pltpu.SemaphoreType.DMA((3,))],
    )
    return pl.pallas_call(
        kern,
        grid_spec=grid_spec,
        out_shape=jax.ShapeDtypeStruct((t, d), F32),
        compiler_params=_params(2),
        name="combine",
    )(src0.reshape(-1), first0.reshape(-1), rounds.reshape(-1), y2d, x1, rank_t, sel_t, g_final)


def _rope_tables(s_len):
    rows = s_len // GRID_W
    half = HEAD_DIM // 2
    inv = 1.0 / (ROPE_THETA ** (jnp.arange(0, half, 2, dtype=F32) / half))
    ang_r = jnp.arange(rows, dtype=F32)[:, None] * inv
    ang_c = jnp.arange(GRID_W, dtype=F32)[:, None] * inv
    by_row = lambda a: jnp.repeat(a, GRID_W, axis=0)
    by_col = lambda a: jnp.tile(a, (rows, 1))
    cos_r, sin_r = by_row(jnp.cos(ang_r)), by_row(jnp.sin(ang_r))
    cos_c, sin_c = by_col(jnp.cos(ang_c)), by_col(jnp.sin(ang_c))
    cos = jnp.concatenate([cos_r, cos_r, cos_c, cos_c], axis=-1)
    sin = jnp.concatenate([-sin_r, sin_r, -sin_c, sin_c], axis=-1)
    return cos, sin


def _t5_bucket_table():
    rel = (jnp.arange(3 * BLOCK) - BLOCK)[None, :] - jnp.arange(BLOCK)[:, None]
    half = NUM_BUCKETS // 2
    ret = jnp.where(rel > 0, half, 0)
    n = jnp.abs(rel)
    max_exact = half // 2
    nf = jnp.maximum(n, 1).astype(F32)
    large = max_exact + (jnp.log(nf / max_exact) / math.log(MAX_DISTANCE / max_exact)
                         * (half - max_exact)).astype(I32)
    large = jnp.minimum(large, half - 1)
    return (ret + jnp.where(n < max_exact, n, large)).astype(I32)


def kernel(x, g_mix, w_in, b_gate, qn_a, kn_a, w_proj_a, sink_b, rel_bias, w_proj_b, w_o, g_ffn, w_router,
           w_gate_e, w_up_e, w_down_e, g_final):
    bsz, s_len, d = x.shape
    depth = g_mix.shape[0]
    t = bsz * s_len
    cap = CAPACITY_FACTOR * s_len // N_EXPERTS
    n_r = s_len // LANES
    cos, sin = _rope_tables(s_len)
    bucket = _t5_bucket_table()
    x2d = x.reshape(t, d)
    for l in range(depth):
        proj = _inproj(x2d, g_mix[l][None, :], w_in[l].astype(BF16), tm=min(1024, t), tn=1792)
        oa = _attn_a(proj, cos, sin, qn_a[l], kn_a[l], bsz, s_len, d,
                     tq=min(256, s_len), tk=min(512, s_len))
        ob = _attn_b(proj, bucket, rel_bias, sink_b[l], bsz, s_len, d, tq=min(512, s_len))
        x1, h2p, aff = _outproj(oa, ob, proj, x2d, b_gate[l][None, :], w_proj_a[l].astype(BF16),
                                w_proj_b[l].astype(BF16), w_o[l].astype(BF16), g_ffn[l][None, :],
                                _split_router(w_router[l]), bsz, s_len, tm=512)
        idx, gates, rank, sel = _topk(aff.reshape(bsz, N_EXPERTS, n_r, LANES), cap,
                                      rpt=h2p.shape[0] // t)
        y = _ffn(idx.reshape(-1), h2p, gates[..., None], w_gate_e[l], w_up_e[l], w_down_e[l],
                 bsz, s_len, cap, ch=min(256, cap // 2))
        x2d = _combine(y.reshape(bsz * N_EXPERTS * cap, d), x1, rank.reshape(bsz, n_r, N_EXPERTS, LANES),
                       sel.reshape(bsz, n_r, N_EXPERTS, LANES), g_final[None, :], bsz, s_len, cap, win=48,
                       final=(l == depth - 1))
    return x2d.reshape(bsz, s_len, d)
```

```python
import functools
import math

import jax
import jax.numpy as jnp
from jax import lax
from jax.experimental import pallas as pl
from jax.experimental.pallas import tpu as pltpu

F32 = jnp.float32
BF16 = jnp.bfloat16
I32 = jnp.int32
U32 = jnp.uint32

HEAD_DIM = 128
GROUP = 4
BLOCK = 128
GRID_W = 64
ROPE_THETA = 10000.0
NUM_BUCKETS = 32
MAX_DISTANCE = 128
N_EXPERTS = 16
CAPACITY_FACTOR = 2
EPS = 1e-6
NEG = -1e30
LANES = 128
SUBLANES = 8
BF16_ROWS = 16
ONES_ROWS = BF16_ROWS
SAFE_EXP2_RANGE = 100.0
ROW_CHAINS = 2
KV_UNROLL = 8
VMEM_LIMIT = 56 * 1024 * 1024
HIGHEST = lax.Precision.HIGHEST
NT = (((1,), (1,)), ((), ()))
TN = (((0,), (0,)), ((), ()))


def _params(n_axes):
    return pltpu.CompilerParams(dimension_semantics=("arbitrary",) * n_axes,
                                vmem_limit_bytes=VMEM_LIMIT)


def _inproj_kernel(x_ref, g_ref, w_ref, o_ref, h_ref):
    @pl.when(pl.program_id(1) == 0)
    def _():
        x = x_ref[...]
        ms = jnp.mean(x * x, axis=-1, keepdims=True)
        h_ref[...] = (x * lax.rsqrt(ms + EPS) * g_ref[...]).astype(BF16)

    o_ref[...] = jnp.dot(h_ref[...], w_ref[...], preferred_element_type=F32).astype(o_ref.dtype)


def _inproj(x2d, g, w, tm, tn):
    t, d = x2d.shape
    n = w.shape[1]
    return pl.pallas_call(
        _inproj_kernel,
        grid=(t // tm, n // tn),
        in_specs=[pl.BlockSpec((tm, d), lambda i, j: (i, 0)),
                  pl.BlockSpec((1, d), lambda i, j: (0, 0)),
                  pl.BlockSpec((d, tn), lambda i, j: (0, j))],
        out_specs=pl.BlockSpec((tm, tn), lambda i, j: (i, j)),
        out_shape=jax.ShapeDtypeStruct((t, n), BF16),
        scratch_shapes=[pltpu.VMEM((tm, d), BF16)],
        compiler_params=_params(2),
        name="inproj",
    )(x2d, g, w)


def _norm_rope(xb, gain, gain_sw, cos, sin_signed):
    a = lax.broadcasted_iota(I32, (HEAD_DIM, HEAD_DIM), 0)
    c = lax.broadcasted_iota(I32, (HEAD_DIM, HEAD_DIM), 1)
    swap_mat = jnp.where(a == (c ^ 32), 1.0, 0.0).astype(BF16)
    ones_mat = jnp.ones((HEAD_DIM, HEAD_DIM), BF16)
    x = xb.astype(F32)
    sq = x * x
    sq_hi = sq.astype(BF16)
    sq_lo = (sq - sq_hi.astype(F32)).astype(BF16)
    ms = (jnp.dot(sq_hi, ones_mat, preferred_element_type=F32)
          + jnp.dot(sq_lo, ones_mat, preferred_element_type=F32)) * (1.0 / HEAD_DIM)
    swapped = jnp.dot(xb, swap_mat, preferred_element_type=F32)
    return lax.rsqrt(ms + EPS) * (x * (gain * cos) + swapped * (gain_sw * sin_signed))


def _attn_a_kernel(q_ref, k_ref, v_ref, cq_ref, sq_ref, ck_ref, sk_ref, qn_ref, kn_ref, o_ref,
                   kp_ref, vt_ref, qt_ref, ksq_ref, m_ref, acc_ref, st0_ref, st1_ref, pt0_ref, pt1_ref,
                   *, tq, tk):
    s_len = k_ref.shape[0]
    n_chunks = s_len // tk

    @pl.when(pl.program_id(2) == 0)
    def _():
        ksq_ref[...] = jnp.zeros(ksq_ref.shape, F32)

        def body(c, carry):
            off = pl.multiple_of(c * tk, tk)
            kb = _norm_rope(k_ref[pl.ds(off, tk), :], kn_ref[0:1, :], kn_ref[1:2, :],
                            ck_ref[pl.ds(off, tk), :], sk_ref[pl.ds(off, tk), :]).astype(BF16)
            kp_ref[pl.ds(off, tk), :] = kb
            kf = kb.astype(F32)
            ksq_ref[...] = jnp.maximum(ksq_ref[...], jnp.max(jnp.sum(kf * kf, axis=1, keepdims=True)))
            vt_ref[c, :HEAD_DIM, :] = v_ref[pl.ds(off, tk), :].astype(F32).T.astype(BF16)
            vt_ref[c, HEAD_DIM:, :] = jnp.ones((ONES_ROWS, tk), BF16)
            return carry
        lax.fori_loop(0, n_chunks, body, 0)

    scale = math.log2(math.e) / math.sqrt(HEAD_DIM)
    qsq = jnp.zeros((1, tq), F32)
    for g in range(GROUP):
        qg = _norm_rope(q_ref[:, g * HEAD_DIM:(g + 1) * HEAD_DIM], qn_ref[0:1, :], qn_ref[1:2, :],
                        cq_ref[...], sq_ref[...]) * scale
        qb = qg.T.astype(BF16)
        qt_ref[g] = qb
        qf = qb.astype(F32)
        qsq = jnp.maximum(qsq, jnp.sum(qf * qf, axis=0, keepdims=True))
    bound_sq = jnp.max(qsq) * jnp.max(ksq_ref[...])
    no_max_needed = bound_sq <= SAFE_EXP2_RANGE * SAFE_EXP2_RANGE

    acc_ref[...] = jnp.zeros(acc_ref.shape, F32)
    unroll = math.gcd(KV_UNROLL, n_chunks)

    @pl.when(no_max_needed)
    def _():
        m_ref[...] = jnp.zeros(m_ref.shape, F32)

        def probs(c, dst_ref, counts):
            kc = kp_ref[pl.ds(pl.multiple_of(c * tk, tk), tk), :]
            for g in range(GROUP):
                p = jnp.exp2(jnp.dot(kc, qt_ref[g], preferred_element_type=F32))
                m_ref[g] += jnp.sum(p, axis=0, keepdims=True) * counts
                dst_ref[g] = p.astype(BF16)

        def weighted_sum(c, src_ref):
            vt = vt_ref[c, :HEAD_DIM, :]
            for g in range(GROUP):
                acc_ref[g, :HEAD_DIM, :] += jnp.dot(vt, src_ref[g], preferred_element_type=F32)

        probs(0, pt0_ref, 1.0)

        def kv_body(j, carry):
            bufs = (pt0_ref, pt1_ref)
            for u in range(unroll):
                c = unroll * j + u
                probs(jnp.minimum(c + 1, n_chunks - 1), bufs[(u + 1) % 2],
                      jnp.where(c + 1 < n_chunks, 1.0, 0.0))
                weighted_sum(c, bufs[u % 2])
            return carry
        lax.fori_loop(0, n_chunks // unroll, kv_body, 0)
        for g in range(GROUP):
            acc_ref[g, HEAD_DIM:HEAD_DIM + 1, :] = m_ref[g]

    @pl.when(jnp.logical_not(no_max_needed))
    def _():
        m_ref[...] = jnp.full(m_ref.shape, -jnp.inf, F32)

        def scores(c, dst_ref):
            kc = kp_ref[pl.ds(pl.multiple_of(c * tk, tk), tk), :]
            for g in range(GROUP):
                dst_ref[g] = jnp.dot(kc, qt_ref[g], preferred_element_type=F32)

        def softmax_pv(c, src_ref):
            vt = vt_ref[c]
            for g in range(GROUP):
                st = src_ref[g]
                m_prev = m_ref[g]
                m_new = jnp.maximum(m_prev, jnp.max(st, axis=0, keepdims=True))
                pt = jnp.exp2(st - m_new).astype(BF16)
                acc_ref[g] = (jnp.exp2(m_prev - m_new) * acc_ref[g]
                              + jnp.dot(vt, pt, preferred_element_type=F32))
                m_ref[g] = m_new

        scores(0, st0_ref)

        def kv_body(j, carry):
            c = 2 * j
            scores(c + 1, st1_ref)
            softmax_pv(c, st0_ref)
            scores(jnp.minimum(c + 2, n_chunks - 1), st0_ref)
            softmax_pv(c + 1, st1_ref)
            return carry
        lax.fori_loop(0, n_chunks // 2, kv_body, 0)

    for g in range(GROUP):
        acc = acc_ref[g]
        o = (acc[:HEAD_DIM] / acc[HEAD_DIM:HEAD_DIM + 1]).T
        o_ref[:, g * HEAD_DIM:(g + 1) * HEAD_DIM] = o.astype(o_ref.dtype)


def _attn_a(proj, cos, sin, qn, kn, bsz, s_len, d, tq, tk):
    q_w = d // 2
    kv_w = d // 8
    kvh = kv_w // HEAD_DIM
    gw = GROUP * HEAD_DIM
    nq = s_len // tq
    k_blk = q_w // HEAD_DIM
    v_blk = (q_w + kv_w) // HEAD_DIM
    assert (s_len // tk) % 2 == 0

    def with_swapped(gain):
        return jnp.stack([gain, gain.reshape(2, 2, 32)[:, ::-1, :].reshape(HEAD_DIM)])

    qn, kn = with_swapped(qn), with_swapped(kn)
    kern = functools.partial(_attn_a_kernel, tq=tq, tk=tk)
    return pl.pallas_call(
        kern,
        grid=(bsz, kvh, nq),
        in_specs=[pl.BlockSpec((tq, gw), lambda b, k, i: (b * nq + i, k)),
                  pl.BlockSpec((s_len, HEAD_DIM), lambda b, k, i: (b, k_blk + k)),
                  pl.BlockSpec((s_len, HEAD_DIM), lambda b, k, i: (b, v_blk + k)),
                  pl.BlockSpec((tq, HEAD_DIM), lambda b, k, i: (i, 0)),
                  pl.BlockSpec((tq, HEAD_DIM), lambda b, k, i: (i, 0)),
                  pl.BlockSpec((s_len, HEAD_DIM), lambda b, k, i: (0, 0)),
                  pl.BlockSpec((s_len, HEAD_DIM), lambda b, k, i: (0, 0)),
                  pl.BlockSpec((2, HEAD_DIM), lambda b, k, i: (0, 0)),
                  pl.BlockSpec((2, HEAD_DIM), lambda b, k, i: (0, 0))],
        out_specs=pl.BlockSpec((tq, gw), lambda b, k, i: (b * nq + i, k)),
        out_shape=jax.ShapeDtypeStruct((bsz * s_len, q_w), BF16),
        scratch_shapes=[pltpu.VMEM((s_len, HEAD_DIM), BF16),
                        pltpu.VMEM((s_len // tk, HEAD_DIM + ONES_ROWS, tk), BF16),
                        pltpu.VMEM((GROUP, HEAD_DIM, tq), BF16),
                        pltpu.VMEM((1, LANES), F32),
                        pltpu.VMEM((GROUP, 1, tq), F32),
                        pltpu.VMEM((GROUP, HEAD_DIM + ONES_ROWS, tq), F32),
                        pltpu.VMEM((GROUP, tk, tq), F32),
                        pltpu.VMEM((GROUP, tk, tq), F32),
                        pltpu.VMEM((GROUP, tk, tq), BF16),
                        pltpu.VMEM((GROUP, tk, tq), BF16)],
        compiler_params=_params(3),
        name="attn_a",
    )(proj, proj, proj, cos, sin, cos, sin, qn, kn)


def _attn_b_kernel(relb_ref, sink_ref, bucket_ref, q_ref, kp_ref, kc_ref, kn_ref, vp_ref, vc_ref, vn_ref,
                   o_ref, bias_ref, st_ref, *, tq, s_len, kvh):
    b = pl.program_id(0)
    k = pl.program_id(1)
    i = pl.program_id(2)
    band = 3 * BLOCK
    log2e = math.log2(math.e)

    @pl.when((b == 0) & (k == 0) & (i == 0))
    def _():
        bucket_t = bucket_ref[...]
        key = lax.broadcasted_iota(I32, (band, BLOCK), 0)
        qry = lax.broadcasted_iota(I32, (band, BLOCK), 1)
        in_window = jnp.abs(key - BLOCK - qry) <= BLOCK
        for kk in range(kvh):
            for g in range(GROUP):
                h = kk * GROUP + g
                tab = jnp.zeros((band, BLOCK), F32)
                for bkt in range(NUM_BUCKETS):
                    tab = jnp.where(bucket_t == bkt, relb_ref[bkt, h], tab)
                tab = jnp.where(in_window, tab * log2e, NEG)
                cols = slice(g * BLOCK, (g + 1) * BLOCK)
                bias_ref[0, kk, :, cols] = tab
                bias_ref[1, kk, :, cols] = jnp.where(key >= BLOCK, tab, NEG)
                bias_ref[2, kk, :, cols] = jnp.where(key < 2 * BLOCK, tab, NEG)

    nsub = tq // BLOCK
    kcat = jnp.concatenate([kp_ref[...], kc_ref[...], kn_ref[...]], axis=0)
    vcat = jnp.concatenate([vp_ref[...], vc_ref[...], vn_ref[...]], axis=0)
    vt = jnp.concatenate([vcat.astype(F32).T.astype(BF16), jnp.ones((ONES_ROWS, vcat.shape[0]), BF16)], axis=0)
    sink = jnp.concatenate([jnp.full((1, BLOCK), sink_ref[k * GROUP + g], F32) for g in range(GROUP)],
                           axis=1) * log2e
    scale = log2e / math.sqrt(HEAD_DIM)
    for jb in range(nsub):
        rows = slice(jb * BLOCK, (jb + 1) * BLOCK)
        qt = jnp.concatenate([q_ref[rows, g * HEAD_DIM:(g + 1) * HEAD_DIM].astype(F32).T.astype(BF16)
                              for g in range(GROUP)], axis=1)
        if jb == 0:
            variant = jnp.where(i == 0, 1, 0)
        elif jb == nsub - 1:
            variant = jnp.where(i == pl.num_programs(2) - 1, 2, 0)
        else:
            variant = 0
        st_ref[jb] = (jnp.dot(kcat[jb * BLOCK:jb * BLOCK + band], qt, preferred_element_type=F32) * scale
                      + bias_ref[variant, k])
    for jb in range(nsub):
        rows = slice(jb * BLOCK, (jb + 1) * BLOCK)
        st = st_ref[jb]
        m = jnp.maximum(jnp.max(st, axis=0, keepdims=True), sink)
        e = jnp.exp2(st - m).astype(BF16)
        acc = jnp.dot(vt[:, jb * BLOCK:jb * BLOCK + band], e, preferred_element_type=F32)
        o = acc[:HEAD_DIM] / (acc[HEAD_DIM:HEAD_DIM + 1] + jnp.exp2(sink - m))
        for g in range(GROUP):
            o_ref[rows, g * HEAD_DIM:(g + 1) * HEAD_DIM] = o[:, g * BLOCK:(g + 1) * BLOCK].T.astype(o_ref.dtype)


def _attn_b(proj, bucket, rel_bias, sink, bsz, s_len, d, tq):
    q_w = d // 2
    kv_w = d // 8
    kvh = kv_w // HEAD_DIM
    gw = GROUP * HEAD_DIM
    nq = s_len // tq
    sub = tq // BLOCK
    nblk = s_len // BLOCK
    q_blk = (q_w + 2 * kv_w) // gw
    k_blk = (2 * q_w + 2 * kv_w) // HEAD_DIM
    v_blk = (2 * q_w + 3 * kv_w) // HEAD_DIM

    def prev_map(col):
        return lambda b, k, i: (b * nblk + jnp.maximum(i * sub - 1, 0), col + k)

    def cur_map(col):
        return lambda b, k, i: (b * nq + i, col + k)

    def next_map(col):
        return lambda b, k, i: (b * nblk + jnp.minimum((i + 1) * sub, nblk - 1), col + k)

    small = (BLOCK, HEAD_DIM)
    assert sub >= 2
    kern = functools.partial(_attn_b_kernel, tq=tq, s_len=s_len, kvh=kvh)
    return pl.pallas_call(
        kern,
        grid=(bsz, kvh, nq),
        in_specs=[pl.BlockSpec(memory_space=pltpu.SMEM),
                  pl.BlockSpec(memory_space=pltpu.SMEM),
                  pl.BlockSpec((3 * BLOCK, BLOCK), lambda b, k, i: (0, 0)),
                  pl.BlockSpec((tq, gw), lambda b, k, i: (b * nq + i, q_blk + k)),
                  pl.BlockSpec(small, prev_map(k_blk)),
                  pl.BlockSpec((tq, HEAD_DIM), cur_map(k_blk)),
                  pl.BlockSpec(small, next_map(k_blk)),
                  pl.BlockSpec(small, prev_map(v_blk)),
                  pl.BlockSpec((tq, HEAD_DIM), cur_map(v_blk)),
                  pl.BlockSpec(small, next_map(v_blk))],
        out_specs=pl.BlockSpec((tq, gw), lambda b, k, i: (b * nq + i, k)),
        out_shape=jax.ShapeDtypeStruct((bsz * s_len, q_w), BF16),
        scratch_shapes=[pltpu.VMEM((3, kvh, 3 * BLOCK, GROUP * BLOCK), F32),
                        pltpu.VMEM((sub, 3 * BLOCK, GROUP * BLOCK), F32)],
        compiler_params=_params(3),
        name="attn_b",
    )(rel_bias, sink, bucket.T, proj, proj, proj, proj, proj, proj, proj)


def _outproj_kernel(oa_ref, ob_ref, ga0_ref, ga1_ref, gb0_ref, gb1_ref, x_ref, bg_ref, wpa_ref, wpb_ref,
                    wo_ref, gf_ref, wr_ref, x1_ref, h2p_ref, aff_ref):
    tm, d = x_ref.shape
    half = d // 2
    rpt = half // LANES
    rn = tm // ROW_CHAINS
    for rc in range(ROW_CHAINS):
        rows = slice(rc * rn, (rc + 1) * rn)
        oa = oa_ref[rows, :]
        ob = ob_ref[rows, :]
        parts = []
        for c, (ga_ref, gb_ref) in enumerate(((ga0_ref, gb0_ref), (ga1_ref, gb1_ref))):
            cols = slice(c * half, (c + 1) * half)
            pa = jnp.dot(oa, wpa_ref[:, cols], preferred_element_type=F32)
            pb = jnp.dot(ob, wpb_ref[:, cols], preferred_element_type=F32)
            gate_a = jax.nn.sigmoid(ga_ref[rows, :].astype(F32) + bg_ref[:, cols])
            gate_b = jax.nn.sigmoid(gb_ref[rows, :].astype(F32) + bg_ref[:, d + c * half:d + (c + 1) * half])
            parts.append((gate_a * pa + gate_b * pb).astype(BF16))
        x1 = x_ref[rows, :] + (jnp.dot(parts[0], wo_ref[:half, :], preferred_element_type=F32)
                               + jnp.dot(parts[1], wo_ref[half:, :], preferred_element_type=F32))
        x1_ref[rows, :] = x1
        ms = jnp.mean(x1 * x1, axis=-1, keepdims=True)
        h = x1 * lax.rsqrt(ms + EPS) * gf_ref[...]
        h_hi = h.astype(BF16)
        bits = pltpu.bitcast(h_hi.astype(F32), U32)
        packed = (bits[:, :half] >> 16) | (bits[:, half:] & jnp.uint32(0xFFFF0000))
        for s in range(rpt):
            h2p_ref[pl.ds(rc * rn * rpt + s, rn, stride=rpt), :] = packed[:, s * LANES:(s + 1) * LANES]
        h_lo = (h - h_hi.astype(F32)).astype(BF16)
        prod = (jnp.dot(h_hi, wr_ref[...], preferred_element_type=F32)
                + jnp.dot(h_lo, wr_ref[...], preferred_element_type=F32))
        logits = (prod[:, :LANES] + prod[:, LANES:]).T[:N_EXPERTS, :]
        ex = jnp.exp(logits - jnp.max(logits, axis=0, keepdims=True))
        aff_ref[:, rows] = ex / jnp.sum(ex, axis=0, keepdims=True)


def _split_router(w):
    w_hi = w.astype(BF16)
    w_lo = (w - w_hi.astype(F32)).astype(BF16)
    pad = ((0, 0), (0, LANES - w.shape[1]))
    return jnp.concatenate([jnp.pad(w_hi, pad), jnp.pad(w_lo, pad)], axis=1)


def _outproj(oa, ob, proj, x2d, b_gate, wpa, wpb, wo, g_ffn, wr_split, bsz, s_len, tm):
    t, d = x2d.shape
    half = d // 2
    nt = s_len // tm
    row = lambda c: (lambda i: (i, c))
    const = lambda i: (0, 0)
    return pl.pallas_call(
        _outproj_kernel,
        grid=(t // tm,),
        in_specs=[pl.BlockSpec((tm, half), row(0)),
                  pl.BlockSpec((tm, half), row(0)),
                  pl.BlockSpec((tm, half), row(3)),
                  pl.BlockSpec((tm, half), row(4)),
                  pl.BlockSpec((tm, half), row(5)),
                  pl.BlockSpec((tm, half), row(6)),
                  pl.BlockSpec((tm, d), row(0)),
                  pl.BlockSpec((1, 2 * d), const),
                  pl.BlockSpec((half, d), const, pipeline_mode=pl.Buffered(1)),
                  pl.BlockSpec((half, d), const, pipeline_mode=pl.Buffered(1)),
                  pl.BlockSpec((d, d), const, pipeline_mode=pl.Buffered(1)),
                  pl.BlockSpec((1, d), const),
                  pl.BlockSpec((d, 2 * LANES), const, pipeline_mode=pl.Buffered(1))],
        out_specs=[pl.BlockSpec((tm, d), row(0)),
                   pl.BlockSpec((tm * (half // LANES), LANES), row(0)),
                   pl.BlockSpec((None, N_EXPERTS, tm), lambda i: (i // nt, 0, i % nt))],
        out_shape=[jax.ShapeDtypeStruct((t, d), F32),
                   jax.ShapeDtypeStruct((t * (half // LANES), LANES), U32),
                   jax.ShapeDtypeStruct((bsz, N_EXPERTS, s_len), F32)],
        compiler_params=_params(1),
        name="outproj",
    )(oa, ob, proj, proj, proj, proj, x2d, b_gate, wpa, wpb, wo, g_ffn, wr_split)


def _topk_kernel(a_ref, idx_ref, gate_ref, rank_ref, sel_ref, *, cap, rpt):
    a = a_ref[...]
    n_e, n_r, _ = a.shape
    bits = pltpu.bitcast(a, I32)

    def total(x):
        return jnp.sum(jnp.sum(x, axis=1, keepdims=True), axis=2, keepdims=True)

    def search(_, carry):
        lo, hi = carry
        mid = lo + ((hi - lo) >> 1)
        enough = total(jnp.where(bits >= mid, 1.0, 0.0)) >= cap
        return jnp.where(enough, mid, lo), jnp.where(enough, hi, mid)

    lo0 = jnp.zeros((n_e, 1, 1), I32)
    hi0 = jnp.full((n_e, 1, 1), 0x7F800000, I32)
    thr, _ = lax.fori_loop(0, 31, search, (lo0, hi0))

    ri = lax.broadcasted_iota(I32, (LANES, LANES), 0)
    ci = lax.broadcasted_iota(I32, (LANES, LANES), 1)
    upper = jnp.where(ri <= ci, 1.0, 0.0).astype(BF16)
    ones = jnp.ones((LANES, LANES), BF16)
    rr = lax.broadcasted_iota(I32, (n_r, n_r), 0)
    rc = lax.broadcasted_iota(I32, (n_r, n_r), 1)
    strict_lower = jnp.where(rc < rr, 1.0, 0.0).astype(BF16)

    def prefix(x):
        x2 = x.reshape(n_e * n_r, LANES).astype(BF16)
        in_row = jnp.dot(x2, upper, preferred_element_type=F32).reshape(n_e, n_r, LANES)
        row_tot = jnp.dot(x2, ones, preferred_element_type=F32).reshape(n_e, n_r, LANES)
        before = jnp.stack([jnp.dot(strict_lower, row_tot[e].astype(BF16), preferred_element_type=F32)
                            for e in range(n_e)], axis=0)
        return in_row + before

    above = jnp.where(bits > thr, 1.0, 0.0)
    tied = jnp.where(bits == thr, 1.0, 0.0)
    need = cap - total(above)
    sel = above + tied * jnp.where(prefix(tied) <= need, 1.0, 0.0)
    cum = prefix(sel)
    rank = (cum - sel).astype(I32)
    for e in range(n_e):
        sel_ref[pl.ds(e, n_r, stride=n_e), :] = sel[e]
        rank_ref[pl.ds(e, n_r, stride=n_e), :] = rank[e]

    slot = lax.broadcasted_iota(I32, (1, cap), 1).astype(F32)
    row_id = lax.broadcasted_iota(I32, (n_r, 1), 0).astype(F32)
    lane_id = lax.broadcasted_iota(I32, (LANES, 1), 0).astype(F32)
    for e in range(n_e):
        c = cum[e]
        row_end = c[:, LANES - 1:LANES]
        srow = jnp.sum(jnp.where(row_end <= slot, 1.0, 0.0), axis=0, keepdims=True)
        pick = jnp.where(row_id == srow, 1.0, 0.0)
        c_row = lax.dot_general(c, pick, TN, precision=HIGHEST, preferred_element_type=F32)
        slane = jnp.sum(jnp.where(c_row <= slot, 1.0, 0.0), axis=0, keepdims=True)
        a_row = lax.dot_general(a[e], pick, TN, precision=HIGHEST, preferred_element_type=F32)
        token = (srow * LANES + slane).astype(I32) + pl.program_id(0) * (n_r * LANES)
        idx_ref[e:e + 1, :] = token * rpt
        gate_ref[e:e + 1, :] = jnp.sum(jnp.where(lane_id == slane, a_row, 0.0), axis=0, keepdims=True)


def _topk(aff4, cap, rpt):
    bsz, n_e, n_r, _ = aff4.shape
    blk4 = pl.BlockSpec((None, n_e, n_r, LANES), lambda b: (b, 0, 0, 0))
    blk3 = pl.BlockSpec((None, n_e, cap), lambda b: (b, 0, 0))
    tiles = pl.BlockSpec((None, n_r * n_e, LANES), lambda b: (b, 0, 0))
    return pl.pallas_call(
        functools.partial(_topk_kernel, cap=cap, rpt=rpt),
        grid=(bsz,),
        in_specs=[blk4],
        out_specs=[blk3, blk3, tiles, tiles],
        out_shape=[jax.ShapeDtypeStruct((bsz, n_e, cap), I32),
                   jax.ShapeDtypeStruct((bsz, n_e, cap), F32),
                   jax.ShapeDtypeStruct((bsz, n_r * n_e, LANES), I32),
                   jax.ShapeDtypeStruct((bsz, n_r * n_e, LANES), F32)],
        compiler_params=_params(1),
        name="topk",
    )(aff4)


def _ffn_kernel(idx_ref, h2_hbm, wg_hbm, wu_hbm, wd_hbm, gate_ref, y_ref, xa, xb, wg_ref, wu_ref, wd_ref,
                stg_g, stg_u, stg_d, sem, wsem, *, bsz, s_len, cap, ch, rpt, unroll):
    n_c = cap // ch
    bufs = (xa, xb)
    e = pl.program_id(0)
    b = pl.program_id(1)
    pair = e * bsz + b
    n_pairs = pl.num_programs(0) * bsz
    slot = e % 2
    rows_gu = stg_g.shape[0]
    rows_d = stg_d.shape[0]

    def weight_copies(ee, part):
        r_gu = pl.ds(pl.multiple_of(part * rows_gu, rows_gu), rows_gu)
        r_d = pl.ds(pl.multiple_of(part * rows_d, rows_d), rows_d)
        return (pltpu.make_async_copy(wg_hbm.at[ee, r_gu, :], stg_g, wsem.at[0]),
                pltpu.make_async_copy(wu_hbm.at[ee, r_gu, :], stg_u, wsem.at[1]),
                pltpu.make_async_copy(wd_hbm.at[ee, r_d, :], stg_d, wsem.at[2]))

    def convert(to_slot, part):
        r_gu = pl.ds(pl.multiple_of(part * rows_gu, rows_gu), rows_gu)
        r_d = pl.ds(pl.multiple_of(part * rows_d, rows_d), rows_d)
        wg_ref[to_slot, r_gu, :] = stg_g[...].astype(BF16)
        wu_ref[to_slot, r_gu, :] = stg_u[...].astype(BF16)
        wd_ref[to_slot, r_d, :] = stg_d[...].astype(BF16)

    @pl.when(pair == 0)
    def _():
        for part in range(bsz):
            copies = weight_copies(0, part)
            for cp in copies:
                cp.start()
            for cp in copies:
                cp.wait()
            convert(0, part)

    @pl.when(e + 1 < pl.num_programs(0))
    def _():
        for cp in weight_copies(e + 1, b):
            cp.start()

    def issue(pair_k, c):
        bb = pair_k % bsz
        ee = pair_k // bsz
        idx_base = (bb * N_EXPERTS + ee) * cap + c * ch
        buf = bufs[c % 2]

        def body(j, carry):
            for u in range(unroll):
                r = j * unroll + u
                src = pl.multiple_of(idx_ref[idx_base + r], rpt)
                dst = pl.multiple_of(r * rpt, rpt)
                pltpu.make_async_copy(h2_hbm.at[pl.ds(src, rpt), :], buf.at[pl.ds(dst, rpt), :],
                                      sem.at[c % 2]).start()
            return carry
        lax.fori_loop(0, ch // unroll, body, 0)

    @pl.when(pair == 0)
    def _():
        issue(pair, 0)

    for c in range(n_c):
        if c + 1 < n_c:
            issue(pair, c + 1)
        else:
            @pl.when(pair + 1 < n_pairs)
            def _():
                issue(pair + 1, 0)
        buf = bufs[c % 2]
        pltpu.make_async_copy(h2_hbm.at[pl.ds(0, ch * rpt), :], buf, sem.at[c % 2]).wait()
        lo, hi = [], []
        for s in range(rpt):
            w = buf[pl.ds(s, ch, stride=rpt), :]
            lo.append(pltpu.bitcast(w << 16, F32).astype(BF16))
            hi.append(pltpu.bitcast(w & jnp.uint32(0xFFFF0000), F32).astype(BF16))
        x = jnp.concatenate(lo + hi, axis=1)
        a = jnp.dot(x, wg_ref[slot], preferred_element_type=F32)
        u = jnp.dot(x, wu_ref[slot], preferred_element_type=F32)
        act = (jax.nn.silu(a) * u).astype(BF16)
        y = jnp.dot(act, wd_ref[slot], preferred_element_type=F32)
        for a in range(ch // LANES):
            ga = c * (ch // LANES) + a
            gcol = jnp.broadcast_to(gate_ref[ga:ga + 1, :], (LANES, LANES)).T
            rows = slice(a * LANES, (a + 1) * LANES)
            y_ref[c * ch + a * LANES:c * ch + (a + 1) * LANES, :] = (
                y[rows, :] * jnp.concatenate([gcol] * (y.shape[1] // LANES), axis=1)).astype(y_ref.dtype)

    @pl.when(e + 1 < pl.num_programs(0))
    def _():
        for cp in weight_copies(e + 1, b):
            cp.wait()
        convert(1 - slot, b)


def _ffn(idx_flat, h2p, gates4, wg, wu, wd, bsz, s_len, cap, ch):
    n_e, d, f = wg.shape
    rpt = h2p.shape[0] // (bsz * s_len)
    assert (cap // ch) % 2 == 0 and ch % LANES == 0 and d % bsz == 0 and f % bsz == 0
    kern = functools.partial(_ffn_kernel, bsz=bsz, s_len=s_len, cap=cap, ch=ch, rpt=rpt, unroll=8)
    grid_spec = pltpu.PrefetchScalarGridSpec(
        num_scalar_prefetch=1,
        grid=(n_e, bsz),
        in_specs=[pl.BlockSpec(memory_space=pl.ANY),
                  pl.BlockSpec(memory_space=pl.ANY),
                  pl.BlockSpec(memory_space=pl.ANY),
                  pl.BlockSpec(memory_space=pl.ANY),
                  pl.BlockSpec((None, None, cap // LANES, LANES), lambda e, b, idx: (b, e, 0, 0))],
        out_specs=pl.BlockSpec((None, None, cap, d), lambda e, b, idx: (b, e, 0, 0)),
        scratch_shapes=[pltpu.VMEM((ch * rpt, LANES), U32),
                        pltpu.VMEM((ch * rpt, LANES), U32),
                        pltpu.VMEM((2, d, f), BF16),
                        pltpu.VMEM((2, d, f), BF16),
                        pltpu.VMEM((2, f, d), BF16),
                        pltpu.VMEM((d // bsz, f), F32),
                        pltpu.VMEM((d // bsz, f), F32),
                        pltpu.VMEM((f // bsz, d), F32),
                        pltpu.SemaphoreType.DMA((2,)),
                        pltpu.SemaphoreType.DMA((3,))],
    )
    return pl.pallas_call(
        kern,
        grid_spec=grid_spec,
        out_shape=jax.ShapeDtypeStruct((bsz, n_e, cap, d), BF16),
        compiler_params=_params(2),
        name="ffn",
    )(idx_flat, h2p, wg, wu, wd, gates4)


def _combine_kernel(src0_ref, first0_ref, rounds_ref, y_hbm, x1_ref, rank_ref, sel_ref, gf_ref, o_ref,
                    ybuf, yextra, sem, *, cap, win, n_t, sub, final):
    b = pl.program_id(0)
    n = b * n_t + pl.program_id(1)
    n_tiles = pl.num_programs(0) * n_t
    slot = n % 2
    rows_all = N_EXPERTS * win

    def start_copies(srcs, dst_ref, sem_k):
        for e, src in enumerate(srcs):
            pltpu.make_async_copy(y_hbm.at[pl.ds(pl.multiple_of(src, BF16_ROWS), win), :],
                                  dst_ref.at[pl.ds(e * win, win), :], sem_k).start()

    def wait_copies(dst_ref, sem_k):
        pltpu.make_async_copy(y_hbm.at[pl.ds(0, rows_all), :], dst_ref, sem_k).wait()

    jcol = lax.broadcasted_iota(I32, (win, 1), 0)

    def expand(firsts, begins, rows_bf16):
        blocks = []
        for e in range(N_EXPERTS):
            hits = []
            for j in range(sub):
                rank = rank_ref[j, e:e + 1, :]
                hits.append(jnp.where(rank == jcol + begins[e],
                                      jnp.where(rank >= firsts[e], sel_ref[j, e:e + 1, :], 0.0), 0.0))
            blocks.append(jnp.concatenate(hits, axis=1))
        onehot = jnp.concatenate(blocks, axis=0).astype(BF16)
        return lax.dot_general(onehot, rows_bf16, TN, preferred_element_type=F32)

    @pl.when(n == 0)
    def _():
        start_copies([src0_ref[e] for e in range(N_EXPERTS)], ybuf.at[slot], sem.at[slot])

    @pl.when(n + 1 < n_tiles)
    def _():
        start_copies([src0_ref[(n + 1) * N_EXPERTS + e] for e in range(N_EXPERTS)],
                     ybuf.at[1 - slot], sem.at[1 - slot])

    bases = [(b * N_EXPERTS + e) * cap for e in range(N_EXPERTS)]
    firsts = [first0_ref[n * N_EXPERTS + e] for e in range(N_EXPERTS)]
    begins = [src0_ref[n * N_EXPERTS + e] - bases[e] for e in range(N_EXPERTS)]

    wait_copies(ybuf.at[slot], sem.at[slot])
    o_ref[...] = x1_ref[...] + expand(firsts, begins, ybuf[slot])

    def extra_round(k, carry):
        firsts_k = [f + k * win for f in firsts]
        begins_k = [jnp.minimum(f, cap - win) for f in firsts_k]
        start_copies([bases[e] + begins_k[e] for e in range(N_EXPERTS)], yextra, sem.at[2])
        wait_copies(yextra, sem.at[2])
        o_ref[...] += expand(firsts_k, begins_k, yextra[...])
        return carry
    lax.fori_loop(1, rounds_ref[n], extra_round, 0)

    if final:
        x2 = o_ref[...]
        ms = jnp.mean(x2 * x2, axis=-1, keepdims=True)
        o_ref[...] = x2 * lax.rsqrt(ms + EPS) * gf_ref[...]


def _combine(y2d, x1, rank_t, sel_t, g_final, bsz, s_len, cap, win, sub, final):
    t, d = x1.shape
    n_t = s_len // (sub * LANES)
    off = rank_t[:, ::sub, :, 0]
    cnt = jnp.concatenate([off[:, 1:], jnp.full((bsz, 1, N_EXPERTS), cap, I32)], axis=1) - off
    first0 = (off // BF16_ROWS) * BF16_ROWS
    base = (jnp.arange(bsz, dtype=I32)[:, None, None] * N_EXPERTS
            + jnp.arange(N_EXPERTS, dtype=I32)[None, None, :]) * cap
    src0 = base + jnp.minimum(first0, cap - win)
    rounds = jnp.max(jnp.where(cnt > 0, (off - first0 + cnt + win - 1) // win, 0), axis=2)

    kern = functools.partial(_combine_kernel, cap=cap, win=win, n_t=n_t, sub=sub, final=final)
    meta = pl.BlockSpec((None, sub, N_EXPERTS, LANES), lambda b, i, *_: (b, i, 0, 0))
    grid_spec = pltpu.PrefetchScalarGridSpec(
        num_scalar_prefetch=3,
        grid=(bsz, n_t),
        in_specs=[pl.BlockSpec(memory_space=pl.ANY),
                  pl.BlockSpec((sub * LANES, d), lambda b, i, *_: (b * n_t + i, 0)),
                  meta, meta,
                  pl.BlockSpec((1, d), lambda b, i, *_: (0, 0))],
        out_specs=pl.BlockSpec((sub * LANES, d), lambda b, i, *_: (b * n_t + i, 0)),
        scratch_shapes=[pltpu.VMEM((2, N_EXPERTS * win, d), BF16),
                        pltpu.VMEM((N_EXPERTS * win, d), BF16),
                        pltpu.SemaphoreType.DMA((3,))],
    )
    return pl.pallas_call(
        kern,
        grid_spec=grid_spec,
        out_shape=jax.ShapeDtypeStruct((t, d), F32),
        compiler_params=_params(2),
        name="combine",
    )(src0.reshape(-1), first0.reshape(-1), rounds.reshape(-1), y2d, x1, rank_t, sel_t, g_final)


def _rope_tables(s_len):
    rows = s_len // GRID_W
    half = HEAD_DIM // 2
    inv = 1.0 / (ROPE_THETA ** (jnp.arange(0, half, 2, dtype=F32) / half))
    ang_r = jnp.arange(rows, dtype=F32)[:, None] * inv
    ang_c = jnp.arange(GRID_W, dtype=F32)[:, None] * inv
    by_row = lambda a: jnp.repeat(a, GRID_W, axis=0)
    by_col = lambda a: jnp.tile(a, (rows, 1))
    cos_r, sin_r = by_row(jnp.cos(ang_r)), by_row(jnp.sin(ang_r))
    cos_c, sin_c = by_col(jnp.cos(ang_c)), by_col(jnp.sin(ang_c))
    cos = jnp.concatenate([cos_r, cos_r, cos_c, cos_c], axis=-1)
    sin = jnp.concatenate([-sin_r, sin_r, -sin_c, sin_c], axis=-1)
    return cos, sin


def _t5_bucket_table():
    rel = (jnp.arange(3 * BLOCK) - BLOCK)[None, :] - jnp.arange(BLOCK)[:, None]
    half = NUM_BUCKETS // 2
    ret = jnp.where(rel > 0, half, 0)
    n = jnp.abs(rel)
    max_exact = half // 2
    nf = jnp.maximum(n, 1).astype(F32)
    large = max_exact + (jnp.log(nf / max_exact) / math.log(MAX_DISTANCE / max_exact)
                         * (half - max_exact)).astype(I32)
    large = jnp.minimum(large, half - 1)
    return (ret + jnp.where(n < max_exact, n, large)).astype(I32)


def kernel(x, g_mix, w_in, b_gate, qn_a, kn_a, w_proj_a, sink_b, rel_bias, w_proj_b, w_o, g_ffn, w_router,
           w_gate_e, w_up_e, w_down_e, g_final):
    bsz, s_len, d = x.shape
    depth = g_mix.shape[0]
    t = bsz * s_len
    cap = CAPACITY_FACTOR * s_len // N_EXPERTS
    n_r = s_len // LANES
    cos, sin = _rope_tables(s_len)
    bucket = _t5_bucket_table()
    x2d = x.reshape(t, d)
    for l in range(depth):
        proj = _inproj(x2d, g_mix[l][None, :], w_in[l].astype(BF16), tm=min(1024, t), tn=1792)
        oa = _attn_a(proj, cos, sin, qn_a[l], kn_a[l], bsz, s_len, d,
                     tq=min(256, s_len), tk=min(512, s_len))
        ob = _attn_b(proj, bucket, rel_bias, sink_b[l], bsz, s_len, d, tq=min(512, s_len))
        x1, h2p, aff = _outproj(oa, ob, proj, x2d, b_gate[l][None, :], w_proj_a[l].astype(BF16),
                                w_proj_b[l].astype(BF16), w_o[l].astype(BF16), g_ffn[l][None, :],
                                _split_router(w_router[l]), bsz, s_len, tm=512)
        idx, gates, rank, sel = _topk(aff.reshape(bsz, N_EXPERTS, n_r, LANES), cap,
                                      rpt=h2p.shape[0] // t)
        y = _ffn(idx.reshape(-1), h2p, gates.reshape(bsz, N_EXPERTS, cap // LANES, LANES),
                 w_gate_e[l], w_up_e[l], w_down_e[l],
                 bsz, s_len, cap, ch=min(512, cap // 2))
        x2d = _combine(y.reshape(bsz * N_EXPERTS * cap, d), x1, rank.reshape(bsz, n_r, N_EXPERTS, LANES),
                       sel.reshape(bsz, n_r, N_EXPERTS, LANES), g_final[None, :], bsz, s_len, cap, win=64, sub=2,
                       final=(l == depth - 1))
    return x2d.reshape(bsz, s_len, d)
```

```python
import functools
import math
from typing import NamedTuple

import jax
import jax.numpy as jnp
from jax import lax
from jax.experimental import pallas as pl
from jax.experimental.pallas import tpu as pltpu

F32 = jnp.float32
BF16 = jnp.bfloat16
I32 = jnp.int32
U32 = jnp.uint32

HEAD_DIM = 128
GROUP = 4
BLOCK = 128
GRID_W = 64
ROPE_THETA = 10000.0
NUM_BUCKETS = 32
MAX_DISTANCE = 128
N_EXPERTS = 16
CAPACITY_FACTOR = 2
EPS = 1e-6
NEG = -1e30
LANES = 128
BF16_ROWS = 16
ONES_ROWS = BF16_ROWS
SAFE_EXP2_RANGE = 100.0
ROW_CHAINS = 2
KV_UNROLL = 8
VMEM_LIMIT = 56 * 1024 * 1024
TN = (((0,), (0,)), ((), ()))


def _params(n_axes):
    return pltpu.CompilerParams(dimension_semantics=("arbitrary",) * n_axes,
                                vmem_limit_bytes=VMEM_LIMIT)


def _inproj_kernel(x_ref, g_ref, w_ref, o_ref, h_ref):
    @pl.when(pl.program_id(1) == 0)
    def _():
        x = x_ref[...]
        ms = jnp.mean(x * x, axis=-1, keepdims=True)
        h_ref[...] = (x * lax.rsqrt(ms + EPS) * g_ref[...]).astype(BF16)

    o_ref[...] = jnp.dot(h_ref[...], w_ref[...], preferred_element_type=F32).astype(o_ref.dtype)


def _inproj(x2d, g, w, tm, tn):
    t, d = x2d.shape
    n = w.shape[1]
    return pl.pallas_call(
        _inproj_kernel,
        grid=(t // tm, n // tn),
        in_specs=[pl.BlockSpec((tm, d), lambda i, j: (i, 0)),
                  pl.BlockSpec((1, d), lambda i, j: (0, 0)),
                  pl.BlockSpec((d, tn), lambda i, j: (0, j))],
        out_specs=pl.BlockSpec((tm, tn), lambda i, j: (i, j)),
        out_shape=jax.ShapeDtypeStruct((t, n), BF16),
        scratch_shapes=[pltpu.VMEM((tm, d), BF16)],
        compiler_params=_params(2),
        name="inproj",
    )(x2d, g, w)


def _norm_rope(xb, gain, gain_sw, cos, sin_signed):
    a = lax.broadcasted_iota(I32, (HEAD_DIM, HEAD_DIM), 0)
    c = lax.broadcasted_iota(I32, (HEAD_DIM, HEAD_DIM), 1)
    swap_mat = jnp.where(a == (c ^ 32), 1.0, 0.0).astype(BF16)
    ones_mat = jnp.ones((HEAD_DIM, HEAD_DIM), BF16)
    x = xb.astype(F32)
    sq = x * x
    sq_hi = sq.astype(BF16)
    sq_lo = (sq - sq_hi.astype(F32)).astype(BF16)
    ms = (jnp.dot(sq_hi, ones_mat, preferred_element_type=F32)
          + jnp.dot(sq_lo, ones_mat, preferred_element_type=F32)) * (1.0 / HEAD_DIM)
    swapped = jnp.dot(xb, swap_mat, preferred_element_type=F32)
    return lax.rsqrt(ms + EPS) * (x * (gain * cos) + swapped * (gain_sw * sin_signed))


def _attn_a_kernel(q_ref, k_ref, v_ref, cq_ref, sq_ref, ck_ref, sk_ref, qn_ref, kn_ref, o_ref,
                   kp_ref, vt_ref, qt_ref, ksq_ref, m_ref, acc_ref, st0_ref, st1_ref, pt0_ref, pt1_ref,
                   *, tq, tk):
    s_len = k_ref.shape[0]
    n_chunks = s_len // tk

    @pl.when(pl.program_id(2) == 0)
    def _():
        ksq_ref[...] = jnp.zeros(ksq_ref.shape, F32)

        def body(c, carry):
            off = pl.multiple_of(c * tk, tk)
            kb = _norm_rope(k_ref[pl.ds(off, tk), :], kn_ref[0:1, :], kn_ref[1:2, :],
                            ck_ref[pl.ds(off, tk), :], sk_ref[pl.ds(off, tk), :]).astype(BF16)
            kp_ref[pl.ds(off, tk), :] = kb
            kf = kb.astype(F32)
            ksq_ref[...] = jnp.maximum(ksq_ref[...], jnp.max(jnp.sum(kf * kf, axis=1, keepdims=True)))
            vt_ref[c, :HEAD_DIM, :] = v_ref[pl.ds(off, tk), :].astype(F32).T.astype(BF16)
            vt_ref[c, HEAD_DIM:, :] = jnp.ones((ONES_ROWS, tk), BF16)
            return carry
        lax.fori_loop(0, n_chunks, body, 0)

    scale = math.log2(math.e) / math.sqrt(HEAD_DIM)
    qsq = jnp.zeros((1, tq), F32)
    for g in range(GROUP):
        qg = _norm_rope(q_ref[:, g * HEAD_DIM:(g + 1) * HEAD_DIM], qn_ref[0:1, :], qn_ref[1:2, :],
                        cq_ref[...], sq_ref[...]) * scale
        qb = qg.T.astype(BF16)
        qt_ref[g] = qb
        qf = qb.astype(F32)
        qsq = jnp.maximum(qsq, jnp.sum(qf * qf, axis=0, keepdims=True))
    bound_sq = jnp.max(qsq) * jnp.max(ksq_ref[...])
    no_max_needed = bound_sq <= SAFE_EXP2_RANGE * SAFE_EXP2_RANGE

    acc_ref[...] = jnp.zeros(acc_ref.shape, F32)
    unroll = math.gcd(KV_UNROLL, n_chunks)

    @pl.when(no_max_needed)
    def _():
        m_ref[...] = jnp.zeros(m_ref.shape, F32)

        def probs(c, dst_ref):
            kc = kp_ref[pl.ds(pl.multiple_of(c * tk, tk), tk), :]
            for g in range(GROUP):
                p = jnp.exp2(jnp.dot(kc, qt_ref[g], preferred_element_type=F32))
                m_ref[g] += jnp.sum(p, axis=0, keepdims=True)
                dst_ref[g] = p.astype(BF16)

        def weighted_sum(c, src_ref):
            vt = vt_ref[c, :HEAD_DIM, :]
            for g in range(GROUP):
                acc_ref[g, :HEAD_DIM, :] += jnp.dot(vt, src_ref[g], preferred_element_type=F32)

        probs(0, pt0_ref)

        def trip(j, last):
            bufs = (pt0_ref, pt1_ref)
            for u in range(unroll):
                c = unroll * j + u
                if not (last and u == unroll - 1):
                    probs(c + 1, bufs[(u + 1) % 2])
                weighted_sum(c, bufs[u % 2])

        def kv_body(j, carry):
            trip(j, False)
            return carry
        n_trips = n_chunks // unroll
        lax.fori_loop(0, n_trips - 1, kv_body, 0)
        trip(n_trips - 1, True)
        for g in range(GROUP):
            acc_ref[g, HEAD_DIM:HEAD_DIM + 1, :] = m_ref[g]

    @pl.when(jnp.logical_not(no_max_needed))
    def _():
        m_ref[...] = jnp.full(m_ref.shape, -jnp.inf, F32)

        def scores(c, dst_ref):
            kc = kp_ref[pl.ds(pl.multiple_of(c * tk, tk), tk), :]
            for g in range(GROUP):
                dst_ref[g] = jnp.dot(kc, qt_ref[g], preferred_element_type=F32)

        def softmax_pv(c, src_ref):
            vt = vt_ref[c]
            for g in range(GROUP):
                st = src_ref[g]
                m_prev = m_ref[g]
                m_new = jnp.maximum(m_prev, jnp.max(st, axis=0, keepdims=True))
                pt = jnp.exp2(st - m_new).astype(BF16)
                acc_ref[g] = (jnp.exp2(m_prev - m_new) * acc_ref[g]
                              + jnp.dot(vt, pt, preferred_element_type=F32))
                m_ref[g] = m_new

        scores(0, st0_ref)

        def kv_body(j, carry):
            c = 2 * j
            scores(c + 1, st1_ref)
            softmax_pv(c, st0_ref)
            scores(jnp.minimum(c + 2, n_chunks - 1), st0_ref)
            softmax_pv(c + 1, st1_ref)
            return carry
        lax.fori_loop(0, n_chunks // 2, kv_body, 0)

    for g in range(GROUP):
        acc = acc_ref[g]
        o = (acc[:HEAD_DIM] / acc[HEAD_DIM:HEAD_DIM + 1]).T
        o_ref[:, g * HEAD_DIM:(g + 1) * HEAD_DIM] = o.astype(o_ref.dtype)


def _attn_a(proj, cos, sin, qn, kn, bsz, s_len, d, tq, tk):
    q_w = d // 2
    kv_w = d // 8
    kvh = kv_w // HEAD_DIM
    gw = GROUP * HEAD_DIM
    nq = s_len // tq
    k_blk = q_w // HEAD_DIM
    v_blk = (q_w + kv_w) // HEAD_DIM
    assert (s_len // tk) % 2 == 0

    def with_swapped(gain):
        return jnp.stack([gain, gain.reshape(2, 2, 32)[:, ::-1, :].reshape(HEAD_DIM)])

    qn, kn = with_swapped(qn), with_swapped(kn)
    kern = functools.partial(_attn_a_kernel, tq=tq, tk=tk)
    return pl.pallas_call(
        kern,
        grid=(bsz, kvh, nq),
        in_specs=[pl.BlockSpec((tq, gw), lambda b, k, i: (b * nq + i, k)),
                  pl.BlockSpec((s_len, HEAD_DIM), lambda b, k, i: (b, k_blk + k)),
                  pl.BlockSpec((s_len, HEAD_DIM), lambda b, k, i: (b, v_blk + k)),
                  pl.BlockSpec((tq, HEAD_DIM), lambda b, k, i: (i, 0)),
                  pl.BlockSpec((tq, HEAD_DIM), lambda b, k, i: (i, 0)),
                  pl.BlockSpec((s_len, HEAD_DIM), lambda b, k, i: (0, 0)),
                  pl.BlockSpec((s_len, HEAD_DIM), lambda b, k, i: (0, 0)),
                  pl.BlockSpec((2, HEAD_DIM), lambda b, k, i: (0, 0)),
                  pl.BlockSpec((2, HEAD_DIM), lambda b, k, i: (0, 0))],
        out_specs=pl.BlockSpec((tq, gw), lambda b, k, i: (b * nq + i, k)),
        out_shape=jax.ShapeDtypeStruct((bsz * s_len, q_w), BF16),
        scratch_shapes=[pltpu.VMEM((s_len, HEAD_DIM), BF16),
                        pltpu.VMEM((s_len // tk, HEAD_DIM + ONES_ROWS, tk), BF16),
                        pltpu.VMEM((GROUP, HEAD_DIM, tq), BF16),
                        pltpu.VMEM((1, LANES), F32),
                        pltpu.VMEM((GROUP, 1, tq), F32),
                        pltpu.VMEM((GROUP, HEAD_DIM + ONES_ROWS, tq), F32),
                        pltpu.VMEM((GROUP, tk, tq), F32),
                        pltpu.VMEM((GROUP, tk, tq), F32),
                        pltpu.VMEM((GROUP, tk, tq), BF16),
                        pltpu.VMEM((GROUP, tk, tq), BF16)],
        compiler_params=_params(3),
        name="attn_a",
    )(proj, proj, proj, cos, sin, cos, sin, qn, kn)


def _attn_b_kernel(relb_ref, sink_ref, bucket_ref, q_ref, kp_ref, kc_ref, kn_ref, vp_ref, vc_ref, vn_ref,
                   o_ref, bias_ref, st_ref, *, tq, kvh):
    b = pl.program_id(0)
    k = pl.program_id(1)
    i = pl.program_id(2)
    band = 3 * BLOCK
    log2e = math.log2(math.e)

    @pl.when((b == 0) & (k == 0) & (i == 0))
    def _():
        bucket_t = bucket_ref[...]
        key = lax.broadcasted_iota(I32, (band, BLOCK), 0)
        qry = lax.broadcasted_iota(I32, (band, BLOCK), 1)
        in_window = jnp.abs(key - BLOCK - qry) <= BLOCK
        for kk in range(kvh):
            for g in range(GROUP):
                h = kk * GROUP + g
                tab = jnp.zeros((band, BLOCK), F32)
                for bkt in range(NUM_BUCKETS):
                    tab = jnp.where(bucket_t == bkt, relb_ref[bkt, h], tab)
                tab = jnp.where(in_window, tab * log2e, NEG)
                cols = slice(g * BLOCK, (g + 1) * BLOCK)
                bias_ref[0, kk, :, cols] = tab
                bias_ref[1, kk, :, cols] = jnp.where(key >= BLOCK, tab, NEG)
                bias_ref[2, kk, :, cols] = jnp.where(key < 2 * BLOCK, tab, NEG)

    nsub = tq // BLOCK
    kcat = jnp.concatenate([kp_ref[...], kc_ref[...], kn_ref[...]], axis=0)
    vcat = jnp.concatenate([vp_ref[...], vc_ref[...], vn_ref[...]], axis=0)
    vt = jnp.concatenate([vcat.astype(F32).T.astype(BF16), jnp.ones((ONES_ROWS, vcat.shape[0]), BF16)], axis=0)
    sink = jnp.concatenate([jnp.full((1, BLOCK), sink_ref[k * GROUP + g], F32) for g in range(GROUP)],
                           axis=1) * log2e
    scale = log2e / math.sqrt(HEAD_DIM)
    for jb in range(nsub):
        rows = slice(jb * BLOCK, (jb + 1) * BLOCK)
        qt = jnp.concatenate([q_ref[rows, g * HEAD_DIM:(g + 1) * HEAD_DIM].astype(F32).T.astype(BF16)
                              for g in range(GROUP)], axis=1)
        if jb == 0:
            variant = jnp.where(i == 0, 1, 0)
        elif jb == nsub - 1:
            variant = jnp.where(i == pl.num_programs(2) - 1, 2, 0)
        else:
            variant = 0
        st_ref[jb] = (jnp.dot(kcat[jb * BLOCK:jb * BLOCK + band], qt, preferred_element_type=F32) * scale
                      + bias_ref[variant, k])
    for jb in range(nsub):
        rows = slice(jb * BLOCK, (jb + 1) * BLOCK)
        st = st_ref[jb]
        m = jnp.maximum(jnp.max(st, axis=0, keepdims=True), sink)
        e = jnp.exp2(st - m).astype(BF16)
        acc = jnp.dot(vt[:, jb * BLOCK:jb * BLOCK + band], e, preferred_element_type=F32)
        o = acc[:HEAD_DIM] / (acc[HEAD_DIM:HEAD_DIM + 1] + jnp.exp2(sink - m))
        for g in range(GROUP):
            o_ref[rows, g * HEAD_DIM:(g + 1) * HEAD_DIM] = o[:, g * BLOCK:(g + 1) * BLOCK].T.astype(o_ref.dtype)


def _attn_b(proj, bucket, rel_bias, sink, bsz, s_len, d, tq):
    q_w = d // 2
    kv_w = d // 8
    kvh = kv_w // HEAD_DIM
    gw = GROUP * HEAD_DIM
    nq = s_len // tq
    sub = tq // BLOCK
    nblk = s_len // BLOCK
    q_blk = (q_w + 2 * kv_w) // gw
    k_blk = (2 * q_w + 2 * kv_w) // HEAD_DIM
    v_blk = (2 * q_w + 3 * kv_w) // HEAD_DIM

    def prev_map(col):
        return lambda b, k, i: (b * nblk + jnp.maximum(i * sub - 1, 0), col + k)

    def cur_map(col):
        return lambda b, k, i: (b * nq + i, col + k)

    def next_map(col):
        return lambda b, k, i: (b * nblk + jnp.minimum((i + 1) * sub, nblk - 1), col + k)

    small = (BLOCK, HEAD_DIM)
    assert sub >= 2
    kern = functools.partial(_attn_b_kernel, tq=tq, kvh=kvh)
    return pl.pallas_call(
        kern,
        grid=(bsz, kvh, nq),
        in_specs=[pl.BlockSpec(memory_space=pltpu.SMEM),
                  pl.BlockSpec(memory_space=pltpu.SMEM),
                  pl.BlockSpec((3 * BLOCK, BLOCK), lambda b, k, i: (0, 0)),
                  pl.BlockSpec((tq, gw), lambda b, k, i: (b * nq + i, q_blk + k)),
                  pl.BlockSpec(small, prev_map(k_blk)),
                  pl.BlockSpec((tq, HEAD_DIM), cur_map(k_blk)),
                  pl.BlockSpec(small, next_map(k_blk)),
                  pl.BlockSpec(small, prev_map(v_blk)),
                  pl.BlockSpec((tq, HEAD_DIM), cur_map(v_blk)),
                  pl.BlockSpec(small, next_map(v_blk))],
        out_specs=pl.BlockSpec((tq, gw), lambda b, k, i: (b * nq + i, k)),
        out_shape=jax.ShapeDtypeStruct((bsz * s_len, q_w), BF16),
        scratch_shapes=[pltpu.VMEM((3, kvh, 3 * BLOCK, GROUP * BLOCK), F32),
                        pltpu.VMEM((sub, 3 * BLOCK, GROUP * BLOCK), F32)],
        compiler_params=_params(3),
        name="attn_b",
    )(rel_bias, sink, bucket.T, proj, proj, proj, proj, proj, proj, proj)


def _outproj_kernel(oa_ref, ob_ref, ga0_ref, ga1_ref, gb0_ref, gb1_ref, x_ref, bg_ref, wpa_ref, wpb_ref,
                    wo_ref, gf_ref, wr_ref, x1_ref, h2p_ref, aff_ref):
    tm, d = x_ref.shape
    half = d // 2
    rpt = half // LANES
    rn = tm // ROW_CHAINS
    for rc in range(ROW_CHAINS):
        rows = slice(rc * rn, (rc + 1) * rn)
        oa = oa_ref[rows, :]
        ob = ob_ref[rows, :]
        parts = []
        for c, (ga_ref, gb_ref) in enumerate(((ga0_ref, gb0_ref), (ga1_ref, gb1_ref))):
            cols = slice(c * half, (c + 1) * half)
            pa = jnp.dot(oa, wpa_ref[:, cols], preferred_element_type=F32)
            pb = jnp.dot(ob, wpb_ref[:, cols], preferred_element_type=F32)
            gate_a = jax.nn.sigmoid(ga_ref[rows, :].astype(F32) + bg_ref[:, cols])
            gate_b = jax.nn.sigmoid(gb_ref[rows, :].astype(F32) + bg_ref[:, d + c * half:d + (c + 1) * half])
            parts.append((gate_a * pa + gate_b * pb).astype(BF16))
        x1 = x_ref[rows, :] + (jnp.dot(parts[0], wo_ref[:half, :], preferred_element_type=F32)
                               + jnp.dot(parts[1], wo_ref[half:, :], preferred_element_type=F32))
        x1_ref[rows, :] = x1
        ms = jnp.mean(x1 * x1, axis=-1, keepdims=True)
        h = x1 * lax.rsqrt(ms + EPS) * gf_ref[...]
        h_hi = h.astype(BF16)
        bits = pltpu.bitcast(h_hi.astype(F32), U32)
        packed = (bits[:, :half] >> 16) | (bits[:, half:] & jnp.uint32(0xFFFF0000))
        for s in range(rpt):
            h2p_ref[pl.ds(rc * rn * rpt + s, rn, stride=rpt), :] = packed[:, s * LANES:(s + 1) * LANES]
        h_lo = (h - h_hi.astype(F32)).astype(BF16)
        prod = (jnp.dot(h_hi, wr_ref[...], preferred_element_type=F32)
                + jnp.dot(h_lo, wr_ref[...], preferred_element_type=F32))
        logits = (prod[:, :LANES] + prod[:, LANES:]).T[:N_EXPERTS, :]
        ex = jnp.exp(logits - jnp.max(logits, axis=0, keepdims=True))
        aff_ref[:, rows] = ex / jnp.sum(ex, axis=0, keepdims=True)


def _split_router(w):
    w_hi = w.astype(BF16)
    w_lo = (w - w_hi.astype(F32)).astype(BF16)
    pad = ((0, 0), (0, LANES - w.shape[1]))
    return jnp.concatenate([jnp.pad(w_hi, pad), jnp.pad(w_lo, pad)], axis=1)


def _outproj(oa, ob, proj, x2d, b_gate, wpa, wpb, wo, g_ffn, wr_split, bsz, s_len, tm):
    t, d = x2d.shape
    half = d // 2
    nt = s_len // tm
    row = lambda c: (lambda i: (i, c))
    const = lambda i: (0, 0)
    return pl.pallas_call(
        _outproj_kernel,
        grid=(t // tm,),
        in_specs=[pl.BlockSpec((tm, half), row(0)),
                  pl.BlockSpec((tm, half), row(0)),
                  pl.BlockSpec((tm, half), row(3)),
                  pl.BlockSpec((tm, half), row(4)),
                  pl.BlockSpec((tm, half), row(5)),
                  pl.BlockSpec((tm, half), row(6)),
                  pl.BlockSpec((tm, d), row(0)),
                  pl.BlockSpec((1, 2 * d), const),
                  pl.BlockSpec((half, d), const, pipeline_mode=pl.Buffered(1)),
                  pl.BlockSpec((half, d), const, pipeline_mode=pl.Buffered(1)),
                  pl.BlockSpec((d, d), const, pipeline_mode=pl.Buffered(1)),
                  pl.BlockSpec((1, d), const),
                  pl.BlockSpec((d, 2 * LANES), const, pipeline_mode=pl.Buffered(1))],
        out_specs=[pl.BlockSpec((tm, d), row(0)),
                   pl.BlockSpec((tm * (half // LANES), LANES), row(0)),
                   pl.BlockSpec((None, N_EXPERTS, tm), lambda i: (i // nt, 0, i % nt))],
        out_shape=[jax.ShapeDtypeStruct((t, d), F32),
                   jax.ShapeDtypeStruct((t * (half // LANES), LANES), U32),
                   jax.ShapeDtypeStruct((bsz, N_EXPERTS, s_len), F32)],
        compiler_params=_params(1),
        name="outproj",
    )(oa, ob, proj, proj, proj, proj, x2d, b_gate, wpa, wpb, wo, g_ffn, wr_split)


def _topk_kernel(a_ref, idx_ref, gate_ref, rank_ref, sel_ref, *, cap, rpt):
    a = a_ref[...]
    n_e, n_r, _ = a.shape
    bits = pltpu.bitcast(a, I32)

    def total(x):
        return jnp.sum(jnp.sum(x, axis=1, keepdims=True), axis=2, keepdims=True)

    def search(_, carry):
        lo, hi = carry
        mid = lo + ((hi - lo) >> 1)
        enough = total(jnp.where(bits >= mid, 1.0, 0.0)) >= cap
        return jnp.where(enough, mid, lo), jnp.where(enough, hi, mid)

    lo0 = jnp.zeros((n_e, 1, 1), I32)
    hi0 = jnp.full((n_e, 1, 1), 0x7F800000, I32)
    thr, _ = lax.fori_loop(0, 31, search, (lo0, hi0))

    ri = lax.broadcasted_iota(I32, (LANES, LANES), 0)
    ci = lax.broadcasted_iota(I32, (LANES, LANES), 1)
    upper = jnp.where(ri <= ci, 1.0, 0.0).astype(BF16)
    ones = jnp.ones((LANES, LANES), BF16)
    rr = lax.broadcasted_iota(I32, (n_r, n_r), 0)
    rc = lax.broadcasted_iota(I32, (n_r, n_r), 1)
    strict_lower = jnp.where(rc < rr, 1.0, 0.0).astype(BF16)

    def prefix(x):
        x2 = x.reshape(n_e * n_r, LANES).astype(BF16)
        in_row = jnp.dot(x2, upper, preferred_element_type=F32).reshape(n_e, n_r, LANES)
        row_tot = jnp.dot(x2, ones, preferred_element_type=F32).reshape(n_e, n_r, LANES)
        before = jnp.stack([jnp.dot(strict_lower, row_tot[e].astype(BF16), preferred_element_type=F32)
                            for e in range(n_e)], axis=0)
        return in_row + before

    above = jnp.where(bits > thr, 1.0, 0.0)
    tied = jnp.where(bits == thr, 1.0, 0.0)
    need = cap - total(above)
    sel = above + tied * jnp.where(prefix(tied) <= need, 1.0, 0.0)
    cum = prefix(sel)
    rank = (cum - sel).astype(I32)
    for e in range(n_e):
        sel_ref[pl.ds(e, n_r, stride=n_e), :] = sel[e]
        rank_ref[pl.ds(e, n_r, stride=n_e), :] = rank[e]

    slot = lax.broadcasted_iota(I32, (1, cap), 1).astype(F32)
    row_id = lax.broadcasted_iota(I32, (n_r, 1), 0).astype(F32)
    lane_id = lax.broadcasted_iota(I32, (LANES, 1), 0).astype(F32)
    for e in range(n_e):
        c = cum[e]
        row_end = c[:, LANES - 1:LANES]
        srow = jnp.sum(jnp.where(row_end <= slot, 1.0, 0.0), axis=0, keepdims=True)
        pick = jnp.where(row_id == srow, 1.0, 0.0).astype(BF16)
        c_hi = jnp.floor(c * (1.0 / 32.0))
        c_lo = c - 32.0 * c_hi
        a1 = a[e].astype(BF16)
        r1 = a[e] - a1.astype(F32)
        a2 = r1.astype(BF16)
        a3 = (r1 - a2.astype(F32)).astype(BF16)
        parts = jnp.concatenate([c_hi.astype(BF16), c_lo.astype(BF16), a1, a2, a3], axis=1)
        rows = lax.dot_general(parts, pick, TN, preferred_element_type=F32)
        c_row = 32.0 * rows[:LANES] + rows[LANES:2 * LANES]
        a_row = (rows[2 * LANES:3 * LANES] + rows[3 * LANES:4 * LANES]) + rows[4 * LANES:]
        slane = jnp.sum(jnp.where(c_row <= slot, 1.0, 0.0), axis=0, keepdims=True)
        token = (srow * LANES + slane).astype(I32) + pl.program_id(0) * (n_r * LANES)
        idx_ref[e:e + 1, :] = token * rpt
        gate_ref[e:e + 1, :] = jnp.sum(jnp.where(lane_id == slane, a_row, 0.0), axis=0, keepdims=True)


def _topk(aff4, cap, rpt):
    bsz, n_e, n_r, _ = aff4.shape
    blk4 = pl.BlockSpec((None, n_e, n_r, LANES), lambda b: (b, 0, 0, 0))
    blk3 = pl.BlockSpec((None, n_e, cap), lambda b: (b, 0, 0))
    tiles = pl.BlockSpec((None, n_r * n_e, LANES), lambda b: (b, 0, 0))
    return pl.pallas_call(
        functools.partial(_topk_kernel, cap=cap, rpt=rpt),
        grid=(bsz,),
        in_specs=[blk4],
        out_specs=[blk3, blk3, tiles, tiles],
        out_shape=[jax.ShapeDtypeStruct((bsz, n_e, cap), I32),
                   jax.ShapeDtypeStruct((bsz, n_e, cap), F32),
                   jax.ShapeDtypeStruct((bsz, n_r * n_e, LANES), I32),
                   jax.ShapeDtypeStruct((bsz, n_r * n_e, LANES), F32)],
        compiler_params=_params(1),
        name="topk",
    )(aff4)


def _ffn_kernel(idx_ref, h2_hbm, wg_hbm, wu_hbm, wd_hbm, gate_ref, y_ref, xa, xb, wg_ref, wu_ref, wd_ref,
                stg_g, stg_u, stg_d, sem, wsem, *, bsz, cap, ch, rpt, unroll):
    n_c = cap // ch
    bufs = (xa, xb)
    e = pl.program_id(0)
    b = pl.program_id(1)
    pair = e * bsz + b
    n_pairs = pl.num_programs(0) * bsz
    slot = e % 2
    rows_gu = stg_g.shape[0]
    rows_d = stg_d.shape[0]

    def weight_copies(ee, part):
        r_gu = pl.ds(pl.multiple_of(part * rows_gu, rows_gu), rows_gu)
        r_d = pl.ds(pl.multiple_of(part * rows_d, rows_d), rows_d)
        return (pltpu.make_async_copy(wg_hbm.at[ee, r_gu, :], stg_g, wsem.at[0]),
                pltpu.make_async_copy(wu_hbm.at[ee, r_gu, :], stg_u, wsem.at[1]),
                pltpu.make_async_copy(wd_hbm.at[ee, r_d, :], stg_d, wsem.at[2]))

    def convert(to_slot, part):
        r_gu = pl.ds(pl.multiple_of(part * rows_gu, rows_gu), rows_gu)
        r_d = pl.ds(pl.multiple_of(part * rows_d, rows_d), rows_d)
        wg_ref[to_slot, r_gu, :] = stg_g[...].astype(BF16)
        wu_ref[to_slot, r_gu, :] = stg_u[...].astype(BF16)
        wd_ref[to_slot, r_d, :] = stg_d[...].astype(BF16)

    @pl.when(pair == 0)
    def _():
        for part in range(bsz):
            copies = weight_copies(0, part)
            for cp in copies:
                cp.start()
            for cp in copies:
                cp.wait()
            convert(0, part)

    @pl.when(e + 1 < pl.num_programs(0))
    def _():
        for cp in weight_copies(e + 1, b):
            cp.start()

    def issue(pair_k, c):
        bb = pair_k % bsz
        ee = pair_k // bsz
        idx_base = (bb * N_EXPERTS + ee) * cap + c * ch
        buf = bufs[c % 2]

        def body(j, carry):
            for u in range(unroll):
                r = j * unroll + u
                src = pl.multiple_of(idx_ref[idx_base + r], rpt)
                dst = pl.multiple_of(r * rpt, rpt)
                pltpu.make_async_copy(h2_hbm.at[pl.ds(src, rpt), :], buf.at[pl.ds(dst, rpt), :],
                                      sem.at[c % 2]).start()
            return carry
        lax.fori_loop(0, ch // unroll, body, 0)

    @pl.when(pair == 0)
    def _():
        issue(pair, 0)

    for c in range(n_c):
        if c + 1 < n_c:
            issue(pair, c + 1)
        else:
            @pl.when(pair + 1 < n_pairs)
            def _():
                issue(pair + 1, 0)
        buf = bufs[c % 2]
        pltpu.make_async_copy(h2_hbm.at[pl.ds(0, ch * rpt), :], buf, sem.at[c % 2]).wait()
        lo, hi = [], []
        for s in range(rpt):
            w = buf[pl.ds(s, ch, stride=rpt), :]
            lo.append(pltpu.bitcast(w << 16, F32).astype(BF16))
            hi.append(pltpu.bitcast(w & jnp.uint32(0xFFFF0000), F32).astype(BF16))
        x = jnp.concatenate(lo + hi, axis=1)
        a = jnp.dot(x, wg_ref[slot], preferred_element_type=F32)
        u = jnp.dot(x, wu_ref[slot], preferred_element_type=F32)
        act = (jax.nn.silu(a) * u).astype(BF16)
        y = jnp.dot(act, wd_ref[slot], preferred_element_type=F32)
        for blk in range(ch // LANES):
            first = c * ch + blk * LANES
            gcol = jnp.broadcast_to(gate_ref[first // LANES:first // LANES + 1, :], (LANES, LANES)).T
            y_ref[first:first + LANES, :] = (
                y[blk * LANES:(blk + 1) * LANES, :]
                * jnp.concatenate([gcol] * (y.shape[1] // LANES), axis=1)).astype(y_ref.dtype)

    @pl.when(e + 1 < pl.num_programs(0))
    def _():
        for cp in weight_copies(e + 1, b):
            cp.wait()
        convert(1 - slot, b)


def _ffn(idx_flat, h2p, gates4, wg, wu, wd, bsz, s_len, cap, ch):
    n_e, d, f = wg.shape
    rpt = h2p.shape[0] // (bsz * s_len)
    assert (cap // ch) % 2 == 0 and ch % LANES == 0 and d % bsz == 0 and f % bsz == 0
    kern = functools.partial(_ffn_kernel, bsz=bsz, cap=cap, ch=ch, rpt=rpt, unroll=8)
    grid_spec = pltpu.PrefetchScalarGridSpec(
        num_scalar_prefetch=1,
        grid=(n_e, bsz),
        in_specs=[pl.BlockSpec(memory_space=pl.ANY),
                  pl.BlockSpec(memory_space=pl.ANY),
                  pl.BlockSpec(memory_space=pl.ANY),
                  pl.BlockSpec(memory_space=pl.ANY),
                  pl.BlockSpec((None, None, cap // LANES, LANES), lambda e, b, idx: (b, e, 0, 0))],
        out_specs=pl.BlockSpec((None, None, cap, d), lambda e, b, idx: (b, e, 0, 0)),
        scratch_shapes=[pltpu.VMEM((ch * rpt, LANES), U32),
                        pltpu.VMEM((ch * rpt, LANES), U32),
                        pltpu.VMEM((2, d, f), BF16),
                        pltpu.VMEM((2, d, f), BF16),
                        pltpu.VMEM((2, f, d), BF16),
                        pltpu.VMEM((d // bsz, f), F32),
                        pltpu.VMEM((d // bsz, f), F32),
                        pltpu.VMEM((f // bsz, d), F32),
                        pltpu.SemaphoreType.DMA((2,)),
                        pltpu.SemaphoreType.DMA((3,))],
    )
    return pl.pallas_call(
        kern,
        grid_spec=grid_spec,
        out_shape=jax.ShapeDtypeStruct((bsz, n_e, cap, d), BF16),
        compiler_params=_params(2),
        name="ffn",
    )(idx_flat, h2p, wg, wu, wd, gates4)


def _combine_kernel(src0_ref, first0_ref, rounds_ref, y_hbm, x1_ref, rank_ref, sel_ref, gf_ref, o_ref,
                    ybuf, yextra, sem, *, cap, win, n_t, sub, final):
    b = pl.program_id(0)
    n = b * n_t + pl.program_id(1)
    n_tiles = pl.num_programs(0) * n_t
    slot = n % 2
    rows_all = N_EXPERTS * win

    def start_copies(srcs, dst_ref, sem_k):
        for e, src in enumerate(srcs):
            pltpu.make_async_copy(y_hbm.at[pl.ds(pl.multiple_of(src, BF16_ROWS), win), :],
                                  dst_ref.at[pl.ds(e * win, win), :], sem_k).start()

    def wait_copies(dst_ref, sem_k):
        pltpu.make_async_copy(y_hbm.at[pl.ds(0, rows_all), :], dst_ref, sem_k).wait()

    jcol = lax.broadcasted_iota(I32, (win, 1), 0)

    def expand(firsts, begins, rows_bf16):
        blocks = []
        for e in range(N_EXPERTS):
            hits = []
            for j in range(sub):
                rank = rank_ref[j, e:e + 1, :]
                hits.append(jnp.where(rank == jcol + begins[e],
                                      jnp.where(rank >= firsts[e], sel_ref[j, e:e + 1, :], 0.0), 0.0))
            blocks.append(jnp.concatenate(hits, axis=1))
        onehot = jnp.concatenate(blocks, axis=0).astype(BF16)
        return lax.dot_general(onehot, rows_bf16, TN, preferred_element_type=F32)

    @pl.when(n == 0)
    def _():
        start_copies([src0_ref[e] for e in range(N_EXPERTS)], ybuf.at[slot], sem.at[slot])

    @pl.when(n + 1 < n_tiles)
    def _():
        start_copies([src0_ref[(n + 1) * N_EXPERTS + e] for e in range(N_EXPERTS)],
                     ybuf.at[1 - slot], sem.at[1 - slot])

    bases = [(b * N_EXPERTS + e) * cap for e in range(N_EXPERTS)]
    firsts = [first0_ref[n * N_EXPERTS + e] for e in range(N_EXPERTS)]
    begins = [src0_ref[n * N_EXPERTS + e] - bases[e] for e in range(N_EXPERTS)]

    wait_copies(ybuf.at[slot], sem.at[slot])
    o_ref[...] = x1_ref[...] + expand(firsts, begins, ybuf[slot])

    def extra_round(k, carry):
        firsts_k = [f + k * win for f in firsts]
        begins_k = [jnp.minimum(f, cap - win) for f in firsts_k]
        start_copies([bases[e] + begins_k[e] for e in range(N_EXPERTS)], yextra, sem.at[2])
        wait_copies(yextra, sem.at[2])
        o_ref[...] += expand(firsts_k, begins_k, yextra[...])
        return carry
    lax.fori_loop(1, rounds_ref[n], extra_round, 0)

    if final:
        x2 = o_ref[...]
        ms = jnp.mean(x2 * x2, axis=-1, keepdims=True)
        o_ref[...] = x2 * lax.rsqrt(ms + EPS) * gf_ref[...]


def _combine(y2d, x1, rank_t, sel_t, g_final, bsz, s_len, cap, win, sub, final):
    t, d = x1.shape
    n_t = s_len // (sub * LANES)
    off = rank_t[:, ::sub, :, 0]
    cnt = jnp.concatenate([off[:, 1:], jnp.full((bsz, 1, N_EXPERTS), cap, I32)], axis=1) - off
    first0 = (off // BF16_ROWS) * BF16_ROWS
    base = (jnp.arange(bsz, dtype=I32)[:, None, None] * N_EXPERTS
            + jnp.arange(N_EXPERTS, dtype=I32)[None, None, :]) * cap
    src0 = base + jnp.minimum(first0, cap - win)
    rounds = jnp.max(jnp.where(cnt > 0, (off - first0 + cnt + win - 1) // win, 0), axis=2)

    kern = functools.partial(_combine_kernel, cap=cap, win=win, n_t=n_t, sub=sub, final=final)
    meta = pl.BlockSpec((None, sub, N_EXPERTS, LANES), lambda b, i, *_: (b, i, 0, 0))
    grid_spec = pltpu.PrefetchScalarGridSpec(
        num_scalar_prefetch=3,
        grid=(bsz, n_t),
        in_specs=[pl.BlockSpec(memory_space=pl.ANY),
                  pl.BlockSpec((sub * LANES, d), lambda b, i, *_: (b * n_t + i, 0)),
                  meta, meta,
                  pl.BlockSpec((1, d), lambda b, i, *_: (0, 0))],
        out_specs=pl.BlockSpec((sub * LANES, d), lambda b, i, *_: (b * n_t + i, 0)),
        scratch_shapes=[pltpu.VMEM((2, N_EXPERTS * win, d), BF16),
                        pltpu.VMEM((N_EXPERTS * win, d), BF16),
                        pltpu.SemaphoreType.DMA((3,))],
    )
    return pl.pallas_call(
        kern,
        grid_spec=grid_spec,
        out_shape=jax.ShapeDtypeStruct((t, d), F32),
        compiler_params=_params(2),
        name="combine",
    )(src0.reshape(-1), first0.reshape(-1), rounds.reshape(-1), y2d, x1, rank_t, sel_t, g_final)


def _rope_tables(s_len):
    rows = s_len // GRID_W
    half = HEAD_DIM // 2
    inv = 1.0 / (ROPE_THETA ** (jnp.arange(0, half, 2, dtype=F32) / half))
    ang_r = jnp.arange(rows, dtype=F32)[:, None] * inv
    ang_c = jnp.arange(GRID_W, dtype=F32)[:, None] * inv
    by_row = lambda a: jnp.repeat(a, GRID_W, axis=0)
    by_col = lambda a: jnp.tile(a, (rows, 1))
    cos_r, sin_r = by_row(jnp.cos(ang_r)), by_row(jnp.sin(ang_r))
    cos_c, sin_c = by_col(jnp.cos(ang_c)), by_col(jnp.sin(ang_c))
    cos = jnp.concatenate([cos_r, cos_r, cos_c, cos_c], axis=-1)
    sin = jnp.concatenate([-sin_r, sin_r, -sin_c, sin_c], axis=-1)
    return cos, sin


def _t5_bucket_table():
    rel = (jnp.arange(3 * BLOCK) - BLOCK)[None, :] - jnp.arange(BLOCK)[:, None]
    half = NUM_BUCKETS // 2
    ret = jnp.where(rel > 0, half, 0)
    n = jnp.abs(rel)
    max_exact = half // 2
    nf = jnp.maximum(n, 1).astype(F32)
    large = max_exact + (jnp.log(nf / max_exact) / math.log(MAX_DISTANCE / max_exact)
                         * (half - max_exact)).astype(I32)
    large = jnp.minimum(large, half - 1)
    return (ret + jnp.where(n < max_exact, n, large)).astype(I32)


class _Tiles(NamedTuple):
    inproj_rows: int
    inproj_cols: int
    attn_a_q: int
    attn_a_k: int
    attn_b_q: int
    outproj_rows: int
    ffn_rows: int
    combine_rows128: int
    combine_window: int


def _plan_tiles(bsz, s_len, d, cap):
    in_width = (7 * d) // 2
    tiles = _Tiles(inproj_rows=min(1024, bsz * s_len), inproj_cols=in_width // 4,
                   attn_a_q=min(512, s_len), attn_a_k=min(512, s_len), attn_b_q=min(1024, s_len),
                   outproj_rows=512, ffn_rows=min(512, cap // 2), combine_rows128=2, combine_window=64)
    assert in_width % tiles.inproj_cols == 0 and tiles.inproj_cols % (2 * LANES) == 0
    assert (bsz * s_len) % tiles.inproj_rows == 0 and s_len % tiles.outproj_rows == 0
    assert s_len % tiles.attn_a_q == 0 and s_len % tiles.attn_a_k == 0 and s_len % tiles.attn_b_q == 0
    assert s_len % (tiles.combine_rows128 * LANES) == 0 and cap % LANES == 0
    assert tiles.combine_window % BF16_ROWS == 0 and tiles.combine_window <= cap
    assert SAFE_EXP2_RANGE + math.log2(s_len) + 8 < 127
    return tiles


def kernel(x, g_mix, w_in, b_gate, qn_a, kn_a, w_proj_a, sink_b, rel_bias, w_proj_b, w_o, g_ffn, w_router,
           w_gate_e, w_up_e, w_down_e, g_final):
    bsz, s_len, d = x.shape
    depth = g_mix.shape[0]
    t = bsz * s_len
    cap = CAPACITY_FACTOR * s_len // N_EXPERTS
    n_r = s_len // LANES
    tiles = _plan_tiles(bsz, s_len, d, cap)
    cos, sin = _rope_tables(s_len)
    bucket = _t5_bucket_table()
    x2d = x.reshape(t, d)
    for l in range(depth):
        proj = _inproj(x2d, g_mix[l][None, :], w_in[l].astype(BF16), tm=tiles.inproj_rows, tn=tiles.inproj_cols)
        oa = _attn_a(proj, cos, sin, qn_a[l], kn_a[l], bsz, s_len, d, tq=tiles.attn_a_q, tk=tiles.attn_a_k)
        ob = _attn_b(proj, bucket, rel_bias, sink_b[l], bsz, s_len, d, tq=tiles.attn_b_q)
        x1, h2p, aff = _outproj(oa, ob, proj, x2d, b_gate[l][None, :], w_proj_a[l].astype(BF16),
                                w_proj_b[l].astype(BF16), w_o[l].astype(BF16), g_ffn[l][None, :],
                                _split_router(w_router[l]), bsz, s_len, tm=tiles.outproj_rows)
        idx, gates, rank, sel = _topk(aff.reshape(bsz, N_EXPERTS, n_r, LANES), cap, rpt=h2p.shape[0] // t)
        y = _ffn(idx.reshape(-1), h2p, gates.reshape(bsz, N_EXPERTS, cap // LANES, LANES),
                 w_gate_e[l], w_up_e[l], w_down_e[l], bsz, s_len, cap, ch=tiles.ffn_rows)
        x2d = _combine(y.reshape(bsz * N_EXPERTS * cap, d), x1, rank.reshape(bsz, n_r, N_EXPERTS, LANES),
                       sel.reshape(bsz, n_r, N_EXPERTS, LANES), g_final[None, :], bsz, s_len, cap,
                       win=tiles.combine_window, sub=tiles.combine_rows128, final=(l == depth - 1))
    return x2d.reshape(bsz, s_len, d)
```

```python
import functools
import math
from typing import NamedTuple

import jax
import jax.numpy as jnp
from jax import lax
from jax.experimental import pallas as pl
from jax.experimental.pallas import tpu as pltpu

F32 = jnp.float32
BF16 = jnp.bfloat16
I32 = jnp.int32
U32 = jnp.uint32

HEAD_DIM = 128
GROUP = 4
BLOCK = 128
GRID_W = 64
ROPE_THETA = 10000.0
NUM_BUCKETS = 32
MAX_DISTANCE = 128
N_EXPERTS = 16
CAPACITY_FACTOR = 2
EPS = 1e-6
NEG = -1e30
LANES = 128
BF16_ROWS = 16
ONES_ROWS = BF16_ROWS
SAFE_EXP2_RANGE = 100.0
ROW_CHAINS = 2
KV_UNROLL = 8
VMEM_LIMIT = 56 * 1024 * 1024
TN = (((0,), (0,)), ((), ()))


def _params(n_axes):
    return pltpu.CompilerParams(dimension_semantics=("arbitrary",) * n_axes,
                                vmem_limit_bytes=VMEM_LIMIT)


def _inproj_kernel(x_ref, g_ref, w_ref, o_ref, h_ref):
    @pl.when(pl.program_id(1) == 0)
    def _():
        x = x_ref[...]
        ms = jnp.mean(x * x, axis=-1, keepdims=True)
        h_ref[...] = (x * lax.rsqrt(ms + EPS) * g_ref[...]).astype(BF16)

    o_ref[...] = jnp.dot(h_ref[...], w_ref[...], preferred_element_type=F32).astype(o_ref.dtype)


def _inproj(x2d, g, w, tm, tn):
    t, d = x2d.shape
    n = w.shape[1]
    return pl.pallas_call(
        _inproj_kernel,
        grid=(t // tm, n // tn),
        in_specs=[pl.BlockSpec((tm, d), lambda i, j: (i, 0)),
                  pl.BlockSpec((1, d), lambda i, j: (0, 0)),
                  pl.BlockSpec((d, tn), lambda i, j: (0, j))],
        out_specs=pl.BlockSpec((tm, tn), lambda i, j: (i, j)),
        out_shape=jax.ShapeDtypeStruct((t, n), BF16),
        scratch_shapes=[pltpu.VMEM((tm, d), BF16)],
        compiler_params=_params(2),
        name="inproj",
    )(x2d, g, w)


def _norm_rope(xb, gain, gain_sw, cos, sin_signed):
    a = lax.broadcasted_iota(I32, (HEAD_DIM, HEAD_DIM), 0)
    c = lax.broadcasted_iota(I32, (HEAD_DIM, HEAD_DIM), 1)
    swap_mat = jnp.where(a == (c ^ 32), 1.0, 0.0).astype(BF16)
    ones_mat = jnp.ones((HEAD_DIM, HEAD_DIM), BF16)
    x = xb.astype(F32)
    sq = x * x
    sq_hi = sq.astype(BF16)
    sq_lo = (sq - sq_hi.astype(F32)).astype(BF16)
    ms = (jnp.dot(sq_hi, ones_mat, preferred_element_type=F32)
          + jnp.dot(sq_lo, ones_mat, preferred_element_type=F32)) * (1.0 / HEAD_DIM)
    swapped = jnp.dot(xb, swap_mat, preferred_element_type=F32)
    return lax.rsqrt(ms + EPS) * (x * (gain * cos) + swapped * (gain_sw * sin_signed))


def _attn_a_kernel(q_ref, k_ref, v_ref, cq_ref, sq_ref, ck_ref, sk_ref, qn_ref, kn_ref, o_ref,
                   kp_ref, vt_ref, qt_ref, m_ref, acc_ref, st0_ref, st1_ref, pt0_ref, pt1_ref,
                   *, tq, tk):
    s_len = k_ref.shape[0]
    n_chunks = s_len // tk

    @pl.when(pl.program_id(2) == 0)
    def _():
        def body(c, carry):
            off = pl.multiple_of(c * tk, tk)
            kb = _norm_rope(k_ref[pl.ds(off, tk), :], kn_ref[0:1, :], kn_ref[1:2, :],
                            ck_ref[pl.ds(off, tk), :], sk_ref[pl.ds(off, tk), :])
            kp_ref[pl.ds(off, tk), :] = kb.astype(BF16)
            vt_ref[c, :HEAD_DIM, :] = v_ref[pl.ds(off, tk), :].astype(F32).T.astype(BF16)
            vt_ref[c, HEAD_DIM:, :] = jnp.ones((ONES_ROWS, tk), BF16)
            return carry
        lax.fori_loop(0, n_chunks, body, 0)

    scale = math.log2(math.e) / math.sqrt(HEAD_DIM)
    for g in range(GROUP):
        qg = _norm_rope(q_ref[:, g * HEAD_DIM:(g + 1) * HEAD_DIM], qn_ref[0:1, :], qn_ref[1:2, :],
                        cq_ref[...], sq_ref[...]) * scale
        qt_ref[g] = qg.T.astype(BF16)
    q_gain = jnp.max(jnp.abs(qn_ref[0:1, :]))
    k_gain = jnp.max(jnp.abs(kn_ref[0:1, :]))
    bound_sq = 1.02 * (HEAD_DIM * scale * scale) * HEAD_DIM * (q_gain * q_gain) * (k_gain * k_gain)
    no_max_needed = bound_sq <= SAFE_EXP2_RANGE * SAFE_EXP2_RANGE

    acc_ref[...] = jnp.zeros(acc_ref.shape, F32)
    unroll = math.gcd(KV_UNROLL, n_chunks)

    @pl.when(no_max_needed)
    def _():
        m_ref[...] = jnp.zeros(m_ref.shape, F32)

        def probs(c, dst_ref):
            kc = kp_ref[pl.ds(pl.multiple_of(c * tk, tk), tk), :]
            for g in range(GROUP):
                p = jnp.exp2(jnp.dot(kc, qt_ref[g], preferred_element_type=F32))
                m_ref[g] += jnp.sum(p, axis=0, keepdims=True)
                dst_ref[g] = p.astype(BF16)

        def weighted_sum(c, src_ref):
            vt = vt_ref[c, :HEAD_DIM, :]
            for g in range(GROUP):
                acc_ref[g, :HEAD_DIM, :] += jnp.dot(vt, src_ref[g], preferred_element_type=F32)

        probs(0, pt0_ref)

        def trip(j, last):
            bufs = (pt0_ref, pt1_ref)
            for u in range(unroll):
                c = unroll * j + u
                if not (last and u == unroll - 1):
                    probs(c + 1, bufs[(u + 1) % 2])
                weighted_sum(c, bufs[u % 2])

        def kv_body(j, carry):
            trip(j, False)
            return carry
        n_trips = n_chunks // unroll
        lax.fori_loop(0, n_trips - 1, kv_body, 0)
        trip(n_trips - 1, True)
        for g in range(GROUP):
            acc_ref[g, HEAD_DIM:HEAD_DIM + 1, :] = m_ref[g]

    @pl.when(jnp.logical_not(no_max_needed))
    def _():
        m_ref[...] = jnp.full(m_ref.shape, -jnp.inf, F32)

        def scores(c, dst_ref):
            kc = kp_ref[pl.ds(pl.multiple_of(c * tk, tk), tk), :]
            for g in range(GROUP):
                dst_ref[g] = jnp.dot(kc, qt_ref[g], preferred_element_type=F32)

        def softmax_pv(c, src_ref):
            vt = vt_ref[c]
            for g in range(GROUP):
                st = src_ref[g]
                m_prev = m_ref[g]
                m_new = jnp.maximum(m_prev, jnp.max(st, axis=0, keepdims=True))
                pt = jnp.exp2(st - m_new).astype(BF16)
                acc_ref[g] = (jnp.exp2(m_prev - m_new) * acc_ref[g]
                              + jnp.dot(vt, pt, preferred_element_type=F32))
                m_ref[g] = m_new

        scores(0, st0_ref)

        def kv_body(j, carry):
            c = 2 * j
            scores(c + 1, st1_ref)
            softmax_pv(c, st0_ref)
            scores(jnp.minimum(c + 2, n_chunks - 1), st0_ref)
            softmax_pv(c + 1, st1_ref)
            return carry
        lax.fori_loop(0, n_chunks // 2, kv_body, 0)

    for g in range(GROUP):
        acc = acc_ref[g]
        o = (acc[:HEAD_DIM] / acc[HEAD_DIM:HEAD_DIM + 1]).T
        o_ref[:, g * HEAD_DIM:(g + 1) * HEAD_DIM] = o.astype(o_ref.dtype)


def _attn_a(proj, cos, sin, qn, kn, bsz, s_len, d, tq, tk):
    q_w = d // 2
    kv_w = d // 8
    kvh = kv_w // HEAD_DIM
    gw = GROUP * HEAD_DIM
    nq = s_len // tq
    k_blk = q_w // HEAD_DIM
    v_blk = (q_w + kv_w) // HEAD_DIM
    assert (s_len // tk) % 2 == 0

    def with_swapped(gain):
        return jnp.stack([gain, gain.reshape(2, 2, 32)[:, ::-1, :].reshape(HEAD_DIM)])

    qn, kn = with_swapped(qn), with_swapped(kn)
    kern = functools.partial(_attn_a_kernel, tq=tq, tk=tk)
    return pl.pallas_call(
        kern,
        grid=(bsz, kvh, nq),
        in_specs=[pl.BlockSpec((tq, gw), lambda b, k, i: (b * nq + i, k)),
                  pl.BlockSpec((s_len, HEAD_DIM), lambda b, k, i: (b, k_blk + k)),
                  pl.BlockSpec((s_len, HEAD_DIM), lambda b, k, i: (b, v_blk + k)),
                  pl.BlockSpec((tq, HEAD_DIM), lambda b, k, i: (i, 0)),
                  pl.BlockSpec((tq, HEAD_DIM), lambda b, k, i: (i, 0)),
                  pl.BlockSpec((s_len, HEAD_DIM), lambda b, k, i: (0, 0)),
                  pl.BlockSpec((s_len, HEAD_DIM), lambda b, k, i: (0, 0)),
                  pl.BlockSpec((2, HEAD_DIM), lambda b, k, i: (0, 0)),
                  pl.BlockSpec((2, HEAD_DIM), lambda b, k, i: (0, 0))],
        out_specs=pl.BlockSpec((tq, gw), lambda b, k, i: (b * nq + i, k)),
        out_shape=jax.ShapeDtypeStruct((bsz * s_len, q_w), BF16),
        scratch_shapes=[pltpu.VMEM((s_len, HEAD_DIM), BF16),
                        pltpu.VMEM((s_len // tk, HEAD_DIM + ONES_ROWS, tk), BF16),
                        pltpu.VMEM((GROUP, HEAD_DIM, tq), BF16),
                        pltpu.VMEM((GROUP, 1, tq), F32),
                        pltpu.VMEM((GROUP, HEAD_DIM + ONES_ROWS, tq), F32),
                        pltpu.VMEM((GROUP, tk, tq), F32),
                        pltpu.VMEM((GROUP, tk, tq), F32),
                        pltpu.VMEM((GROUP, tk, tq), BF16),
                        pltpu.VMEM((GROUP, tk, tq), BF16)],
        compiler_params=_params(3),
        name="attn_a",
    )(proj, proj, proj, cos, sin, cos, sin, qn, kn)


def _attn_b_kernel(relb_ref, sink_ref, bucket_ref, q_ref, kp_ref, kc_ref, kn_ref, vp_ref, vc_ref, vn_ref,
                   o_ref, bias_ref, st_ref, *, tq, kvh):
    b = pl.program_id(0)
    k = pl.program_id(1)
    i = pl.program_id(2)
    band = 3 * BLOCK
    log2e = math.log2(math.e)

    @pl.when((b == 0) & (k == 0) & (i == 0))
    def _():
        bucket_t = bucket_ref[...]
        key = lax.broadcasted_iota(I32, (band, BLOCK), 0)
        qry = lax.broadcasted_iota(I32, (band, BLOCK), 1)
        in_window = jnp.abs(key - BLOCK - qry) <= BLOCK
        for kk in range(kvh):
            for g in range(GROUP):
                h = kk * GROUP + g
                tab = jnp.zeros((band, BLOCK), F32)
                for bkt in range(NUM_BUCKETS):
                    tab = jnp.where(bucket_t == bkt, relb_ref[bkt, h], tab)
                tab = jnp.where(in_window, tab * log2e, NEG)
                cols = slice(g * BLOCK, (g + 1) * BLOCK)
                bias_ref[0, kk, :, cols] = tab
                bias_ref[1, kk, :, cols] = jnp.where(key >= BLOCK, tab, NEG)
                bias_ref[2, kk, :, cols] = jnp.where(key < 2 * BLOCK, tab, NEG)

    nsub = tq // BLOCK
    kcat = jnp.concatenate([kp_ref[...], kc_ref[...], kn_ref[...]], axis=0)
    vcat = jnp.concatenate([vp_ref[...], vc_ref[...], vn_ref[...]], axis=0)
    vt = jnp.concatenate([vcat.astype(F32).T.astype(BF16), jnp.ones((ONES_ROWS, vcat.shape[0]), BF16)], axis=0)
    sink = jnp.concatenate([jnp.full((1, BLOCK), sink_ref[k * GROUP + g], F32) for g in range(GROUP)],
                           axis=1) * log2e
    scale = log2e / math.sqrt(HEAD_DIM)
    for jb in range(nsub):
        rows = slice(jb * BLOCK, (jb + 1) * BLOCK)
        qt = jnp.concatenate([q_ref[rows, g * HEAD_DIM:(g + 1) * HEAD_DIM].astype(F32).T.astype(BF16)
                              for g in range(GROUP)], axis=1)
        if jb == 0:
            variant = jnp.where(i == 0, 1, 0)
        elif jb == nsub - 1:
            variant = jnp.where(i == pl.num_programs(2) - 1, 2, 0)
        else:
            variant = 0
        st_ref[jb] = (jnp.dot(kcat[jb * BLOCK:jb * BLOCK + band], qt, preferred_element_type=F32) * scale
                      + bias_ref[variant, k])
    for jb in range(nsub):
        rows = slice(jb * BLOCK, (jb + 1) * BLOCK)
        st = st_ref[jb]
        m = jnp.maximum(jnp.max(st, axis=0, keepdims=True), sink)
        e = jnp.exp2(st - m).astype(BF16)
        acc = jnp.dot(vt[:, jb * BLOCK:jb * BLOCK + band], e, preferred_element_type=F32)
        o = acc[:HEAD_DIM] / (acc[HEAD_DIM:HEAD_DIM + 1] + jnp.exp2(sink - m))
        for g in range(GROUP):
            o_ref[rows, g * HEAD_DIM:(g + 1) * HEAD_DIM] = o[:, g * BLOCK:(g + 1) * BLOCK].T.astype(o_ref.dtype)


def _attn_b(proj, bucket, rel_bias, sink, bsz, s_len, d, tq):
    q_w = d // 2
    kv_w = d // 8
    kvh = kv_w // HEAD_DIM
    gw = GROUP * HEAD_DIM
    nq = s_len // tq
    sub = tq // BLOCK
    nblk = s_len // BLOCK
    q_blk = (q_w + 2 * kv_w) // gw
    k_blk = (2 * q_w + 2 * kv_w) // HEAD_DIM
    v_blk = (2 * q_w + 3 * kv_w) // HEAD_DIM

    def prev_map(col):
        return lambda b, k, i: (b * nblk + jnp.maximum(i * sub - 1, 0), col + k)

    def cur_map(col):
        return lambda b, k, i: (b * nq + i, col + k)

    def next_map(col):
        return lambda b, k, i: (b * nblk + jnp.minimum((i + 1) * sub, nblk - 1), col + k)

    small = (BLOCK, HEAD_DIM)
    assert sub >= 2
    kern = functools.partial(_attn_b_kernel, tq=tq, kvh=kvh)
    return pl.pallas_call(
        kern,
        grid=(bsz, kvh, nq),
        in_specs=[pl.BlockSpec(memory_space=pltpu.SMEM),
                  pl.BlockSpec(memory_space=pltpu.SMEM),
                  pl.BlockSpec((3 * BLOCK, BLOCK), lambda b, k, i: (0, 0)),
                  pl.BlockSpec((tq, gw), lambda b, k, i: (b * nq + i, q_blk + k)),
                  pl.BlockSpec(small, prev_map(k_blk)),
                  pl.BlockSpec((tq, HEAD_DIM), cur_map(k_blk)),
                  pl.BlockSpec(small, next_map(k_blk)),
                  pl.BlockSpec(small, prev_map(v_blk)),
                  pl.BlockSpec((tq, HEAD_DIM), cur_map(v_blk)),
                  pl.BlockSpec(small, next_map(v_blk))],
        out_specs=pl.BlockSpec((tq, gw), lambda b, k, i: (b * nq + i, k)),
        out_shape=jax.ShapeDtypeStruct((bsz * s_len, q_w), BF16),
        scratch_shapes=[pltpu.VMEM((3, kvh, 3 * BLOCK, GROUP * BLOCK), F32),
                        pltpu.VMEM((sub, 3 * BLOCK, GROUP * BLOCK), F32)],
        compiler_params=_params(3),
        name="attn_b",
    )(rel_bias, sink, bucket.T, proj, proj, proj, proj, proj, proj, proj)


def _outproj_kernel(oa_ref, ob_ref, ga0_ref, ga1_ref, gb0_ref, gb1_ref, x_ref, bg_ref, wpa_ref, wpb_ref,
                    wo_ref, gf_ref, wr_ref, x1_ref, h2p_ref, aff_ref):
    tm, d = x_ref.shape
    half = d // 2
    rpt = half // LANES
    rn = tm // ROW_CHAINS
    for rc in range(ROW_CHAINS):
        rows = slice(rc * rn, (rc + 1) * rn)
        oa = oa_ref[rows, :]
        ob = ob_ref[rows, :]
        parts = []
        for c, (ga_ref, gb_ref) in enumerate(((ga0_ref, gb0_ref), (ga1_ref, gb1_ref))):
            cols = slice(c * half, (c + 1) * half)
            pa = jnp.dot(oa, wpa_ref[:, cols], preferred_element_type=F32)
            pb = jnp.dot(ob, wpb_ref[:, cols], preferred_element_type=F32)
            gate_a = jax.nn.sigmoid(ga_ref[rows, :].astype(F32) + bg_ref[:, cols])
            gate_b = jax.nn.sigmoid(gb_ref[rows, :].astype(F32) + bg_ref[:, d + c * half:d + (c + 1) * half])
            parts.append((gate_a * pa + gate_b * pb).astype(BF16))
        x1 = x_ref[rows, :] + (jnp.dot(parts[0], wo_ref[:half, :], preferred_element_type=F32)
                               + jnp.dot(parts[1], wo_ref[half:, :], preferred_element_type=F32))
        x1_ref[rows, :] = x1
        ms = jnp.mean(x1 * x1, axis=-1, keepdims=True)
        h = x1 * lax.rsqrt(ms + EPS) * gf_ref[...]
        h_hi = h.astype(BF16)
        bits = pltpu.bitcast(h_hi.astype(F32), U32)
        packed = (bits[:, :half] >> 16) | (bits[:, half:] & jnp.uint32(0xFFFF0000))
        for s in range(rpt):
            h2p_ref[pl.ds(rc * rn * rpt + s, rn, stride=rpt), :] = packed[:, s * LANES:(s + 1) * LANES]
        h_lo = (h - h_hi.astype(F32)).astype(BF16)
        prod = (jnp.dot(h_hi, wr_ref[...], preferred_element_type=F32)
                + jnp.dot(h_lo, wr_ref[...], preferred_element_type=F32))
        logits = (prod[:, :LANES] + prod[:, LANES:]).T[:N_EXPERTS, :]
        ex = jnp.exp(logits - jnp.max(logits, axis=0, keepdims=True))
        aff_ref[:, rows] = ex / jnp.sum(ex, axis=0, keepdims=True)


def _split_router(w):
    w_hi = w.astype(BF16)
    w_lo = (w - w_hi.astype(F32)).astype(BF16)
    pad = ((0, 0), (0, LANES - w.shape[1]))
    return jnp.concatenate([jnp.pad(w_hi, pad), jnp.pad(w_lo, pad)], axis=1)


def _outproj(oa, ob, proj, x2d, b_gate, wpa, wpb, wo, g_ffn, wr_split, bsz, s_len, tm):
    t, d = x2d.shape
    half = d // 2
    nt = s_len // tm
    row = lambda c: (lambda i: (i, c))
    const = lambda i: (0, 0)
    return pl.pallas_call(
        _outproj_kernel,
        grid=(t // tm,),
        in_specs=[pl.BlockSpec((tm, half), row(0)),
                  pl.BlockSpec((tm, half), row(0)),
                  pl.BlockSpec((tm, half), row(3)),
                  pl.BlockSpec((tm, half), row(4)),
                  pl.BlockSpec((tm, half), row(5)),
                  pl.BlockSpec((tm, half), row(6)),
                  pl.BlockSpec((tm, d), row(0)),
                  pl.BlockSpec((1, 2 * d), const),
                  pl.BlockSpec((half, d), const, pipeline_mode=pl.Buffered(1)),
                  pl.BlockSpec((half, d), const, pipeline_mode=pl.Buffered(1)),
                  pl.BlockSpec((d, d), const, pipeline_mode=pl.Buffered(1)),
                  pl.BlockSpec((1, d), const),
                  pl.BlockSpec((d, 2 * LANES), const, pipeline_mode=pl.Buffered(1))],
        out_specs=[pl.BlockSpec((tm, d), row(0)),
                   pl.BlockSpec((tm * (half // LANES), LANES), row(0)),
                   pl.BlockSpec((None, N_EXPERTS, tm), lambda i: (i // nt, 0, i % nt))],
        out_shape=[jax.ShapeDtypeStruct((t, d), F32),
                   jax.ShapeDtypeStruct((t * (half // LANES), LANES), U32),
                   jax.ShapeDtypeStruct((bsz, N_EXPERTS, s_len), F32)],
        compiler_params=_params(1),
        name="outproj",
    )(oa, ob, proj, proj, proj, proj, x2d, b_gate, wpa, wpb, wo, g_ffn, wr_split)


def _topk_kernel(a_ref, idx_ref, gate_ref, rank_ref, sel_ref, *, cap, rpt):
    a = a_ref[...]
    n_e, n_r, _ = a.shape
    bits = pltpu.bitcast(a, I32)

    def total(x):
        return jnp.sum(jnp.sum(x, axis=1, keepdims=True), axis=2, keepdims=True)

    def search(_, carry):
        lo, hi = carry
        mid = lo + ((hi - lo) >> 1)
        enough = total(jnp.where(bits >= mid, 1.0, 0.0)) >= cap
        return jnp.where(enough, mid, lo), jnp.where(enough, hi, mid)

    lo0 = jnp.zeros((n_e, 1, 1), I32)
    hi0 = jnp.full((n_e, 1, 1), 0x7F800000, I32)
    thr, _ = lax.fori_loop(0, 31, search, (lo0, hi0))

    ri = lax.broadcasted_iota(I32, (LANES, LANES), 0)
    ci = lax.broadcasted_iota(I32, (LANES, LANES), 1)
    upper = jnp.where(ri <= ci, 1.0, 0.0).astype(BF16)
    ones = jnp.ones((LANES, LANES), BF16)
    rr = lax.broadcasted_iota(I32, (n_r, n_r), 0)
    rc = lax.broadcasted_iota(I32, (n_r, n_r), 1)
    strict_lower = jnp.where(rc < rr, 1.0, 0.0).astype(BF16)

    def prefix(x):
        x2 = x.reshape(n_e * n_r, LANES).astype(BF16)
        in_row = jnp.dot(x2, upper, preferred_element_type=F32).reshape(n_e, n_r, LANES)
        row_tot = jnp.dot(x2, ones, preferred_element_type=F32).reshape(n_e, n_r, LANES)
        before = jnp.stack([jnp.dot(strict_lower, row_tot[e].astype(BF16), preferred_element_type=F32)
                            for e in range(n_e)], axis=0)
        return in_row + before

    above = jnp.where(bits > thr, 1.0, 0.0)
    tied = jnp.where(bits == thr, 1.0, 0.0)
    need = cap - total(above)
    sel = above + tied * jnp.where(prefix(tied) <= need, 1.0, 0.0)
    cum = prefix(sel)
    rank = (cum - sel).astype(I32)
    for e in range(n_e):
        sel_ref[pl.ds(e, n_r, stride=n_e), :] = sel[e]
        rank_ref[pl.ds(e, n_r, stride=n_e), :] = rank[e]

    slot = lax.broadcasted_iota(I32, (1, cap), 1).astype(F32)
    row_id = lax.broadcasted_iota(I32, (n_r, 1), 0).astype(F32)
    lane_id = lax.broadcasted_iota(I32, (LANES, 1), 0).astype(F32)
    for e in range(n_e):
        c = cum[e]
        row_end = c[:, LANES - 1:LANES]
        srow = jnp.sum(jnp.where(row_end <= slot, 1.0, 0.0), axis=0, keepdims=True)
        pick = jnp.where(row_id == srow, 1.0, 0.0).astype(BF16)
        c_hi = jnp.floor(c * (1.0 / 32.0))
        c_lo = c - 32.0 * c_hi
        a1 = a[e].astype(BF16)
        r1 = a[e] - a1.astype(F32)
        a2 = r1.astype(BF16)
        a3 = (r1 - a2.astype(F32)).astype(BF16)
        parts = jnp.concatenate([c_hi.astype(BF16), c_lo.astype(BF16), a1, a2, a3], axis=1)
        rows = lax.dot_general(parts, pick, TN, preferred_element_type=F32)
        c_row = 32.0 * rows[:LANES] + rows[LANES:2 * LANES]
        a_row = (rows[2 * LANES:3 * LANES] + rows[3 * LANES:4 * LANES]) + rows[4 * LANES:]
        slane = jnp.sum(jnp.where(c_row <= slot, 1.0, 0.0), axis=0, keepdims=True)
        token = (srow * LANES + slane).astype(I32) + pl.program_id(0) * (n_r * LANES)
        idx_ref[e:e + 1, :] = token * rpt
        gate_ref[e:e + 1, :] = jnp.sum(jnp.where(lane_id == slane, a_row, 0.0), axis=0, keepdims=True)


def _topk(aff4, cap, rpt):
    bsz, n_e, n_r, _ = aff4.shape
    blk4 = pl.BlockSpec((None, n_e, n_r, LANES), lambda b: (b, 0, 0, 0))
    blk3 = pl.BlockSpec((None, n_e, cap), lambda b: (b, 0, 0))
    tiles = pl.BlockSpec((None, n_r * n_e, LANES), lambda b: (b, 0, 0))
    return pl.pallas_call(
        functools.partial(_topk_kernel, cap=cap, rpt=rpt),
        grid=(bsz,),
        in_specs=[blk4],
        out_specs=[blk3, blk3, tiles, tiles],
        out_shape=[jax.ShapeDtypeStruct((bsz, n_e, cap), I32),
                   jax.ShapeDtypeStruct((bsz, n_e, cap), F32),
                   jax.ShapeDtypeStruct((bsz, n_r * n_e, LANES), I32),
                   jax.ShapeDtypeStruct((bsz, n_r * n_e, LANES), F32)],
        compiler_params=_params(1),
        name="topk",
    )(aff4)


def _ffn_kernel(idx_ref, h2_hbm, wg_hbm, wu_hbm, wd_hbm, gate_ref, y_ref, xa, xb, wg_ref, wu_ref, wd_ref,
                stg_g, stg_u, stg_d, sem, wsem, *, bsz, cap, ch, rpt, unroll):
    n_c = cap // ch
    bufs = (xa, xb)
    e = pl.program_id(0)
    b = pl.program_id(1)
    pair = e * bsz + b
    n_pairs = pl.num_programs(0) * bsz
    slot = e % 2
    rows_gu = stg_g.shape[0]
    rows_d = stg_d.shape[0]

    def weight_copies(ee, part):
        r_gu = pl.ds(pl.multiple_of(part * rows_gu, rows_gu), rows_gu)
        r_d = pl.ds(pl.multiple_of(part * rows_d, rows_d), rows_d)
        return (pltpu.make_async_copy(wg_hbm.at[ee, r_gu, :], stg_g, wsem.at[0]),
                pltpu.make_async_copy(wu_hbm.at[ee, r_gu, :], stg_u, wsem.at[1]),
                pltpu.make_async_copy(wd_hbm.at[ee, r_d, :], stg_d, wsem.at[2]))

    def convert(to_slot, part):
        r_gu = pl.ds(pl.multiple_of(part * rows_gu, rows_gu), rows_gu)
        r_d = pl.ds(pl.multiple_of(part * rows_d, rows_d), rows_d)
        wg_ref[to_slot, r_gu, :] = stg_g[...].astype(BF16)
        wu_ref[to_slot, r_gu, :] = stg_u[...].astype(BF16)
        wd_ref[to_slot, r_d, :] = stg_d[...].astype(BF16)

    @pl.when(pair == 0)
    def _():
        for part in range(bsz):
            copies = weight_copies(0, part)
            for cp in copies:
                cp.start()
            for cp in copies:
                cp.wait()
            convert(0, part)

    @pl.when(e + 1 < pl.num_programs(0))
    def _():
        for cp in weight_copies(e + 1, b):
            cp.start()

    def issue(pair_k, c):
        bb = pair_k % bsz
        ee = pair_k // bsz
        idx_base = (bb * N_EXPERTS + ee) * cap + c * ch
        buf = bufs[c % 2]

        def body(j, carry):
            for u in range(unroll):
                r = j * unroll + u
                src = pl.multiple_of(idx_ref[idx_base + r], rpt)
                dst = pl.multiple_of(r * rpt, rpt)
                pltpu.make_async_copy(h2_hbm.at[pl.ds(src, rpt), :], buf.at[pl.ds(dst, rpt), :],
                                      sem.at[c % 2]).start()
            return carry
        lax.fori_loop(0, ch // unroll, body, 0)

    @pl.when(pair == 0)
    def _():
        issue(pair, 0)

    for c in range(n_c):
        if c + 1 < n_c:
            issue(pair, c + 1)
        else:
            @pl.when(pair + 1 < n_pairs)
            def _():
                issue(pair + 1, 0)
        buf = bufs[c % 2]
        pltpu.make_async_copy(h2_hbm.at[pl.ds(0, ch * rpt), :], buf, sem.at[c % 2]).wait()
        lo, hi = [], []
        for s in range(rpt):
            w = buf[pl.ds(s, ch, stride=rpt), :]
            lo.append(pltpu.bitcast(w << 16, F32).astype(BF16))
            hi.append(pltpu.bitcast(w & jnp.uint32(0xFFFF0000), F32).astype(BF16))
        x = jnp.concatenate(lo + hi, axis=1)
        a = jnp.dot(x, wg_ref[slot], preferred_element_type=F32)
        u = jnp.dot(x, wu_ref[slot], preferred_element_type=F32)
        act = (jax.nn.silu(a) * u).astype(BF16)
        y = jnp.dot(act, wd_ref[slot], preferred_element_type=F32)
        for blk in range(ch // LANES):
            first = c * ch + blk * LANES
            gcol = jnp.broadcast_to(gate_ref[first // LANES:first // LANES + 1, :], (LANES, LANES)).T
            y_ref[first:first + LANES, :] = (
                y[blk * LANES:(blk + 1) * LANES, :]
                * jnp.concatenate([gcol] * (y.shape[1] // LANES), axis=1)).astype(y_ref.dtype)

    @pl.when(e + 1 < pl.num_programs(0))
    def _():
        for cp in weight_copies(e + 1, b):
            cp.wait()
        convert(1 - slot, b)


def _ffn(idx_flat, h2p, gates4, wg, wu, wd, bsz, s_len, cap, ch):
    n_e, d, f = wg.shape
    rpt = h2p.shape[0] // (bsz * s_len)
    assert (cap // ch) % 2 == 0 and ch % LANES == 0 and d % bsz == 0 and f % bsz == 0
    kern = functools.partial(_ffn_kernel, bsz=bsz, cap=cap, ch=ch, rpt=rpt, unroll=8)
    grid_spec = pltpu.PrefetchScalarGridSpec(
        num_scalar_prefetch=1,
        grid=(n_e, bsz),
        in_specs=[pl.BlockSpec(memory_space=pl.ANY),
                  pl.BlockSpec(memory_space=pl.ANY),
                  pl.BlockSpec(memory_space=pl.ANY),
                  pl.BlockSpec(memory_space=pl.ANY),
                  pl.BlockSpec((None, None, cap // LANES, LANES), lambda e, b, idx: (b, e, 0, 0))],
        out_specs=pl.BlockSpec((None, None, cap, d), lambda e, b, idx: (b, e, 0, 0)),
        scratch_shapes=[pltpu.VMEM((ch * rpt, LANES), U32),
                        pltpu.VMEM((ch * rpt, LANES), U32),
                        pltpu.VMEM((2, d, f), BF16),
                        pltpu.VMEM((2, d, f), BF16),
                        pltpu.VMEM((2, f, d), BF16),
                        pltpu.VMEM((d // bsz, f), F32),
                        pltpu.VMEM((d // bsz, f), F32),
                        pltpu.VMEM((f // bsz, d), F32),
                        pltpu.SemaphoreType.DMA((2,)),
                        pltpu.SemaphoreType.DMA((3,))],
    )
    return pl.pallas_call(
        kern,
        grid_spec=grid_spec,
        out_shape=jax.ShapeDtypeStruct((bsz, n_e, cap, d), BF16),
        compiler_params=_params(2),
        name="ffn",
    )(idx_flat, h2p, wg, wu, wd, gates4)


def _combine_kernel(src0_ref, first0_ref, rounds_ref, y_hbm, x1_ref, rank_ref, sel_ref, gf_ref, o_ref,
                    ybuf, yextra, sem, *, cap, win, n_t, sub, final):
    b = pl.program_id(0)
    n = b * n_t + pl.program_id(1)
    n_tiles = pl.num_programs(0) * n_t
    slot = n % 2
    rows_all = N_EXPERTS * win

    def start_copies(srcs, dst_ref, sem_k):
        for e, src in enumerate(srcs):
            pltpu.make_async_copy(y_hbm.at[pl.ds(pl.multiple_of(src, BF16_ROWS), win), :],
                                  dst_ref.at[pl.ds(e * win, win), :], sem_k).start()

    def wait_copies(dst_ref, sem_k):
        pltpu.make_async_copy(y_hbm.at[pl.ds(0, rows_all), :], dst_ref, sem_k).wait()

    jcol = lax.broadcasted_iota(I32, (win, 1), 0)

    def expand(firsts, begins, rows_bf16):
        blocks = []
        for e in range(N_EXPERTS):
            hits = []
            for j in range(sub):
                rank = rank_ref[j, e:e + 1, :]
                hits.append(jnp.where(rank == jcol + begins[e],
                                      jnp.where(rank >= firsts[e], sel_ref[j, e:e + 1, :], 0.0), 0.0))
            blocks.append(jnp.concatenate(hits, axis=1))
        onehot = jnp.concatenate(blocks, axis=0).astype(BF16)
        return lax.dot_general(onehot, rows_bf16, TN, preferred_element_type=F32)

    @pl.when(n == 0)
    def _():
        start_copies([src0_ref[e] for e in range(N_EXPERTS)], ybuf.at[slot], sem.at[slot])

    @pl.when(n + 1 < n_tiles)
    def _():
        start_copies([src0_ref[(n + 1) * N_EXPERTS + e] for e in range(N_EXPERTS)],
                     ybuf.at[1 - slot], sem.at[1 - slot])

    bases = [(b * N_EXPERTS + e) * cap for e in range(N_EXPERTS)]
    firsts = [first0_ref[n * N_EXPERTS + e] for e in range(N_EXPERTS)]
    begins = [src0_ref[n * N_EXPERTS + e] - bases[e] for e in range(N_EXPERTS)]

    wait_copies(ybuf.at[slot], sem.at[slot])
    o_ref[...] = x1_ref[...] + expand(firsts, begins, ybuf[slot])

    def extra_round(k, carry):
        firsts_k = [f + k * win for f in firsts]
        begins_k = [jnp.minimum(f, cap - win) for f in firsts_k]
        start_copies([bases[e] + begins_k[e] for e in range(N_EXPERTS)], yextra, sem.at[2])
        wait_copies(yextra, sem.at[2])
        o_ref[...] += expand(firsts_k, begins_k, yextra[...])
        return carry
    lax.fori_loop(1, rounds_ref[n], extra_round, 0)

    if final:
        x2 = o_ref[...]
        ms = jnp.mean(x2 * x2, axis=-1, keepdims=True)
        o_ref[...] = x2 * lax.rsqrt(ms + EPS) * gf_ref[...]


def _combine(y2d, x1, rank_t, sel_t, g_final, bsz, s_len, cap, win, sub, final):
    t, d = x1.shape
    n_t = s_len // (sub * LANES)
    off = rank_t[:, ::sub, :, 0]
    cnt = jnp.concatenate([off[:, 1:], jnp.full((bsz, 1, N_EXPERTS), cap, I32)], axis=1) - off
    first0 = (off // BF16_ROWS) * BF16_ROWS
    base = (jnp.arange(bsz, dtype=I32)[:, None, None] * N_EXPERTS
            + jnp.arange(N_EXPERTS, dtype=I32)[None, None, :]) * cap
    src0 = base + jnp.minimum(first0, cap - win)
    rounds = jnp.max(jnp.where(cnt > 0, (off - first0 + cnt + win - 1) // win, 0), axis=2)

    kern = functools.partial(_combine_kernel, cap=cap, win=win, n_t=n_t, sub=sub, final=final)
    meta = pl.BlockSpec((None, sub, N_EXPERTS, LANES), lambda b, i, *_: (b, i, 0, 0))
    grid_spec = pltpu.PrefetchScalarGridSpec(
        num_scalar_prefetch=3,
        grid=(bsz, n_t),
        in_specs=[pl.BlockSpec(memory_space=pl.ANY),
                  pl.BlockSpec((sub * LANES, d), lambda b, i, *_: (b * n_t + i, 0)),
                  meta, meta,
                  pl.BlockSpec((1, d), lambda b, i, *_: (0, 0))],
        out_specs=pl.BlockSpec((sub * LANES, d), lambda b, i, *_: (b * n_t + i, 0)),
        scratch_shapes=[pltpu.VMEM((2, N_EXPERTS * win, d), BF16),
                        pltpu.VMEM((N_EXPERTS * win, d), BF16),
                        pltpu.SemaphoreType.DMA((3,))],
    )
    return pl.pallas_call(
        kern,
        grid_spec=grid_spec,
        out_shape=jax.ShapeDtypeStruct((t, d), F32),
        compiler_params=_params(2),
        name="combine",
    )(src0.reshape(-1), first0.reshape(-1), rounds.reshape(-1), y2d, x1, rank_t, sel_t, g_final)


def _rope_tables(s_len):
    rows = s_len // GRID_W
    half = HEAD_DIM // 2
    inv = 1.0 / (ROPE_THETA ** (jnp.arange(0, half, 2, dtype=F32) / half))
    ang_r = jnp.arange(rows, dtype=F32)[:, None] * inv
    ang_c = jnp.arange(GRID_W, dtype=F32)[:, None] * inv
    by_row = lambda a: jnp.repeat(a, GRID_W, axis=0)
    by_col = lambda a: jnp.tile(a, (rows, 1))
    cos_r, sin_r = by_row(jnp.cos(ang_r)), by_row(jnp.sin(ang_r))
    cos_c, sin_c = by_col(jnp.cos(ang_c)), by_col(jnp.sin(ang_c))
    cos = jnp.concatenate([cos_r, cos_r, cos_c, cos_c], axis=-1)
    sin = jnp.concatenate([-sin_r, sin_r, -sin_c, sin_c], axis=-1)
    return cos, sin


def _t5_bucket_table():
    rel = (jnp.arange(3 * BLOCK) - BLOCK)[None, :] - jnp.arange(BLOCK)[:, None]
    half = NUM_BUCKETS // 2
    ret = jnp.where(rel > 0, half, 0)
    n = jnp.abs(rel)
    max_exact = half // 2
    nf = jnp.maximum(n, 1).astype(F32)
    large = max_exact + (jnp.log(nf / max_exact) / math.log(MAX_DISTANCE / max_exact)
                         * (half - max_exact)).astype(I32)
    large = jnp.minimum(large, half - 1)
    return (ret + jnp.where(n < max_exact, n, large)).astype(I32)


class _Tiles(NamedTuple):
    inproj_rows: int
    inproj_cols: int
    attn_a_q: int
    attn_a_k: int
    attn_b_q: int
    outproj_rows: int
    ffn_rows: int
    combine_rows128: int
    combine_window: int


def _plan_tiles(bsz, s_len, d, cap):
    in_width = (7 * d) // 2
    tiles = _Tiles(inproj_rows=min(1024, bsz * s_len), inproj_cols=in_width // 4,
                   attn_a_q=min(512, s_len), attn_a_k=min(512, s_len), attn_b_q=min(1024, s_len),
                   outproj_rows=512, ffn_rows=min(512, cap // 2), combine_rows128=2, combine_window=64)
    assert in_width % tiles.inproj_cols == 0 and tiles.inproj_cols % (2 * LANES) == 0
    assert (bsz * s_len) % tiles.inproj_rows == 0 and s_len % tiles.outproj_rows == 0
    assert s_len % tiles.attn_a_q == 0 and s_len % tiles.attn_a_k == 0 and s_len % tiles.attn_b_q == 0
    assert s_len % (tiles.combine_rows128 * LANES) == 0 and cap % LANES == 0
    assert tiles.combine_window % BF16_ROWS == 0 and tiles.combine_window <= cap
    assert SAFE_EXP2_RANGE + math.log2(s_len) + 8 < 127
    return tiles


def kernel(x, g_mix, w_in, b_gate, qn_a, kn_a, w_proj_a, sink_b, rel_bias, w_proj_b, w_o, g_ffn, w_router,
           w_gate_e, w_up_e, w_down_e, g_final):
    bsz, s_len, d = x.shape
    depth = g_mix.shape[0]
    t = bsz * s_len
    cap = CAPACITY_FACTOR * s_len // N_EXPERTS
    n_r = s_len // LANES
    tiles = _plan_tiles(bsz, s_len, d, cap)
    cos, sin = _rope_tables(s_len)
    bucket = _t5_bucket_table()
    x2d = x.reshape(t, d)
    for l in range(depth):
        proj = _inproj(x2d, g_mix[l][None, :], w_in[l].astype(BF16), tm=tiles.inproj_rows, tn=tiles.inproj_cols)
        oa = _attn_a(proj, cos, sin, qn_a[l], kn_a[l], bsz, s_len, d, tq=tiles.attn_a_q, tk=tiles.attn_a_k)
        ob = _attn_b(proj, bucket, rel_bias, sink_b[l], bsz, s_len, d, tq=tiles.attn_b_q)
        x1, h2p, aff = _outproj(oa, ob, proj, x2d, b_gate[l][None, :], w_proj_a[l].astype(BF16),
                                w_proj_b[l].astype(BF16), w_o[l].astype(BF16), g_ffn[l][None, :],
                                _split_router(w_router[l]), bsz, s_len, tm=tiles.outproj_rows)
        idx, gates, rank, sel = _topk(aff.reshape(bsz, N_EXPERTS, n_r, LANES), cap, rpt=h2p.shape[0] // t)
        y = _ffn(idx.reshape(-1), h2p, gates.reshape(bsz, N_EXPERTS, cap // LANES, LANES),
                 w_gate_e[l], w_up_e[l], w_down_e[l], bsz, s_len, cap, ch=tiles.ffn_rows)
        x2d = _combine(y.reshape(bsz * N_EXPERTS * cap, d), x1, rank.reshape(bsz, n_r, N_EXPERTS, LANES),
                       sel.reshape(bsz, n_r, N_EXPERTS, LANES), g_final[None, :], bsz, s_len, cap,
                       win=tiles.combine_window, sub=tiles.combine_rows128, final=(l == depth - 1))
    return x2d.reshape(bsz, s_len, d)
```
